```python
import math
import jax, jax.numpy as jnp
from jax import lax
import numpy as np

D_MODEL = 1024
BATCH = 8
SEQ = 2048
DEPTH = 4
DEC_BATCH = 128
DEC_SEQ = 1
PAST_LEN = 16384
PAGE_SIZE = 128

N_META = 16
W_BRANCH = 512
D_MIX = 3 * W_BRANCH
CONV_W = 4
RG_BLOCKS = 8
RG_BLOCK = W_BRANCH // RG_BLOCKS
RG_C = 8.0
RET_HEADS = 4
RET_DK = W_BRANCH // RET_HEADS
RET_CHUNK = 64
ROPE_BASE = 10000.0
GDN_HEADS = 4
GDN_DK = W_BRANCH // GDN_HEADS
GDN_CHUNK = 64
EPS = 1e-6
DEEPNORM_ALPHA = (2.0 * DEPTH) ** 0.25
DEEPNORM_BETA = (8.0 * DEPTH) ** -0.25
D_IN = 10 * W_BRANCH + 2 * GDN_HEADS

kernel_name = "hymba_rglru_retention_gdn_deepnorm_step"


def _layernorm(x, w, b):
    x = x.astype(jnp.float32)
    mu = jnp.mean(x, -1, keepdims=True)
    var = jnp.mean(jnp.square(x - mu), -1, keepdims=True)
    return (x - mu) * lax.rsqrt(var + EPS) * w + b


def _heads(t, h):
    b, s, _ = t.shape
    return t.reshape(b, s, h, -1).transpose(0, 2, 1, 3)


def _merge(t):
    b, h, s, d = t.shape
    return t.transpose(0, 2, 1, 3).reshape(b, s, h * d)


def _l2norm(t):
    return t * lax.rsqrt(jnp.sum(t * t, -1, keepdims=True) + EPS)


def _rope(t, pos):
    half = t.shape[-1] // 2
    inv = ROPE_BASE ** (-jnp.arange(half, dtype=jnp.float32) / half)
    ang = pos[:, None] * inv[None, :]
    cos, sin = jnp.cos(ang), jnp.sin(ang)
    t1, t2 = t[..., :half], t[..., half:]
    return jnp.concatenate([t1 * cos - t2 * sin, t1 * sin + t2 * cos], -1)


def _causal_conv(x, buf, w):
    t = x.shape[1]
    xp = jnp.concatenate([buf.astype(x.dtype), x], axis=1)
    y = sum(w[j].astype(jnp.float32) * xp[:, j:j + t] for j in range(CONV_W))
    return y, xp[:, -(CONV_W - 1):]


def _rglru(x, h0, w_a, b_a, w_x, b_x, lam):
    b, t, _ = x.shape
    xb = x.reshape(b, t, RG_BLOCKS, RG_BLOCK)
    r = jax.nn.sigmoid(jnp.einsum('btnc,ncd->btnd', xb, w_a.astype(jnp.float32)).reshape(b, t, W_BRANCH) + b_a)
    i = jax.nn.sigmoid(jnp.einsum('btnc,ncd->btnd', xb, w_x.astype(jnp.float32)).reshape(b, t, W_BRANCH) + b_x)
    log_a = -RG_C * r * jax.nn.softplus(-lam.astype(jnp.float32))
    a = jnp.exp(log_a)
    u = jnp.sqrt(-jnp.expm1(2.0 * log_a)) * (i * x)
    u = u.at[:, 0].add(a[:, 0] * h0.astype(jnp.float32))

    def comb(l, r_):
        a1, b1 = l
        a2, b2 = r_
        return a1 * a2, a2 * b1 + b2

    _, h = lax.associative_scan(comb, (a, u), axis=1)
    return h, h[:, -1]


def _segments(t, lead, chunk):
    segs = []
    if lead > 0:
        segs.append((lead, lead))
    rest = t - lead
    segs.append((rest, chunk if rest % chunk == 0 else rest))
    return segs


def _chunk_run(make_step, chunk, s, xs, lead):
    t = xs[0].shape[2]
    outs, start = [], 0
    for length, c in _segments(t, lead, chunk):
        n = length // c
        seg = tuple(jnp.moveaxis(a[:, :, start:start + length].reshape(a.shape[:2] + (n, c) + a.shape[3:]), 2, 0)
                    for a in xs)
        s, o = lax.scan(make_step(c), s, seg)
        o = jnp.moveaxis(o, 0, 2)
        outs.append(o.reshape(o.shape[:2] + (length,) + o.shape[4:]))
        start += length
    return jnp.concatenate(outs, axis=2), s


def _ret_step(c):
    lg = jnp.log1p(-jnp.exp2(-5.0 - jnp.arange(RET_HEADS, dtype=jnp.float32)))[:, None, None]
    idx = jnp.arange(c, dtype=jnp.float32)
    diff = idx[:, None] - idx[None, :]
    decay = jnp.where(diff >= 0, jnp.exp(lg * jnp.maximum(diff, 0.0)), 0.0)
    q_dec = jnp.exp(lg[:, 0] * (idx + 1.0))[:, :, None]
    k_dec = jnp.exp(lg[:, 0] * (c - 1.0 - idx))[:, :, None]
    s_dec = jnp.exp(lg * c)

    def step(s, xs):
        q, k, v = xs
        sc = jnp.einsum('bhid,bhjd->bhij', q, k) * decay
        o = jnp.einsum('bhij,bhjv->bhiv', sc, v) + jnp.einsum('bhid,bhdv->bhiv', q * q_dec, s)
        s = s * s_dec + jnp.einsum('bhjd,bhjv->bhdv', k * k_dec, v)
        return s, o
    return step


def _gdn_step(c):
    tril = jnp.tril(jnp.ones((c, c), bool))
    strict = jnp.tril(jnp.ones((c, c), bool), -1)
    eye = jnp.eye(c, dtype=jnp.float32)

    def step(s, xs):
        q, k, v, g, beta = xs
        gc = jnp.cumsum(g, axis=-1)
        decay = jnp.exp(jnp.where(tril, gc[..., :, None] - gc[..., None, :], -jnp.inf))
        kb = k * beta[..., None]
        a_low = jnp.where(strict, jnp.einsum('bhid,bhjd->bhij', kb, k) * decay, 0.0)
        rhs = jnp.concatenate([v * beta[..., None], kb * jnp.exp(gc)[..., None]], -1)
        sol = lax.linalg.triangular_solve(eye + a_low, rhs, left_side=True, lower=True)
        u, w = sol[..., :GDN_DK], sol[..., GDN_DK:]
        v_new = u - jnp.einsum('bhik,bhkv->bhiv', w, s)
        att = jnp.einsum('bhid,bhjd->bhij', q, k) * decay
        o = jnp.einsum('bhik,bhkv->bhiv', q * jnp.exp(gc)[..., None], s) + jnp.einsum('bhij,bhjv->bhiv', att, v_new)
        g_last = gc[..., -1:]
        s = s * jnp.exp(g_last)[..., None] + jnp.einsum('bhik,bhiv->bhkv', k * jnp.exp(g_last - gc)[..., None], v_new)
        return s, o
    return step


def _layer(x, pos, lead, h0, rg_buf, s_ret, gdn_buf, s_gdn,
           w_in, rg_conv_w, rg_conv_b, rg_w_a, rg_b_a, rg_w_x, rg_b_x, rg_lambda,
           ret_gn_w, ret_gn_b, gdn_conv_w, gdn_a_log, gdn_dt_bias, gdn_norm_w,
           w_out, ln_w, ln_b):
    f32 = jnp.float32
    W = W_BRANCH
    u = jnp.einsum('btd,de->bte', x, w_in).astype(f32)
    rg_in, rg_z = u[..., 0:W], u[..., W:2 * W]
    r_q, r_k, r_v, r_z = (u[..., (2 + j) * W:(3 + j) * W] for j in range(4))
    g_qkv, g_z = u[..., 6 * W:9 * W], u[..., 9 * W:10 * W]
    g_a, g_b = u[..., 10 * W:10 * W + GDN_HEADS], u[..., 10 * W + GDN_HEADS:]

    rg_c, rg_buf_new = _causal_conv(rg_in, rg_buf, rg_conv_w)
    rg_h, h_new = _rglru(rg_c + rg_conv_b, h0, rg_w_a, rg_b_a, rg_w_x, rg_b_x, rg_lambda)
    y_rg = rg_h * jax.nn.silu(rg_z)

    q = _rope(_heads(r_q, RET_HEADS), pos)
    k = _rope(_heads(r_k, RET_HEADS), pos) * RET_DK ** -0.5
    v = _heads(r_v, RET_HEADS)
    o, s_ret_new = _chunk_run(_ret_step, RET_CHUNK, s_ret.astype(f32), (q, k, v), lead)
    mu = jnp.mean(o, -1, keepdims=True)
    o = (o - mu) * lax.rsqrt(jnp.mean(jnp.square(o - mu), -1, keepdims=True) + EPS)
    y_ret = (_merge(o) * ret_gn_w + ret_gn_b) * jax.nn.silu(r_z)

    gc_, gdn_buf_new = _causal_conv(g_qkv, gdn_buf, gdn_conv_w)
    gc_ = jax.nn.silu(gc_)
    gq = _l2norm(_heads(gc_[..., :W], GDN_HEADS)) * GDN_DK ** -0.5
    gk = _l2norm(_heads(gc_[..., W:2 * W], GDN_HEADS))
    gv = _heads(gc_[..., 2 * W:], GDN_HEADS)
    g_log = -jnp.exp(gdn_a_log.astype(f32)) * jax.nn.softplus(g_a + gdn_dt_bias)
    beta = jax.nn.sigmoid(g_b)
    o, s_gdn_new = _chunk_run(_gdn_step, GDN_CHUNK, s_gdn.astype(f32),
                              (gq, gk, gv, g_log.transpose(0, 2, 1), beta.transpose(0, 2, 1)), lead)
    o = o * lax.rsqrt(jnp.mean(jnp.square(o), -1, keepdims=True) + EPS) * gdn_norm_w
    y_gdn = _merge(o) * jax.nn.silu(g_z)

    mix = jnp.concatenate([y_rg, y_ret, y_gdn], -1)
    out = jnp.einsum('bte,ed->btd', mix, w_out.astype(f32))
    y = _layernorm(DEEPNORM_ALPHA * x.astype(f32) + out, ln_w, ln_b).astype(x.dtype)
    return y, h_new, rg_buf_new, s_ret_new, gdn_buf_new, s_gdn_new


def setup_inputs(seed: int = 0) -> dict:
    key = jax.random.key(seed)
    ks = jax.random.split(key, 32)
    nrm = jax.random.normal
    f32 = jnp.float32
    a_init = jax.random.uniform(ks[12], (DEPTH, W_BRANCH), f32, 0.9, 0.999)
    dt = jnp.exp(jax.random.uniform(ks[16], (DEPTH, GDN_HEADS), f32, math.log(1e-3), math.log(1e-1)))
    return {
        "x_prompt": nrm(ks[0], (BATCH, SEQ, D_MODEL), f32),
        "x_sample": nrm(ks[1], (DEC_BATCH, DEC_SEQ, D_MODEL), f32),
        "state_rglru_h": 0.5 * nrm(ks[2], (DEPTH, DEC_BATCH, W_BRANCH), f32),
        "state_rglru_conv": nrm(ks[3], (DEPTH, DEC_BATCH, CONV_W - 1, W_BRANCH), f32),
        "state_ret": 0.5 * nrm(ks[4], (DEPTH, DEC_BATCH, RET_HEADS, RET_DK, RET_DK), f32),
        "state_gdn_conv": nrm(ks[5], (DEPTH, DEC_BATCH, CONV_W - 1, 3 * W_BRANCH), f32),
        "state_gdn": 0.1 * nrm(ks[6], (DEPTH, DEC_BATCH, GDN_HEADS, GDN_DK, GDN_DK), f32),
        "meta_tokens": nrm(ks[7], (N_META, D_MODEL), f32),
        "w_in": nrm(ks[8], (DEPTH, D_MODEL, D_IN), f32) * D_MODEL ** -0.5,
        "rg_conv_w": nrm(ks[9], (DEPTH, CONV_W, W_BRANCH), f32) * CONV_W ** -0.5,
        "rg_conv_b": 0.01 * nrm(ks[10], (DEPTH, W_BRANCH), f32),
        "rg_w_a": nrm(ks[11], (DEPTH, RG_BLOCKS, RG_BLOCK, RG_BLOCK), f32) * RG_BLOCK ** -0.5,
        "rg_b_a": 0.01 * nrm(ks[13], (DEPTH, W_BRANCH), f32),
        "rg_w_x": nrm(ks[14], (DEPTH, RG_BLOCKS, RG_BLOCK, RG_BLOCK), f32) * RG_BLOCK ** -0.5,
        "rg_b_x": 0.01 * nrm(ks[15], (DEPTH, W_BRANCH), f32),
        "rg_lambda": jnp.log(a_init) - jnp.log1p(-a_init),
        "ret_gn_w": 1.0 + 0.02 * nrm(ks[17], (DEPTH, W_BRANCH), f32),
        "ret_gn_b": 0.01 * nrm(ks[18], (DEPTH, W_BRANCH), f32),
        "gdn_conv_w": nrm(ks[19], (DEPTH, CONV_W, 3 * W_BRANCH), f32) * CONV_W ** -0.5,
        "gdn_a_log": jnp.log(jax.random.uniform(ks[20], (DEPTH, GDN_HEADS), f32, 1.0, 16.0)),
        "gdn_dt_bias": dt + jnp.log(-jnp.expm1(-dt)),
        "gdn_norm_w": 1.0 + 0.02 * nrm(ks[21], (DEPTH, GDN_DK), f32),
        "w_out": nrm(ks[22], (DEPTH, D_MIX, D_MODEL), f32) * (D_MIX ** -0.5) * DEEPNORM_BETA,
        "ln_w": 1.0 + 0.02 * nrm(ks[23], (DEPTH, D_MODEL), f32),
        "ln_b": 0.01 * nrm(ks[24], (DEPTH, D_MODEL), f32),
    }


def reference(x_prompt, x_sample, state_rglru_h, state_rglru_conv, state_ret, state_gdn_conv, state_gdn,
              meta_tokens, w_in, rg_conv_w, rg_conv_b, rg_w_a, rg_b_a, rg_w_x, rg_b_x, rg_lambda,
              ret_gn_w, ret_gn_b, gdn_conv_w, gdn_a_log, gdn_dt_bias, gdn_norm_w, w_out, ln_w, ln_b):
    f32 = jnp.float32
    bp = x_prompt.shape[0]
    meta = jnp.broadcast_to(meta_tokens.astype(x_prompt.dtype)[None], (bp, N_META, D_MODEL))
    xp = jnp.concatenate([meta, x_prompt], axis=1)
    xs = x_sample
    pos_p = jnp.arange(xp.shape[1], dtype=f32)
    pos_s = jnp.arange(xs.shape[1], dtype=f32) + float(PAST_LEN)
    new_p = [[], [], [], [], []]
    new_s = [[], [], [], [], []]
    for l in range(DEPTH):
        params = (w_in[l], rg_conv_w[l], rg_conv_b[l], rg_w_a[l], rg_b_a[l], rg_w_x[l], rg_b_x[l], rg_lambda[l],
                  ret_gn_w[l], ret_gn_b[l], gdn_conv_w[l], gdn_a_log[l], gdn_dt_bias[l], gdn_norm_w[l],
                  w_out[l], ln_w[l], ln_b[l])
        xp, *st_p = _layer(xp, pos_p, N_META,
                           jnp.zeros((bp, W_BRANCH), f32),
                           jnp.zeros((bp, CONV_W - 1, W_BRANCH), f32),
                           jnp.zeros((bp, RET_HEADS, RET_DK, RET_DK), f32),
                           jnp.zeros((bp, CONV_W - 1, 3 * W_BRANCH), f32),
                           jnp.zeros((bp, GDN_HEADS, GDN_DK, GDN_DK), f32),
                           *params)
        xs, *st_s = _layer(xs, pos_s, 0, state_rglru_h[l], state_rglru_conv[l], state_ret[l],
                           state_gdn_conv[l], state_gdn[l], *params)
        for j in range(5):
            new_p[j].append(st_p[j])
            new_s[j].append(st_s[j])
    y_prompt = xp[:, N_META:]
    y_sample = xs
    dts = (state_rglru_h.dtype, state_rglru_conv.dtype, state_ret.dtype, state_gdn_conv.dtype, state_gdn.dtype)
    sp = [jnp.stack(new_p[j]).astype(dts[j]) for j in range(5)]
    ss = [jnp.stack(new_s[j]).astype(dts[j]) for j in range(5)]
    return (y_prompt, y_sample, sp[0], sp[1], sp[2], sp[3], sp[4], ss[0], ss[1], ss[2], ss[3], ss[4])
```

```python
import functools
import math

import jax
import jax.numpy as jnp
from jax import lax
from jax.experimental import pallas as pl
from jax.experimental.pallas import tpu as pltpu

F32 = jnp.float32
BF16 = jnp.bfloat16

D_MODEL = 1024
DEPTH = 4
N_META = 16
PAST_LEN = 16384
W = 512
CONV_W = 4
RG_BLOCKS = 8
RG_C = 8.0
HEADS = 4
DK = W // HEADS
ROPE_BASE = 10000.0
EPS = 1e-6
ALPHA = (2.0 * DEPTH) ** 0.25
D_MAIN = 10 * W
LANES = 128
SUBLANES = 8
HIST = SUBLANES

C_RGX, C_RGZ, C_RQ, C_RK, C_RV, C_RZ, C_GQ, C_GK, C_GV, C_GZ = (i * W for i in range(10))


def _bdot(a, b):
    return jnp.dot(a.astype(BF16), b.astype(BF16), preferred_element_type=F32)


def _bdot_nt(a, b):
    return lax.dot_general(a.astype(BF16), b.astype(BF16), (((1,), (1,)), ((), ())), preferred_element_type=F32)


def _bdot_tn(a, b):
    return lax.dot_general(a.astype(BF16), b.astype(BF16), (((0,), (0,)), ((), ())), preferred_element_type=F32)


def _fdot(a, b):
    return jnp.dot(a, b, precision=lax.Precision.HIGHEST, preferred_element_type=F32)


def _silu(x):
    return x * jax.nn.sigmoid(x)


def _scan_rows(a, b):
    n = a.shape[0]
    rows = lax.broadcasted_iota(jnp.int32, a.shape, 0)
    s = 1
    while s < n:
        a_s = pltpu.roll(a, s, axis=0)
        b_s = pltpu.roll(b, s, axis=0)
        m = rows >= s
        b = jnp.where(m, a * b_s + b, b)
        a = jnp.where(m, a * a_s, a)
        s *= 2
    return a, b


def _prompt_layer_kernel(
        x_ref, cos_ref, sin_ref,
        h0_ref, rgb0_ref, sret0_ref, gb0_ref, sgdn0_ref,
        win_ref, wab_ref, wg_ref, bg_ref, lam_ref, rcw_ref, rcb_ref, gnw_ref, gnb_ref,
        gcw_ref, alog_ref, dtb_ref, gnorm_ref, wout_ref, lnw_ref, lnb_ref,
        rdecay_ref, rqdec_ref, rkdec_ref, rsdec_ref,
        y_ref, h_ref, rgb_ref, sret_ref, gb_ref, sgdn_ref,
        u_ref, ab_ref, xc_ref, mix_ref,
        *, tb, c_ret, c_gdn):
    t = pl.program_id(1)

    @pl.when(t == 0)
    def _init():
        h_ref[...] = h0_ref[...]
        sret_ref[...] = sret0_ref[...]
        sgdn_ref[...] = sgdn0_ref[...]
        u_ref[HIST - 3:HIST, C_RGX:C_RGX + W] = rgb0_ref[...]
        u_ref[HIST - 3:HIST, C_GQ:C_GQ + 3 * W] = gb0_ref[...]

    x = x_ref[...]
    xb = x.astype(BF16)
    u_ref[HIST:HIST + tb, :] = jnp.dot(xb, win_ref[...], preferred_element_type=F32)
    ab_ref[...] = jnp.dot(xb, wab_ref[...], preferred_element_type=F32)

    for j in range(W // LANES):
        cs = slice(C_RGX + j * LANES, C_RGX + (j + 1) * LANES)
        ws = slice(j * LANES, (j + 1) * LANES)
        acc = rcb_ref[:, ws]
        for k in range(CONV_W):
            acc = acc + rcw_ref[k:k + 1, ws] * u_ref[HIST - 3 + k:HIST - 3 + k + tb, cs]
        tail = u_ref[HIST + tb - 3:HIST + tb, cs]
        u_ref[HIST - 3:HIST, cs] = tail
        rgb_ref[:, ws] = tail
        xc_ref[:, ws] = acc
    gates = jnp.dot(xc_ref[...].astype(BF16), wg_ref[...], preferred_element_type=F32) + bg_ref[...]
    for j in range(W // LANES):
        zs = slice(C_RGZ + j * LANES, C_RGZ + (j + 1) * LANES)
        ws = slice(j * LANES, (j + 1) * LANES)
        r = jax.nn.sigmoid(gates[:, j * LANES:(j + 1) * LANES])
        i = jax.nn.sigmoid(gates[:, W + j * LANES:W + (j + 1) * LANES])
        log_a = (-RG_C) * r * jax.nn.softplus(-lam_ref[:, ws])
        a = jnp.exp(log_a)
        b = jnp.sqrt(1.0 - a * a) * (i * xc_ref[:, ws])
        a_cum, hloc = _scan_rows(a, b)
        h = hloc + a_cum * h_ref[:, ws]
        h_ref[:, ws] = h[tb - 1:tb, :]
        mix_ref[:, ws] = (h * _silu(u_ref[HIST:HIST + tb, zs])).astype(mix_ref.dtype)

    cosf = cos_ref[...]
    sinf = sin_ref[...]
    for hd in range(HEADS):
        qs = slice(C_RQ + hd * DK, C_RQ + (hd + 1) * DK)
        ks = slice(C_RK + hd * DK, C_RK + (hd + 1) * DK)
        q = u_ref[HIST:HIST + tb, qs]
        k = u_ref[HIST:HIST + tb, ks]
        u_ref[HIST:HIST + tb, qs] = q * cosf + pltpu.roll(q, DK // 2, axis=1) * sinf
        u_ref[HIST:HIST + tb, ks] = (k * cosf + pltpu.roll(k, DK // 2, axis=1) * sinf) * (DK ** -0.5)

    def ret_chunk(c, carry):
        r0 = pl.multiple_of(c * c_ret, SUBLANES)
        rows = pl.ds(HIST + r0, c_ret)
        for hd in range(HEADS):
            q = u_ref[rows, C_RQ + hd * DK:C_RQ + (hd + 1) * DK]
            k = u_ref[rows, C_RK + hd * DK:C_RK + (hd + 1) * DK]
            v = u_ref[rows, C_RV + hd * DK:C_RV + (hd + 1) * DK]
            z = u_ref[rows, C_RZ + hd * DK:C_RZ + (hd + 1) * DK]
            s = sret_ref[hd]
            sc = _bdot_nt(q, k) * rdecay_ref[hd]
            o = _bdot(sc, v) + _bdot(q * rqdec_ref[hd], s)
            sret_ref[hd] = s * rsdec_ref[hd] + _bdot_tn(k * rkdec_ref[hd], v)
            mu = jnp.mean(o, axis=-1, keepdims=True)
            d = o - mu
            on = d * lax.rsqrt(jnp.mean(d * d, axis=-1, keepdims=True) + EPS)
            gs = slice(hd * DK, (hd + 1) * DK)
            y = (on * gnw_ref[:, gs] + gnb_ref[:, gs]) * _silu(z)
            mix_ref[pl.ds(r0, c_ret), W + hd * DK:W + (hd + 1) * DK] = y.astype(mix_ref.dtype)
        return carry

    lax.fori_loop(0, tb // c_ret, ret_chunk, 0)

    for j in range(3 * W // LANES):
        cs = slice(C_GQ + j * LANES, C_GQ + (j + 1) * LANES)
        ws = slice(j * LANES, (j + 1) * LANES)
        acc = gcw_ref[0:1, ws] * u_ref[HIST - 3:HIST - 3 + tb, cs]
        for k in range(1, CONV_W):
            acc = acc + gcw_ref[k:k + 1, ws] * u_ref[HIST - 3 + k:HIST - 3 + k + tb, cs]
        tail = u_ref[HIST + tb - 3:HIST + tb, cs]
        u_ref[HIST - 3:HIST, cs] = tail
        gb_ref[:, ws] = tail
        u_ref[HIST:HIST + tb, cs] = _silu(acc)

    abv = ab_ref[...]
    glog = -jnp.exp(alog_ref[...]) * jax.nn.softplus(abv + dtb_ref[...])
    beta_all = jax.nn.sigmoid(abv)
    ab_ref[...] = glog
    xc_ref[:, 0:LANES] = beta_all

    ri = lax.broadcasted_iota(jnp.int32, (c_gdn, c_gdn), 0)
    ci = lax.broadcasted_iota(jnp.int32, (c_gdn, c_gdn), 1)
    tril = ri >= ci
    strict = ri > ci
    tri_f = jnp.where(tril, 1.0, 0.0).astype(F32)
    eye_f = jnp.where(ri == ci, 1.0, 0.0).astype(F32)
    n_sq = max(int(math.ceil(math.log2(c_gdn))) - 1, 0)

    def gdn_chunk(c, carry):
        r0 = pl.multiple_of(c * c_gdn, SUBLANES)
        rows = pl.ds(HIST + r0, c_gdn)
        gcs = _fdot(tri_f, ab_ref[pl.ds(r0, c_gdn), :])
        bt = xc_ref[pl.ds(r0, c_gdn), 0:LANES]
        for hd in range(HEADS):
            q = u_ref[rows, C_GQ + hd * DK:C_GQ + (hd + 1) * DK]
            k = u_ref[rows, C_GK + hd * DK:C_GK + (hd + 1) * DK]
            v = u_ref[rows, C_GV + hd * DK:C_GV + (hd + 1) * DK]
            z = u_ref[rows, C_GZ + hd * DK:C_GZ + (hd + 1) * DK]
            q = q * lax.rsqrt(jnp.sum(q * q, axis=-1, keepdims=True) + EPS) * (DK ** -0.5)
            k = k * lax.rsqrt(jnp.sum(k * k, axis=-1, keepdims=True) + EPS)
            gc = jnp.broadcast_to(gcs[:, hd:hd + 1], (c_gdn, LANES))
            beta = jnp.broadcast_to(bt[:, HEADS + hd:HEADS + hd + 1], (c_gdn, LANES))
            diff = gc[:, :c_gdn] - gc.T[:c_gdn, :]
            decay = jnp.where(tril, jnp.exp(jnp.where(tril, diff, 0.0)), 0.0)
            egc = jnp.exp(gc)
            kb = k * beta
            a_low = jnp.where(strict, _bdot_nt(kb, k) * decay, 0.0)
            p = a_low
            tinv = eye_f - a_low
            for _ in range(n_sq):
                p = _fdot(p, p)
                tinv = tinv + _fdot(tinv, p)
            rhs = jnp.concatenate([v * beta, kb * egc], axis=-1)
            sol = _fdot(tinv, rhs)
            uu, ww = sol[:, :DK], sol[:, DK:]
            s = sgdn_ref[hd]
            v_new = uu - _bdot(ww, s)
            att = _bdot_nt(q, k) * decay
            o = _bdot(q * egc, s) + _bdot(att, v_new)
            g_last = gc[c_gdn - 1:c_gdn, :]
            sgdn_ref[hd] = s * jnp.exp(g_last) + _bdot_tn(k * jnp.exp(g_last - gc), v_new)
            on = o * lax.rsqrt(jnp.mean(o * o, axis=-1, keepdims=True) + EPS) * gnorm_ref[...]
            mix_ref[pl.ds(r0, c_gdn), 2 * W + hd * DK:2 * W + (hd + 1) * DK] = (on * _silu(z)).astype(mix_ref.dtype)
        return carry

    lax.fori_loop(0, tb // c_gdn, gdn_chunk, 0)

    out = jnp.dot(mix_ref[...].astype(BF16), wout_ref[...], preferred_element_type=F32)
    r = ALPHA * x + out
    mu = jnp.mean(r, axis=-1, keepdims=True)
    d = r - mu
    var = jnp.mean(d * d, axis=-1, keepdims=True)
    y_ref[...] = d * lax.rsqrt(var + EPS) * lnw_ref[...] + lnb_ref[...]


def _ret_tables(c):
    lg = jnp.log1p(-jnp.exp2(-5.0 - jnp.arange(HEADS, dtype=F32)))[:, None, None]
    idx = jnp.arange(c, dtype=F32)
    diff = idx[:, None] - idx[None, :]
    decay = jnp.where(diff >= 0, jnp.exp(lg * jnp.maximum(diff, 0.0)), 0.0)
    q_dec = jnp.broadcast_to(jnp.exp(lg[:, 0] * (idx + 1.0))[:, :, None], (HEADS, c, DK))
    k_dec = jnp.broadcast_to(jnp.exp(lg[:, 0] * (c - 1.0 - idx))[:, :, None], (HEADS, c, DK))
    s_dec = jnp.broadcast_to(jnp.exp(lg * c), (HEADS, 1, DK))
    return decay, q_dec, k_dec, s_dec


def _rope_tables(pos):
    half = DK // 2
    inv = ROPE_BASE ** (-jnp.arange(half, dtype=F32) / half)
    ang = pos[:, None] * inv[None, :]
    cos, sin = jnp.cos(ang), jnp.sin(ang)
    return jnp.concatenate([cos, cos], -1), jnp.concatenate([-sin, sin], -1)


def _prompt_layer(layer, x, pos, init, wts, *, tb, c_ret, c_gdn):
    bsz, tlen, _ = x.shape
    assert tlen % tb == 0 and tb % c_ret == 0 and tb % c_gdn == 0
    nt = tlen // tb
    cos2, sin2 = _rope_tables(pos)
    rtabs = _ret_tables(c_ret)

    def wspec(a):
        nd = a.ndim - 1
        return pl.BlockSpec((None,) + a.shape[1:], lambda b, t, _n=nd: (layer,) + (0,) * _n,
                            pipeline_mode=pl.Buffered(1))

    def cspec(a):
        nd = a.ndim
        return pl.BlockSpec(a.shape, lambda b, t, _n=nd: (0,) * _n, pipeline_mode=pl.Buffered(1))

    def ispec(a):
        nd = a.ndim - 1
        return pl.BlockSpec((None,) + a.shape[1:], lambda b, t, _n=nd: (0,) * (_n + 1),
                            pipeline_mode=pl.Buffered(1))

    def ospec(shape):
        nd = len(shape)
        return pl.BlockSpec((None,) + shape, lambda b, t, _n=nd: (b,) + (0,) * _n)

    in_specs = ([pl.BlockSpec((None, tb, D_MODEL), lambda b, t: (b, t, 0)),
                 pl.BlockSpec((tb, DK), lambda b, t: (t, 0)),
                 pl.BlockSpec((tb, DK), lambda b, t: (t, 0))]
                + [ispec(a) for a in init] + [wspec(a) for a in wts] + [cspec(a) for a in rtabs])
    out_shape = (jax.ShapeDtypeStruct((bsz, tlen, D_MODEL), F32),
                 jax.ShapeDtypeStruct((bsz, 1, W), F32),
                 jax.ShapeDtypeStruct((bsz, CONV_W - 1, W), F32),
                 jax.ShapeDtypeStruct((bsz, HEADS, DK, DK), F32),
                 jax.ShapeDtypeStruct((bsz, CONV_W - 1, 3 * W), F32),
                 jax.ShapeDtypeStruct((bsz, HEADS, DK, DK), F32))
    out_specs = (pl.BlockSpec((None, tb, D_MODEL), lambda b, t: (b, t, 0)),
                 ospec((1, W)), ospec((CONV_W - 1, W)), ospec((HEADS, DK, DK)),
                 ospec((CONV_W - 1, 3 * W)), ospec((HEADS, DK, DK)))
    scratch = [pltpu.VMEM((HIST + tb, D_MAIN), F32),
               pltpu.VMEM((tb, LANES), F32),
               pltpu.VMEM((tb, W), F32),
               pltpu.VMEM((tb, 3 * W), F32)]
    kern = functools.partial(_prompt_layer_kernel, tb=tb, c_ret=c_ret, c_gdn=c_gdn)
    return pl.pallas_call(
        kern, grid=(bsz, nt), in_specs=in_specs, out_specs=out_specs, out_shape=out_shape,
        scratch_shapes=scratch,
        compiler_params=pltpu.CompilerParams(dimension_semantics=("arbitrary", "arbitrary"),
                                             vmem_limit_bytes=56 * 1024 * 1024),
        name=f"prompt_layer{layer}_t{tlen}",
    )(x, cos2, sin2, *init, *wts, *rtabs)


def _prep_weights(w_in, rg_conv_w, rg_conv_b, rg_w_a, rg_b_a, rg_w_x, rg_b_x, rg_lambda,
                  ret_gn_w, ret_gn_b, gdn_conv_w, gdn_a_log, gdn_dt_bias, gdn_norm_w, w_out, ln_w, ln_b):
    eye = jnp.eye(RG_BLOCKS, dtype=F32)

    def bdiag(w):
        l, n, c, d = w.shape
        return jnp.einsum('lncd,nm->lncmd', w.astype(F32), eye).reshape(l, n * c, n * d)

    pad = LANES - 2 * HEADS
    w_main = w_in[:, :, :D_MAIN].astype(BF16)
    w_ab = jnp.pad(w_in[:, :, D_MAIN:], ((0, 0), (0, 0), (0, pad))).astype(BF16)
    wg = jnp.concatenate([bdiag(rg_w_a), bdiag(rg_w_x)], axis=-1).astype(BF16)
    bg = jnp.concatenate([rg_b_a, rg_b_x], axis=-1)[:, None, :].astype(F32)
    row = lambda a: a[:, None, :].astype(F32)
    padh = lambda a: jnp.pad(a.astype(F32), ((0, 0), (0, LANES - HEADS)))[:, None, :]
    return (w_main, w_ab, wg, bg, row(rg_lambda), rg_conv_w.astype(F32), row(rg_conv_b),
            row(ret_gn_w), row(ret_gn_b), gdn_conv_w.astype(F32), padh(gdn_a_log), padh(gdn_dt_bias),
            row(gdn_norm_w), w_out.astype(BF16), row(ln_w), row(ln_b))


def _sample_kernel(
        x_ref, cos_ref, sin_ref, gam_ref,
        h0_ref, rgb0_ref, sret0_ref, gb0_ref, sgdn0_ref,
        win_ref, wab_ref, wg_ref, bg_ref, lam_ref, rcw_ref, rcb_ref, gnw_ref, gnb_ref,
        gcw_ref, alog_ref, dtb_ref, gnorm_ref, wout_ref, lnw_ref, lnb_ref,
        y_ref, h_ref, rgb_ref, sret_ref, gb_ref, sgdn_ref,
        xcur_ref, u_ref, eg_ref, beta_ref, mix_ref,
        *, bb_rows, n_bb):
    layer = pl.program_id(0)
    bb = pl.program_id(1)
    nb = x_ref.shape[0]

    @pl.when(jnp.logical_and(layer == 0, bb == 0))
    def _load_x():
        xcur_ref[...] = x_ref[...]

    @pl.when(bb == 0)
    def _project():
        xb = xcur_ref[...].astype(BF16)
        u_ref[...] = jnp.dot(xb, win_ref[...], preferred_element_type=F32)
        ab = jnp.dot(xb, wab_ref[...], preferred_element_type=F32)
        eg_ref[...] = jnp.exp(-jnp.exp(alog_ref[...]) * jax.nn.softplus(ab + dtb_ref[...]))
        beta_ref[...] = jax.nn.sigmoid(ab)

        cur = u_ref[:, C_RGX:C_RGX + W]
        xc = rcb_ref[...] + rcw_ref[CONV_W - 1:CONV_W, :] * cur
        for k in range(CONV_W - 1):
            xc = xc + rcw_ref[k:k + 1, :] * rgb0_ref[:, k * W:(k + 1) * W]
        rgb_ref[:, 0:W] = rgb0_ref[:, W:2 * W]
        rgb_ref[:, W:2 * W] = rgb0_ref[:, 2 * W:3 * W]
        rgb_ref[:, 2 * W:3 * W] = cur
        gates = jnp.dot(xc.astype(BF16), wg_ref[...], preferred_element_type=F32) + bg_ref[...]
        r = jax.nn.sigmoid(gates[:, :W])
        i = jax.nn.sigmoid(gates[:, W:])
        log_a = (-RG_C) * r * jax.nn.softplus(-lam_ref[...])
        a = jnp.exp(log_a)
        h = a * h0_ref[...] + jnp.sqrt(1.0 - a * a) * (i * xc)
        h_ref[...] = h
        mix_ref[:, 0:W] = h * _silu(u_ref[:, C_RGZ:C_RGZ + W])

        cosf = cos_ref[...]
        sinf = sin_ref[...]
        for hd in range(HEADS):
            qs = slice(C_RQ + hd * DK, C_RQ + (hd + 1) * DK)
            ks = slice(C_RK + hd * DK, C_RK + (hd + 1) * DK)
            q = u_ref[:, qs]
            k = u_ref[:, ks]
            u_ref[:, qs] = q * cosf + pltpu.roll(q, DK // 2, axis=1) * sinf
            u_ref[:, ks] = (k * cosf + pltpu.roll(k, DK // 2, axis=1) * sinf) * (DK ** -0.5)

        n3 = 3 * W
        for j in range(n3 // LANES):
            cs = slice(C_GQ + j * LANES, C_GQ + (j + 1) * LANES)
            ws = slice(j * LANES, (j + 1) * LANES)
            cur = u_ref[:, cs]
            acc = gcw_ref[CONV_W - 1:CONV_W, ws] * cur
            for k in range(CONV_W - 1):
                acc = acc + gcw_ref[k:k + 1, ws] * gb0_ref[:, k * n3 + j * LANES:k * n3 + (j + 1) * LANES]
            gb_ref[:, j * LANES:(j + 1) * LANES] = gb0_ref[:, n3 + j * LANES:n3 + (j + 1) * LANES]
            gb_ref[:, n3 + j * LANES:n3 + (j + 1) * LANES] = gb0_ref[:, 2 * n3 + j * LANES:2 * n3 + (j + 1) * LANES]
            gb_ref[:, 2 * n3 + j * LANES:2 * n3 + (j + 1) * LANES] = cur
            y = _silu(acc)
            if j < 2 * HEADS:
                y = y * lax.rsqrt(jnp.sum(y * y, axis=-1, keepdims=True) + EPS)
                if j < HEADS:
                    y = y * (DK ** -0.5)
            u_ref[:, cs] = y

    r0 = pl.multiple_of(bb * bb_rows, SUBLANES)
    rows = pl.ds(r0, bb_rows)
    egb = eg_ref[rows, :]
    btb = beta_ref[rows, :]
    for hd in range(HEADS):
        q = u_ref[rows, C_RQ + hd * DK:C_RQ + (hd + 1) * DK]
        k = u_ref[rows, C_RK + hd * DK:C_RK + (hd + 1) * DK]
        v = u_ref[rows, C_RV + hd * DK:C_RV + (hd + 1) * DK]
        z = u_ref[rows, C_RZ + hd * DK:C_RZ + (hd + 1) * DK]
        qk = jnp.sum(q * k, axis=-1, keepdims=True)
        qt = q.T
        kt = k.T
        gam = gam_ref[hd]
        o_rows = []
        for i in range(bb_rows):
            s = sret0_ref[i, hd]
            qc = jnp.broadcast_to(qt[:, i:i + 1], (DK, DK))
            kc = jnp.broadcast_to(kt[:, i:i + 1], (DK, DK))
            qs_ = jnp.sum(qc * s, axis=0, keepdims=True)
            o_rows.append(qk[i:i + 1, :] * v[i:i + 1, :] + gam * qs_)
            sret_ref[i, hd] = gam * s + kc * v[i:i + 1, :]
        o = jnp.concatenate(o_rows, axis=0)
        mu = jnp.mean(o, axis=-1, keepdims=True)
        d = o - mu
        on = d * lax.rsqrt(jnp.mean(d * d, axis=-1, keepdims=True) + EPS)
        gs = slice(hd * DK, (hd + 1) * DK)
        mix_ref[rows, W + hd * DK:W + (hd + 1) * DK] = (on * gnw_ref[:, gs] + gnb_ref[:, gs]) * _silu(z)

        q = u_ref[rows, C_GQ + hd * DK:C_GQ + (hd + 1) * DK]
        k = u_ref[rows, C_GK + hd * DK:C_GK + (hd + 1) * DK]
        v = u_ref[rows, C_GV + hd * DK:C_GV + (hd + 1) * DK]
        z = u_ref[rows, C_GZ + hd * DK:C_GZ + (hd + 1) * DK]
        qk = jnp.sum(q * k, axis=-1, keepdims=True)
        qt = q.T
        kt = k.T
        eg = jnp.broadcast_to(egb[:, hd:hd + 1], (bb_rows, DK))
        beta = jnp.broadcast_to(btb[:, HEADS + hd:HEADS + hd + 1], (bb_rows, DK))
        o_rows = []
        for i in range(bb_rows):
            s = sgdn0_ref[i, hd]
            qc = jnp.broadcast_to(qt[:, i:i + 1], (DK, DK))
            kc = jnp.broadcast_to(kt[:, i:i + 1], (DK, DK))
            ks_ = jnp.sum(kc * s, axis=0, keepdims=True)
            qs_ = jnp.sum(qc * s, axis=0, keepdims=True)
            eg_i = eg[i:i + 1, :]
            v_new = beta[i:i + 1, :] * (v[i:i + 1, :] - eg_i * ks_)
            o_rows.append(eg_i * qs_ + qk[i:i + 1, :] * v_new)
            sgdn_ref[i, hd] = s * eg_i + kc * v_new
        o = jnp.concatenate(o_rows, axis=0)
        on = o * lax.rsqrt(jnp.mean(o * o, axis=-1, keepdims=True) + EPS) * gnorm_ref[...]
        mix_ref[rows, 2 * W + hd * DK:2 * W + (hd + 1) * DK] = on * _silu(z)

    @pl.when(bb == n_bb - 1)
    def _finish():
        out = jnp.dot(mix_ref[...].astype(BF16), wout_ref[...], preferred_element_type=F32)
        r = ALPHA * xcur_ref[...] + out
        mu = jnp.mean(r, axis=-1, keepdims=True)
        d = r - mu
        var = jnp.mean(d * d, axis=-1, keepdims=True)
        y = d * lax.rsqrt(var + EPS) * lnw_ref[...] + lnb_ref[...]
        xcur_ref[...] = y
        y_ref[...] = y


def _sample_path(x, h0, rgb0, sret0, gb0, sgdn0, wts, *, bb_rows):
    nb = x.shape[0]
    n_bb = nb // bb_rows
    pos = jnp.arange(1, dtype=F32) + float(PAST_LEN)
    cos2, sin2 = _rope_tables(pos)
    lg = jnp.log1p(-jnp.exp2(-5.0 - jnp.arange(HEADS, dtype=F32)))
    gam = jnp.broadcast_to(jnp.exp(lg)[:, None, None], (HEADS, 1, DK))

    def const(a):
        nd = a.ndim
        return pl.BlockSpec(a.shape, lambda l, b, _n=nd: (0,) * _n)

    def per_layer(a, prefetch=False):
        nd = a.ndim - 1
        return pl.BlockSpec((None,) + a.shape[1:], lambda l, b, _n=nd: (l,) + (0,) * _n,
                            pipeline_mode=pl.Buffered(2 if prefetch else 1))

    def per_block(a):
        return pl.BlockSpec((None, bb_rows) + a.shape[2:], lambda l, b: (l, b, 0, 0, 0))

    ins = (x, cos2, sin2, gam, h0, rgb0, sret0, gb0, sgdn0) + tuple(wts)
    in_specs = ([const(x), const(cos2), const(sin2), const(gam),
                 per_layer(h0), per_layer(rgb0), per_block(sret0), per_layer(gb0), per_block(sgdn0)]
                + [per_layer(a, prefetch=(i == 0)) for i, a in enumerate(wts)])
    out_shape = (jax.ShapeDtypeStruct(x.shape, F32),
                 jax.ShapeDtypeStruct(h0.shape, F32), jax.ShapeDtypeStruct(rgb0.shape, F32),
                 jax.ShapeDtypeStruct(sret0.shape, F32), jax.ShapeDtypeStruct(gb0.shape, F32),
                 jax.ShapeDtypeStruct(sgdn0.shape, F32))
    out_specs = (const(x), per_layer(h0), per_layer(rgb0), per_block(sret0), per_layer(gb0), per_block(sgdn0))
    scratch = [pltpu.VMEM((nb, D_MODEL), F32),
               pltpu.VMEM((nb, D_MAIN), F32),
               pltpu.VMEM((nb, LANES), F32),
               pltpu.VMEM((nb, LANES), F32),
               pltpu.VMEM((nb, 3 * W), F32)]
    kern = functools.partial(_sample_kernel, bb_rows=bb_rows, n_bb=n_bb)
    return pl.pallas_call(
        kern, grid=(DEPTH, n_bb), in_specs=in_specs, out_specs=out_specs, out_shape=out_shape,
        scratch_shapes=scratch,
        compiler_params=pltpu.CompilerParams(dimension_semantics=("arbitrary", "arbitrary"),
                                             vmem_limit_bytes=58 * 1024 * 1024),
        name="sample_path",
    )(*ins)


def kernel(x_prompt, x_sample, state_rglru_h, state_rglru_conv, state_ret, state_gdn_conv, state_gdn,
           meta_tokens, w_in, rg_conv_w, rg_conv_b, rg_w_a, rg_b_a, rg_w_x, rg_b_x, rg_lambda,
           ret_gn_w, ret_gn_b, gdn_conv_w, gdn_a_log, gdn_dt_bias, gdn_norm_w, w_out, ln_w, ln_b):
    bp, seq, _ = x_prompt.shape
    nb = x_sample.shape[0]
    wts = _prep_weights(w_in, rg_conv_w, rg_conv_b, rg_w_a, rg_b_a, rg_w_x, rg_b_x, rg_lambda,
                        ret_gn_w, ret_gn_b, gdn_conv_w, gdn_a_log, gdn_dt_bias, gdn_norm_w, w_out, ln_w, ln_b)

    pos = jnp.arange(N_META + seq, dtype=F32)
    zeros = lambda *s: jnp.zeros(s, F32)
    init0 = (zeros(1, 1, W), zeros(1, CONV_W - 1, W), zeros(1, HEADS, DK, DK),
             zeros(1, CONV_W - 1, 3 * W), zeros(1, HEADS, DK, DK))
    xm = meta_tokens.astype(x_prompt.dtype)[None]
    xp = x_prompt
    new_p = [[] for _ in range(5)]
    for l in range(DEPTH):
        xm, *st_m = _prompt_layer(l, xm, pos[:N_META], init0, wts, tb=N_META, c_ret=N_META, c_gdn=N_META)
        xp, *st_p = _prompt_layer(l, xp, pos[N_META:], tuple(st_m), wts, tb=256, c_ret=256, c_gdn=64)
        for j in range(5):
            new_p[j].append(st_p[j])
    sp = [jnp.stack(a) for a in new_p]
    sp[0] = sp[0].reshape(DEPTH, bp, W)

    ys, sh, srgb, sret, sgb, sgdn = _sample_path(
        x_sample.reshape(nb, D_MODEL), state_rglru_h,
        state_rglru_conv.reshape(DEPTH, nb, (CONV_W - 1) * W), state_ret,
        state_gdn_conv.reshape(DEPTH, nb, (CONV_W - 1) * 3 * W), state_gdn, wts, bb_rows=8)
    return (xp, ys.reshape(x_sample.shape), sp[0], sp[1], sp[2], sp[3], sp[4],
            sh, srgb.reshape(state_rglru_conv.shape), sret, sgb.reshape(state_gdn_conv.shape), sgdn)
```

```python
import functools
import math

import jax
import jax.numpy as jnp
from jax import lax
from jax.experimental import pallas as pl
from jax.experimental.pallas import tpu as pltpu

F32 = jnp.float32
BF16 = jnp.bfloat16

D_MODEL = 1024
DEPTH = 4
N_META = 16
PAST_LEN = 16384
W = 512
CONV_W = 4
RG_BLOCKS = 8
RG_C = 8.0
HEADS = 4
DK = W // HEADS
ROPE_BASE = 10000.0
EPS = 1e-6
ALPHA = (2.0 * DEPTH) ** 0.25
D_MAIN = 10 * W
LANES = 128
SUBLANES = 8
N_CONV = 4 * W

C_RGX, C_RGZ, C_RQ, C_RK, C_RV, C_RZ, C_GQ, C_GK, C_GV, C_GZ = (i * W for i in range(10))


def _bdot(a, b):
    return jnp.dot(a.astype(BF16), b.astype(BF16), preferred_element_type=F32)


def _bdot_nt(a, b):
    return lax.dot_general(a.astype(BF16), b.astype(BF16), (((1,), (1,)), ((), ())), preferred_element_type=F32)


def _bdot_tn(a, b):
    return lax.dot_general(a.astype(BF16), b.astype(BF16), (((0,), (0,)), ((), ())), preferred_element_type=F32)


def _bmm(a, b):
    return jnp.einsum('nij,njk->nik', a.astype(BF16), b.astype(BF16), preferred_element_type=F32)


def _bmm_nt(a, b):
    return jnp.einsum('nik,njk->nij', a.astype(BF16), b.astype(BF16), preferred_element_type=F32)


def _silu(x):
    return x * jax.nn.sigmoid(x)


def _scan_rows(a, b):
    n = a.shape[0]
    rows = lax.broadcasted_iota(jnp.int32, a.shape, 0)
    s = 1
    while s < n:
        a_s = pltpu.roll(a, s, axis=0)
        b_s = pltpu.roll(b, s, axis=0)
        m = rows >= s
        b = jnp.where(m, a * b_s + b, b)
        a = jnp.where(m, a * a_s, a)
        s *= 2
    return a, b


def _causal_conv_strip(x, hist, taps):
    r8 = lax.broadcasted_iota(jnp.int32, hist.shape, 0)
    acc = taps[CONV_W - 1] * x
    for d in range(1, CONV_W):
        xs = pltpu.roll(x, d, axis=0)
        top = jnp.where(r8 < d, pltpu.roll(hist, d, axis=0), xs[0:SUBLANES])
        acc = acc + taps[CONV_W - 1 - d] * jnp.concatenate([top, xs[SUBLANES:]], axis=0)
    return acc


def _prompt_layer_kernel(
        x_ref, cos_ref, sin_ref,
        h0_ref, rgb0_ref, sret0_ref, gb0_ref, sgdn0_ref,
        win_ref, wab_ref, wg_ref, bg_ref, lam_ref, rcw_ref, rcb_ref, gnw_ref, gnb_ref,
        gcw_ref, alog_ref, dtb_ref, gnorm_ref, wout_ref, lnw_ref, lnb_ref,
        rdecay_ref, rqdec_ref, rkdec_ref, rsdec_ref,
        y_ref, h_ref, rgb_ref, sret_ref, gb_ref, sgdn_ref,
        u_ref, hist_ref, xc_ref, mix_ref,
        *, tb, c_ret, c_gdn):
    t = pl.program_id(1)

    @pl.when(t == 0)
    def _init():
        h_ref[...] = h0_ref[...]
        sret_ref[...] = sret0_ref[...]
        sgdn_ref[...] = sgdn0_ref[...]
        hist_ref[SUBLANES - 3:SUBLANES, 0:W] = rgb0_ref[...]
        hist_ref[SUBLANES - 3:SUBLANES, W:N_CONV] = gb0_ref[...]

    x = x_ref[...]
    xb = x.astype(BF16)
    u_ref[...] = jnp.dot(xb, win_ref[...], preferred_element_type=F32)
    ab = jnp.dot(xb, wab_ref[...], preferred_element_type=F32)

    for j in range(W // LANES):
        cs = slice(C_RGX + j * LANES, C_RGX + (j + 1) * LANES)
        ws = slice(j * LANES, (j + 1) * LANES)
        xj = u_ref[:, cs]
        taps = [rcw_ref[k:k + 1, ws] for k in range(CONV_W)]
        xc_ref[:, ws] = _causal_conv_strip(xj, hist_ref[:, ws], taps) + rcb_ref[:, ws]
        hist_ref[:, ws] = xj[tb - SUBLANES:tb, :]
        rgb_ref[:, ws] = xj[tb - 3:tb, :]
    gates = jnp.dot(xc_ref[...].astype(BF16), wg_ref[...], preferred_element_type=F32) + bg_ref[...]
    for j in range(W // LANES):
        zs = slice(C_RGZ + j * LANES, C_RGZ + (j + 1) * LANES)
        ws = slice(j * LANES, (j + 1) * LANES)
        r = jax.nn.sigmoid(gates[:, j * LANES:(j + 1) * LANES])
        i = jax.nn.sigmoid(gates[:, W + j * LANES:W + (j + 1) * LANES])
        log_a = (-RG_C) * r * jax.nn.softplus(-lam_ref[:, ws])
        a = jnp.exp(log_a)
        b = jnp.sqrt(1.0 - a * a) * (i * xc_ref[:, ws])
        a_cum, hloc = _scan_rows(a, b)
        h = hloc + a_cum * h_ref[:, ws]
        h_ref[:, ws] = h[tb - 1:tb, :]
        mix_ref[:, ws] = h * _silu(u_ref[:, zs])

    cosf = cos_ref[...]
    sinf = sin_ref[...]
    for hd in range(HEADS):
        q = u_ref[:, C_RQ + hd * DK:C_RQ + (hd + 1) * DK]
        k = u_ref[:, C_RK + hd * DK:C_RK + (hd + 1) * DK]
        v = u_ref[:, C_RV + hd * DK:C_RV + (hd + 1) * DK]
        z = u_ref[:, C_RZ + hd * DK:C_RZ + (hd + 1) * DK]
        q = q * cosf + pltpu.roll(q, DK // 2, axis=1) * sinf
        k = (k * cosf + pltpu.roll(k, DK // 2, axis=1) * sinf) * (DK ** -0.5)
        s = sret_ref[hd]
        outs = []
        for c in range(tb // c_ret):
            rs = slice(c * c_ret, (c + 1) * c_ret)
            qc, kc, vc = q[rs], k[rs], v[rs]
            sc = _bdot_nt(qc, kc) * rdecay_ref[hd]
            outs.append(_bdot(sc, vc) + _bdot(qc * rqdec_ref[hd], s))
            s = s * rsdec_ref[hd] + _bdot_tn(kc * rkdec_ref[hd], vc)
        sret_ref[hd] = s
        o = outs[0] if len(outs) == 1 else jnp.concatenate(outs, axis=0)
        mu = jnp.mean(o, axis=-1, keepdims=True)
        d = o - mu
        on = d * lax.rsqrt(jnp.mean(d * d, axis=-1, keepdims=True) + EPS)
        gs = slice(hd * DK, (hd + 1) * DK)
        mix_ref[:, W + hd * DK:W + (hd + 1) * DK] = (on * gnw_ref[:, gs] + gnb_ref[:, gs]) * _silu(z)

    for j in range(3 * W // LANES):
        cs = slice(C_GQ + j * LANES, C_GQ + (j + 1) * LANES)
        ws = slice(j * LANES, (j + 1) * LANES)
        hs = slice(W + j * LANES, W + (j + 1) * LANES)
        xj = u_ref[:, cs]
        taps = [gcw_ref[k:k + 1, ws] for k in range(CONV_W)]
        u_ref[:, cs] = _silu(_causal_conv_strip(xj, hist_ref[:, hs], taps))
        hist_ref[:, hs] = xj[tb - SUBLANES:tb, :]
        gb_ref[:, ws] = xj[tb - 3:tb, :]

    nc = tb // c_gdn
    glog = -jnp.exp(alog_ref[...]) * jax.nn.softplus(ab + dtb_ref[...])
    beta_all = jax.nn.sigmoid(ab)
    in_chunk = lax.broadcasted_iota(jnp.int32, (tb, LANES), 0) % c_gdn
    gcs = glog
    sh = 1
    while sh < c_gdn:
        gcs = gcs + jnp.where(in_chunk >= sh, pltpu.roll(gcs, sh, axis=0), 0.0)
        sh *= 2

    ri = lax.broadcasted_iota(jnp.int32, (c_gdn, c_gdn), 0)
    ci = lax.broadcasted_iota(jnp.int32, (c_gdn, c_gdn), 1)
    tril = ri >= ci
    strict = ri > ci
    eye_f = jnp.where(ri == ci, 1.0, 0.0).astype(F32)
    n_sq = max(int(math.ceil(math.log2(c_gdn))) - 1, 0)
    to3 = lambda m: m.reshape(nc, c_gdn, m.shape[-1])

    for hd in range(HEADS):
        q = u_ref[:, C_GQ + hd * DK:C_GQ + (hd + 1) * DK]
        k = u_ref[:, C_GK + hd * DK:C_GK + (hd + 1) * DK]
        v = u_ref[:, C_GV + hd * DK:C_GV + (hd + 1) * DK]
        z = u_ref[:, C_GZ + hd * DK:C_GZ + (hd + 1) * DK]
        q = q * lax.rsqrt(jnp.sum(q * q, axis=-1, keepdims=True) + EPS) * (DK ** -0.5)
        k = k * lax.rsqrt(jnp.sum(k * k, axis=-1, keepdims=True) + EPS)
        gc = jnp.broadcast_to(gcs[:, hd:hd + 1], (tb, LANES))
        beta = jnp.broadcast_to(beta_all[:, HEADS + hd:HEADS + hd + 1], (tb, LANES))
        egc = jnp.exp(gc)
        kb = k * beta
        gc3 = to3(gc)
        g_last = gc3[:, c_gdn - 1:c_gdn, :]
        kd3 = to3(k) * jnp.exp(g_last - gc3)
        qe3 = to3(q * egc)
        decs = []
        for c in range(nc):
            gcc = gc3[c]
            diff = gcc[:, :c_gdn] - gcc.T[:c_gdn, :]
            decs.append(jnp.where(tril, jnp.exp(jnp.where(tril, diff, 0.0)), 0.0))
        dec3 = jnp.stack(decs)
        k3 = to3(k)
        a_low = jnp.where(strict, _bmm_nt(to3(kb), k3) * dec3, 0.0)
        att = _bmm_nt(to3(q), k3) * dec3
        p = a_low
        tinv = eye_f - a_low
        for _ in range(n_sq):
            p = _bmm(p, p)
            tinv = tinv + _bmm(tinv, p)
        sol = _bmm(tinv, jnp.concatenate([to3(v * beta), to3(kb * egc)], axis=-1))
        uu, ww = sol[..., :DK], sol[..., DK:]
        s = sgdn_ref[hd]
        outs = []
        for c in range(nc):
            ws_ = _bdot(jnp.concatenate([ww[c], qe3[c]], axis=0), s)
            v_new = uu[c] - ws_[:c_gdn]
            outs.append(ws_[c_gdn:] + _bdot(att[c], v_new))
            s = s * jnp.exp(g_last[c]) + _bdot_tn(kd3[c], v_new)
        sgdn_ref[hd] = s
        o = outs[0] if nc == 1 else jnp.concatenate(outs, axis=0)
        on = o * lax.rsqrt(jnp.mean(o * o, axis=-1, keepdims=True) + EPS) * gnorm_ref[...]
        mix_ref[:, 2 * W + hd * DK:2 * W + (hd + 1) * DK] = on * _silu(z)

    out = jnp.dot(mix_ref[...].astype(BF16), wout_ref[...], preferred_element_type=F32)
    r = ALPHA * x + out
    mu = jnp.mean(r, axis=-1, keepdims=True)
    d = r - mu
    var = jnp.mean(d * d, axis=-1, keepdims=True)
    y_ref[...] = d * lax.rsqrt(var + EPS) * lnw_ref[...] + lnb_ref[...]


def _ret_tables(c):
    lg = jnp.log1p(-jnp.exp2(-5.0 - jnp.arange(HEADS, dtype=F32)))[:, None, None]
    idx = jnp.arange(c, dtype=F32)
    diff = idx[:, None] - idx[None, :]
    decay = jnp.where(diff >= 0, jnp.exp(lg * jnp.maximum(diff, 0.0)), 0.0)
    q_dec = jnp.broadcast_to(jnp.exp(lg[:, 0] * (idx + 1.0))[:, :, None], (HEADS, c, DK))
    k_dec = jnp.broadcast_to(jnp.exp(lg[:, 0] * (c - 1.0 - idx))[:, :, None], (HEADS, c, DK))
    s_dec = jnp.broadcast_to(jnp.exp(lg * c), (HEADS, 1, DK))
    return decay, q_dec, k_dec, s_dec


def _rope_tables(pos):
    half = DK // 2
    inv = ROPE_BASE ** (-jnp.arange(half, dtype=F32) / half)
    ang = pos[:, None] * inv[None, :]
    cos, sin = jnp.cos(ang), jnp.sin(ang)
    return jnp.concatenate([cos, cos], -1), jnp.concatenate([-sin, sin], -1)


def _prompt_layer(layer, x, pos, init, wts, *, tb, c_ret, c_gdn):
    bsz, tlen, _ = x.shape
    assert tlen % tb == 0 and tb % c_ret == 0 and tb % c_gdn == 0 and tb % SUBLANES == 0
    nt = tlen // tb
    cos2, sin2 = _rope_tables(pos)
    rtabs = _ret_tables(c_ret)

    def wspec(a):
        nd = a.ndim - 1
        return pl.BlockSpec((None,) + a.shape[1:], lambda b, t, _n=nd: (layer,) + (0,) * _n,
                            pipeline_mode=pl.Buffered(1))

    def cspec(a):
        nd = a.ndim
        return pl.BlockSpec(a.shape, lambda b, t, _n=nd: (0,) * _n, pipeline_mode=pl.Buffered(1))

    def ispec(a):
        nd = a.ndim - 1
        return pl.BlockSpec((None,) + a.shape[1:], lambda b, t, _n=nd: (0,) * (_n + 1),
                            pipeline_mode=pl.Buffered(1))

    def ospec(shape):
        nd = len(shape)
        return pl.BlockSpec((None,) + shape, lambda b, t, _n=nd: (b,) + (0,) * _n)

    in_specs = ([pl.BlockSpec((None, tb, D_MODEL), lambda b, t: (b, t, 0)),
                 pl.BlockSpec((tb, DK), lambda b, t: (t, 0)),
                 pl.BlockSpec((tb, DK), lambda b, t: (t, 0))]
                + [ispec(a) for a in init] + [wspec(a) for a in wts] + [cspec(a) for a in rtabs])
    out_shape = (jax.ShapeDtypeStruct((bsz, tlen, D_MODEL), F32),
                 jax.ShapeDtypeStruct((bsz, 1, W), F32),
                 jax.ShapeDtypeStruct((bsz, CONV_W - 1, W), F32),
                 jax.ShapeDtypeStruct((bsz, HEADS, DK, DK), F32),
                 jax.ShapeDtypeStruct((bsz, CONV_W - 1, 3 * W), F32),
                 jax.ShapeDtypeStruct((bsz, HEADS, DK, DK), F32))
    out_specs = (pl.BlockSpec((None, tb, D_MODEL), lambda b, t: (b, t, 0)),
                 ospec((1, W)), ospec((CONV_W - 1, W)), ospec((HEADS, DK, DK)),
                 ospec((CONV_W - 1, 3 * W)), ospec((HEADS, DK, DK)))
    scratch = [pltpu.VMEM((tb, D_MAIN), F32),
               pltpu.VMEM((SUBLANES, N_CONV), F32),
               pltpu.VMEM((tb, W), F32),
               pltpu.VMEM((tb, 3 * W), F32)]
    kern = functools.partial(_prompt_layer_kernel, tb=tb, c_ret=c_ret, c_gdn=c_gdn)
    return pl.pallas_call(
        kern, grid=(bsz, nt), in_specs=in_specs, out_specs=out_specs, out_shape=out_shape,
        scratch_shapes=scratch,
        compiler_params=pltpu.CompilerParams(dimension_semantics=("arbitrary", "arbitrary"),
                                             vmem_limit_bytes=56 * 1024 * 1024),
        name=f"prompt_layer{layer}_t{tlen}",
    )(x, cos2, sin2, *init, *wts, *rtabs)


def _prep_weights(w_in, rg_conv_w, rg_conv_b, rg_w_a, rg_b_a, rg_w_x, rg_b_x, rg_lambda,
                  ret_gn_w, ret_gn_b, gdn_conv_w, gdn_a_log, gdn_dt_bias, gdn_norm_w, w_out, ln_w, ln_b):
    eye = jnp.eye(RG_BLOCKS, dtype=F32)

    def bdiag(w):
        l, n, c, d = w.shape
        return jnp.einsum('lncd,nm->lncmd', w.astype(F32), eye).reshape(l, n * c, n * d)

    pad = LANES - 2 * HEADS
    w_main = w_in[:, :, :D_MAIN].astype(BF16)
    w_ab = jnp.pad(w_in[:, :, D_MAIN:], ((0, 0), (0, 0), (0, pad))).astype(BF16)
    wg = jnp.concatenate([bdiag(rg_w_a), bdiag(rg_w_x)], axis=-1).astype(BF16)
    bg = jnp.concatenate([rg_b_a, rg_b_x], axis=-1)[:, None, :].astype(F32)
    row = lambda a: a[:, None, :].astype(F32)
    padh = lambda a: jnp.pad(a.astype(F32), ((0, 0), (0, LANES - HEADS)))[:, None, :]
    return (w_main, w_ab, wg, bg, row(rg_lambda), rg_conv_w.astype(F32), row(rg_conv_b),
            row(ret_gn_w), row(ret_gn_b), gdn_conv_w.astype(F32), padh(gdn_a_log), padh(gdn_dt_bias),
            row(gdn_norm_w), w_out.astype(BF16), row(ln_w), row(ln_b))


def _sample_kernel(
        x_ref, cos_ref, sin_ref, gam_ref,
        h0_ref, rgb0_ref, sret0_ref, gb0_ref, sgdn0_ref,
        win_ref, wab_ref, wg_ref, bg_ref, lam_ref, rcw_ref, rcb_ref, gnw_ref, gnb_ref,
        gcw_ref, alog_ref, dtb_ref, gnorm_ref, wout_ref, lnw_ref, lnb_ref,
        y_ref, h_ref, rgb_ref, sret_ref, gb_ref, sgdn_ref,
        xcur_ref, u_ref, eg_ref, beta_ref, mix_ref,
        *, bb_rows, n_bb):
    layer = pl.program_id(0)
    bb = pl.program_id(1)

    @pl.when(jnp.logical_and(layer == 0, bb == 0))
    def _load_x():
        xcur_ref[...] = x_ref[...]

    @pl.when(bb == 0)
    def _project():
        xb = xcur_ref[...].astype(BF16)
        u_ref[...] = jnp.dot(xb, win_ref[...], preferred_element_type=F32)
        ab = jnp.dot(xb, wab_ref[...], preferred_element_type=F32)
        eg_ref[...] = jnp.exp(-jnp.exp(alog_ref[...]) * jax.nn.softplus(ab + dtb_ref[...]))
        beta_ref[...] = jax.nn.sigmoid(ab)

        cur = u_ref[:, C_RGX:C_RGX + W]
        xc = rcb_ref[...] + rcw_ref[CONV_W - 1:CONV_W, :] * cur
        for k in range(CONV_W - 1):
            xc = xc + rcw_ref[k:k + 1, :] * rgb0_ref[:, k * W:(k + 1) * W]
        rgb_ref[:, 0:W] = rgb0_ref[:, W:2 * W]
        rgb_ref[:, W:2 * W] = rgb0_ref[:, 2 * W:3 * W]
        rgb_ref[:, 2 * W:3 * W] = cur
        gates = jnp.dot(xc.astype(BF16), wg_ref[...], preferred_element_type=F32) + bg_ref[...]
        r = jax.nn.sigmoid(gates[:, :W])
        i = jax.nn.sigmoid(gates[:, W:])
        log_a = (-RG_C) * r * jax.nn.softplus(-lam_ref[...])
        a = jnp.exp(log_a)
        h = a * h0_ref[...] + jnp.sqrt(1.0 - a * a) * (i * xc)
        h_ref[...] = h
        mix_ref[:, 0:W] = h * _silu(u_ref[:, C_RGZ:C_RGZ + W])

        cosf = cos_ref[...]
        sinf = sin_ref[...]
        for hd in range(HEADS):
            qs = slice(C_RQ + hd * DK, C_RQ + (hd + 1) * DK)
            ks = slice(C_RK + hd * DK, C_RK + (hd + 1) * DK)
            q = u_ref[:, qs]
            k = u_ref[:, ks]
            u_ref[:, qs] = q * cosf + pltpu.roll(q, DK // 2, axis=1) * sinf
            u_ref[:, ks] = (k * cosf + pltpu.roll(k, DK // 2, axis=1) * sinf) * (DK ** -0.5)

        n3 = 3 * W
        for j in range(n3 // LANES):
            cs = slice(C_GQ + j * LANES, C_GQ + (j + 1) * LANES)
            ws = slice(j * LANES, (j + 1) * LANES)
            cur = u_ref[:, cs]
            acc = gcw_ref[CONV_W - 1:CONV_W, ws] * cur
            for k in range(CONV_W - 1):
                acc = acc + gcw_ref[k:k + 1, ws] * gb0_ref[:, k * n3 + j * LANES:k * n3 + (j + 1) * LANES]
            gb_ref[:, j * LANES:(j + 1) * LANES] = gb0_ref[:, n3 + j * LANES:n3 + (j + 1) * LANES]
            gb_ref[:, n3 + j * LANES:n3 + (j + 1) * LANES] = gb0_ref[:, 2 * n3 + j * LANES:2 * n3 + (j + 1) * LANES]
            gb_ref[:, 2 * n3 + j * LANES:2 * n3 + (j + 1) * LANES] = cur
            y = _silu(acc)
            if j < 2 * HEADS:
                y = y * lax.rsqrt(jnp.sum(y * y, axis=-1, keepdims=True) + EPS)
                if j < HEADS:
                    y = y * (DK ** -0.5)
            u_ref[:, cs] = y

    r0 = pl.multiple_of(bb * bb_rows, SUBLANES)
    rows = pl.ds(r0, bb_rows)
    egb = eg_ref[rows, :]
    btb = beta_ref[rows, :]
    for hd in range(HEADS):
        q = u_ref[rows, C_RQ + hd * DK:C_RQ + (hd + 1) * DK]
        k = u_ref[rows, C_RK + hd * DK:C_RK + (hd + 1) * DK]
        v = u_ref[rows, C_RV + hd * DK:C_RV + (hd + 1) * DK]
        z = u_ref[rows, C_RZ + hd * DK:C_RZ + (hd + 1) * DK]
        qk = jnp.sum(q * k, axis=-1, keepdims=True)
        qt = q.T
        kt = k.T
        gam = gam_ref[hd]
        o_rows = []
        for i in range(bb_rows):
            s = sret0_ref[i, hd]
            qc = jnp.broadcast_to(qt[:, i:i + 1], (DK, DK))
            kc = jnp.broadcast_to(kt[:, i:i + 1], (DK, DK))
            qs_ = jnp.sum(qc * s, axis=0, keepdims=True)
            o_rows.append(qk[i:i + 1, :] * v[i:i + 1, :] + gam * qs_)
            sret_ref[i, hd] = gam * s + kc * v[i:i + 1, :]
        o = jnp.concatenate(o_rows, axis=0)
        mu = jnp.mean(o, axis=-1, keepdims=True)
        d = o - mu
        on = d * lax.rsqrt(jnp.mean(d * d, axis=-1, keepdims=True) + EPS)
        gs = slice(hd * DK, (hd + 1) * DK)
        mix_ref[rows, W + hd * DK:W + (hd + 1) * DK] = (on * gnw_ref[:, gs] + gnb_ref[:, gs]) * _silu(z)

        q = u_ref[rows, C_GQ + hd * DK:C_GQ + (hd + 1) * DK]
        k = u_ref[rows, C_GK + hd * DK:C_GK + (hd + 1) * DK]
        v = u_ref[rows, C_GV + hd * DK:C_GV + (hd + 1) * DK]
        z = u_ref[rows, C_GZ + hd * DK:C_GZ + (hd + 1) * DK]
        qk = jnp.sum(q * k, axis=-1, keepdims=True)
        qt = q.T
        kt = k.T
        eg = jnp.broadcast_to(egb[:, hd:hd + 1], (bb_rows, DK))
        beta = jnp.broadcast_to(btb[:, HEADS + hd:HEADS + hd + 1], (bb_rows, DK))
        o_rows = []
        for i in range(bb_rows):
            s = sgdn0_ref[i, hd]
            qc = jnp.broadcast_to(qt[:, i:i + 1], (DK, DK))
            kc = jnp.broadcast_to(kt[:, i:i + 1], (DK, DK))
            ks_ = jnp.sum(kc * s, axis=0, keepdims=True)
            qs_ = jnp.sum(qc * s, axis=0, keepdims=True)
            eg_i = eg[i:i + 1, :]
            v_new = beta[i:i + 1, :] * (v[i:i + 1, :] - eg_i * ks_)
            o_rows.append(eg_i * qs_ + qk[i:i + 1, :] * v_new)
            sgdn_ref[i, hd] = s * eg_i + kc * v_new
        o = jnp.concatenate(o_rows, axis=0)
        on = o * lax.rsqrt(jnp.mean(o * o, axis=-1, keepdims=True) + EPS) * gnorm_ref[...]
        mix_ref[rows, 2 * W + hd * DK:2 * W + (hd + 1) * DK] = on * _silu(z)

    @pl.when(bb == n_bb - 1)
    def _finish():
        out = jnp.dot(mix_ref[...].astype(BF16), wout_ref[...], preferred_element_type=F32)
        r = ALPHA * xcur_ref[...] + out
        mu = jnp.mean(r, axis=-1, keepdims=True)
        d = r - mu
        var = jnp.mean(d * d, axis=-1, keepdims=True)
        y = d * lax.rsqrt(var + EPS) * lnw_ref[...] + lnb_ref[...]
        xcur_ref[...] = y
        y_ref[...] = y


def _sample_path(x, h0, rgb0, sret0, gb0, sgdn0, wts, *, bb_rows):
    nb = x.shape[0]
    n_bb = nb // bb_rows
    pos = jnp.arange(1, dtype=F32) + float(PAST_LEN)
    cos2, sin2 = _rope_tables(pos)
    lg = jnp.log1p(-jnp.exp2(-5.0 - jnp.arange(HEADS, dtype=F32)))
    gam = jnp.broadcast_to(jnp.exp(lg)[:, None, None], (HEADS, 1, DK))

    def const(a):
        nd = a.ndim
        return pl.BlockSpec(a.shape, lambda l, b, _n=nd: (0,) * _n)

    def per_layer(a, prefetch=False):
        nd = a.ndim - 1
        return pl.BlockSpec((None,) + a.shape[1:], lambda l, b, _n=nd: (l,) + (0,) * _n,
                            pipeline_mode=pl.Buffered(2 if prefetch else 1))

    def per_block(a):
        return pl.BlockSpec((None, bb_rows) + a.shape[2:], lambda l, b: (l, b, 0, 0, 0))

    ins = (x, cos2, sin2, gam, h0, rgb0, sret0, gb0, sgdn0) + tuple(wts)
    in_specs = ([const(x), const(cos2), const(sin2), const(gam),
                 per_layer(h0), per_layer(rgb0), per_block(sret0), per_layer(gb0), per_block(sgdn0)]
                + [per_layer(a, prefetch=(i == 0)) for i, a in enumerate(wts)])
    out_shape = (jax.ShapeDtypeStruct(x.shape, F32),
                 jax.ShapeDtypeStruct(h0.shape, F32), jax.ShapeDtypeStruct(rgb0.shape, F32),
                 jax.ShapeDtypeStruct(sret0.shape, F32), jax.ShapeDtypeStruct(gb0.shape, F32),
                 jax.ShapeDtypeStruct(sgdn0.shape, F32))
    out_specs = (const(x), per_layer(h0), per_layer(rgb0), per_block(sret0), per_layer(gb0), per_block(sgdn0))
    scratch = [pltpu.VMEM((nb, D_MODEL), F32),
               pltpu.VMEM((nb, D_MAIN), F32),
               pltpu.VMEM((nb, LANES), F32),
               pltpu.VMEM((nb, LANES), F32),
               pltpu.VMEM((nb, 3 * W), F32)]
    kern = functools.partial(_sample_kernel, bb_rows=bb_rows, n_bb=n_bb)
    return pl.pallas_call(
        kern, grid=(DEPTH, n_bb), in_specs=in_specs, out_specs=out_specs, out_shape=out_shape,
        scratch_shapes=scratch,
        compiler_params=pltpu.CompilerParams(dimension_semantics=("arbitrary", "arbitrary"),
                                             vmem_limit_bytes=58 * 1024 * 1024),
        name="sample_path",
    )(*ins)


def kernel(x_prompt, x_sample, state_rglru_h, state_rglru_conv, state_ret, state_gdn_conv, state_gdn,
           meta_tokens, w_in, rg_conv_w, rg_conv_b, rg_w_a, rg_b_a, rg_w_x, rg_b_x, rg_lambda,
           ret_gn_w, ret_gn_b, gdn_conv_w, gdn_a_log, gdn_dt_bias, gdn_norm_w, w_out, ln_w, ln_b):
    bp, seq, _ = x_prompt.shape
    nb = x_sample.shape[0]
    wts = _prep_weights(w_in, rg_conv_w, rg_conv_b, rg_w_a, rg_b_a, rg_w_x, rg_b_x, rg_lambda,
                        ret_gn_w, ret_gn_b, gdn_conv_w, gdn_a_log, gdn_dt_bias, gdn_norm_w, w_out, ln_w, ln_b)

    pos = jnp.arange(N_META + seq, dtype=F32)
    zeros = lambda *s: jnp.zeros(s, F32)
    init0 = (zeros(1, 1, W), zeros(1, CONV_W - 1, W), zeros(1, HEADS, DK, DK),
             zeros(1, CONV_W - 1, 3 * W), zeros(1, HEADS, DK, DK))
    xm = meta_tokens.astype(x_prompt.dtype)[None]
    xp = x_prompt
    new_p = [[] for _ in range(5)]
    for l in range(DEPTH):
        xm, *st_m = _prompt_layer(l, xm, pos[:N_META], init0, wts, tb=N_META, c_ret=N_META, c_gdn=N_META)
        xp, *st_p = _prompt_layer(l, xp, pos[N_META:], tuple(st_m), wts, tb=256, c_ret=256, c_gdn=64)
        for j in range(5):
            new_p[j].append(st_p[j])
    sp = [jnp.stack(a) for a in new_p]
    sp[0] = sp[0].reshape(DEPTH, bp, W)

    ys, sh, srgb, sret, sgb, sgdn = _sample_path(
        x_sample.reshape(nb, D_MODEL), state_rglru_h,
        state_rglru_conv.reshape(DEPTH, nb, (CONV_W - 1) * W), state_ret,
        state_gdn_conv.reshape(DEPTH, nb, (CONV_W - 1) * 3 * W), state_gdn, wts, bb_rows=8)
    return (xp, ys.reshape(x_sample.shape), sp[0], sp[1], sp[2], sp[3], sp[4],
            sh, srgb.reshape(state_rglru_conv.shape), sret, sgb.reshape(state_gdn_conv.shape), sgdn)
```

```python
import functools
import math

import jax
import jax.numpy as jnp
from jax import lax
from jax.experimental import pallas as pl
from jax.experimental.pallas import tpu as pltpu

F32 = jnp.float32
BF16 = jnp.bfloat16

D_MODEL = 1024
DEPTH = 4
N_META = 16
PAST_LEN = 16384
W = 512
CONV_W = 4
RG_BLOCKS = 8
RG_C = 8.0
HEADS = 4
DK = W // HEADS
ROPE_BASE = 10000.0
EPS = 1e-6
ALPHA = (2.0 * DEPTH) ** 0.25
D_MAIN = 10 * W
LANES = 128
SUBLANES = 8
N_CONV = 4 * W

C_RGX, C_RGZ, C_RQ, C_RK, C_RV, C_RZ, C_GQ, C_GK, C_GV, C_GZ = (i * W for i in range(10))


def _bdot(a, b):
    return jnp.dot(a.astype(BF16), b.astype(BF16), preferred_element_type=F32)


def _bdot_nt(a, b):
    return lax.dot_general(a.astype(BF16), b.astype(BF16), (((1,), (1,)), ((), ())), preferred_element_type=F32)


def _bdot_tn(a, b):
    return lax.dot_general(a.astype(BF16), b.astype(BF16), (((0,), (0,)), ((), ())), preferred_element_type=F32)


def _bmm(a, b):
    return jnp.einsum('nij,njk->nik', a.astype(BF16), b.astype(BF16), preferred_element_type=F32)


def _bmm_nt(a, b):
    return jnp.einsum('nik,njk->nij', a.astype(BF16), b.astype(BF16), preferred_element_type=F32)


def _silu(x):
    return x * jax.nn.sigmoid(x)


def _scan_rows(a, b):
    n = a.shape[0]
    rows = lax.broadcasted_iota(jnp.int32, a.shape, 0)
    s = 1
    while s < n:
        a_s = pltpu.roll(a, s, axis=0)
        b_s = pltpu.roll(b, s, axis=0)
        m = rows >= s
        b = jnp.where(m, a * b_s + b, b)
        a = jnp.where(m, a * a_s, a)
        s *= 2
    return a, b


def _causal_conv_strip(x, hist, taps):
    r8 = lax.broadcasted_iota(jnp.int32, hist.shape, 0)
    acc = taps[CONV_W - 1] * x
    for d in range(1, CONV_W):
        xs = pltpu.roll(x, d, axis=0)
        top = jnp.where(r8 < d, pltpu.roll(hist, d, axis=0), xs[0:SUBLANES])
        acc = acc + taps[CONV_W - 1 - d] * jnp.concatenate([top, xs[SUBLANES:]], axis=0)
    return acc


def _prompt_layer_kernel(
        x_ref, cos_ref, sin_ref,
        h0_ref, rgb0_ref, sret0_ref, gb0_ref, sgdn0_ref,
        win_ref, wab_ref, wg_ref, bg_ref, lam_ref, rcw_ref, rcb_ref, gnw_ref, gnb_ref,
        gcw_ref, alog_ref, dtb_ref, gnorm_ref, wout_ref, lnw_ref, lnb_ref,
        rdecay_ref, rqdec_ref, rkdec_ref, rsdec_ref,
        y_ref, h_ref, rgb_ref, sret_ref, gb_ref, sgdn_ref,
        u_ref, hist_ref, xc_ref, mix_ref,
        *, tb, c_ret, c_gdn):
    t = pl.program_id(1)

    @pl.when(t == 0)
    def _init():
        h_ref[...] = h0_ref[...]
        sret_ref[...] = sret0_ref[...]
        sgdn_ref[...] = sgdn0_ref[...]
        hist_ref[SUBLANES - 3:SUBLANES, 0:W] = rgb0_ref[...]
        hist_ref[SUBLANES - 3:SUBLANES, W:N_CONV] = gb0_ref[...]

    x = x_ref[...]
    xb = x.astype(BF16)
    u_ref[...] = jnp.dot(xb, win_ref[...], preferred_element_type=F32)
    ab = jnp.dot(xb, wab_ref[...], preferred_element_type=F32)

    for j in range(W // LANES):
        cs = slice(C_RGX + j * LANES, C_RGX + (j + 1) * LANES)
        ws = slice(j * LANES, (j + 1) * LANES)
        xj = u_ref[:, cs]
        taps = [rcw_ref[k:k + 1, ws] for k in range(CONV_W)]
        xc_ref[:, ws] = _causal_conv_strip(xj, hist_ref[:, ws], taps) + rcb_ref[:, ws]
        hist_ref[:, ws] = xj[tb - SUBLANES:tb, :]
        rgb_ref[:, ws] = xj[tb - 3:tb, :]
    gates = jnp.dot(xc_ref[...].astype(BF16), wg_ref[...], preferred_element_type=F32) + bg_ref[...]
    for j in range(W // LANES):
        zs = slice(C_RGZ + j * LANES, C_RGZ + (j + 1) * LANES)
        ws = slice(j * LANES, (j + 1) * LANES)
        r = jax.nn.sigmoid(gates[:, j * LANES:(j + 1) * LANES])
        i = jax.nn.sigmoid(gates[:, W + j * LANES:W + (j + 1) * LANES])
        log_a = (-RG_C) * r * jax.nn.softplus(-lam_ref[:, ws])
        a = jnp.exp(log_a)
        b = jnp.sqrt(1.0 - a * a) * (i * xc_ref[:, ws])
        a_cum, hloc = _scan_rows(a, b)
        h = hloc + a_cum * h_ref[:, ws]
        h_ref[:, ws] = h[tb - 1:tb, :]
        mix_ref[:, ws] = h * _silu(u_ref[:, zs])

    cosf = cos_ref[...]
    sinf = sin_ref[...]
    for hd in range(HEADS):
        q = u_ref[:, C_RQ + hd * DK:C_RQ + (hd + 1) * DK]
        k = u_ref[:, C_RK + hd * DK:C_RK + (hd + 1) * DK]
        v = u_ref[:, C_RV + hd * DK:C_RV + (hd + 1) * DK]
        z = u_ref[:, C_RZ + hd * DK:C_RZ + (hd + 1) * DK]
        q = q * cosf + pltpu.roll(q, DK // 2, axis=1) * sinf
        k = (k * cosf + pltpu.roll(k, DK // 2, axis=1) * sinf) * (DK ** -0.5)
        s = sret_ref[hd]
        outs = []
        for c in range(tb // c_ret):
            rs = slice(c * c_ret, (c + 1) * c_ret)
            qc, kc, vc = q[rs], k[rs], v[rs]
            sc = _bdot_nt(qc, kc) * rdecay_ref[hd]
            outs.append(_bdot(sc, vc) + _bdot(qc * rqdec_ref[hd], s))
            s = s * rsdec_ref[hd] + _bdot_tn(kc * rkdec_ref[hd], vc)
        sret_ref[hd] = s
        o = outs[0] if len(outs) == 1 else jnp.concatenate(outs, axis=0)
        mu = jnp.mean(o, axis=-1, keepdims=True)
        d = o - mu
        on = d * lax.rsqrt(jnp.mean(d * d, axis=-1, keepdims=True) + EPS)
        gs = slice(hd * DK, (hd + 1) * DK)
        mix_ref[:, W + hd * DK:W + (hd + 1) * DK] = (on * gnw_ref[:, gs] + gnb_ref[:, gs]) * _silu(z)

    for j in range(3 * W // LANES):
        cs = slice(C_GQ + j * LANES, C_GQ + (j + 1) * LANES)
        ws = slice(j * LANES, (j + 1) * LANES)
        hs = slice(W + j * LANES, W + (j + 1) * LANES)
        xj = u_ref[:, cs]
        taps = [gcw_ref[k:k + 1, ws] for k in range(CONV_W)]
        u_ref[:, cs] = _silu(_causal_conv_strip(xj, hist_ref[:, hs], taps))
        hist_ref[:, hs] = xj[tb - SUBLANES:tb, :]
        gb_ref[:, ws] = xj[tb - 3:tb, :]

    nc = tb // c_gdn
    glog = -jnp.exp(alog_ref[...]) * jax.nn.softplus(ab + dtb_ref[...])
    beta_all = jax.nn.sigmoid(ab)
    in_chunk = lax.broadcasted_iota(jnp.int32, (tb, LANES), 0) % c_gdn
    gcs = glog
    sh = 1
    while sh < c_gdn:
        gcs = gcs + jnp.where(in_chunk >= sh, pltpu.roll(gcs, sh, axis=0), 0.0)
        sh *= 2

    ri = lax.broadcasted_iota(jnp.int32, (c_gdn, c_gdn), 0)
    ci = lax.broadcasted_iota(jnp.int32, (c_gdn, c_gdn), 1)
    tril = ri >= ci
    strict = ri > ci
    eye_f = jnp.where(ri == ci, 1.0, 0.0).astype(F32)
    n_sq = max(int(math.ceil(math.log2(c_gdn))) - 1, 0)

    names = ("q", "k", "kb", "vb", "kbe", "qe", "kd", "gl", "dec")
    parts = {n: [[None] * HEADS for _ in range(nc)] for n in names}
    for hd in range(HEADS):
        q = u_ref[:, C_GQ + hd * DK:C_GQ + (hd + 1) * DK]
        k = u_ref[:, C_GK + hd * DK:C_GK + (hd + 1) * DK]
        v = u_ref[:, C_GV + hd * DK:C_GV + (hd + 1) * DK]
        q = q * lax.rsqrt(jnp.sum(q * q, axis=-1, keepdims=True) + EPS) * (DK ** -0.5)
        k = k * lax.rsqrt(jnp.sum(k * k, axis=-1, keepdims=True) + EPS)
        gc = jnp.broadcast_to(gcs[:, hd:hd + 1], (tb, LANES))
        beta = jnp.broadcast_to(beta_all[:, HEADS + hd:HEADS + hd + 1], (tb, LANES))
        egc = jnp.exp(gc)
        kb = k * beta
        vb = v * beta
        kbe = kb * egc
        qe = q * egc
        for c in range(nc):
            rs = slice(c * c_gdn, (c + 1) * c_gdn)
            gcc = gc[rs]
            gl = gcc[c_gdn - 1:c_gdn, :]
            diff = gcc[:, :c_gdn] - gcc.T[:c_gdn, :]
            parts["dec"][c][hd] = jnp.where(tril, jnp.exp(jnp.where(tril, diff, 0.0)), 0.0)
            parts["gl"][c][hd] = gl
            parts["kd"][c][hd] = k[rs] * jnp.exp(gl - gcc)
            for n, val in (("q", q), ("k", k), ("kb", kb), ("vb", vb), ("kbe", kbe), ("qe", qe)):
                parts[n][c][hd] = val[rs]
    st = {n: jnp.stack([parts[n][c][hd] for c in range(nc) for hd in range(HEADS)]) for n in names}

    a_low = jnp.where(strict, _bmm_nt(st["kb"], st["k"]) * st["dec"], 0.0)
    att = _bmm_nt(st["q"], st["k"]) * st["dec"]
    p = a_low
    tinv = eye_f - a_low
    for _ in range(n_sq):
        p = _bmm(p, p)
        tinv = tinv + _bmm(tinv, p)
    sol = _bmm(tinv, jnp.concatenate([st["vb"], st["kbe"]], axis=-1))
    uu, ww = sol[..., :DK], sol[..., DK:]

    s = sgdn_ref[...]
    outs = []
    for c in range(nc):
        hs4 = slice(c * HEADS, (c + 1) * HEADS)
        ws_ = _bmm(jnp.concatenate([ww[hs4], st["qe"][hs4]], axis=1), s)
        v_new = uu[hs4] - ws_[:, :c_gdn]
        outs.append(ws_[:, c_gdn:] + _bmm(att[hs4], v_new))
        kv = jnp.einsum('hik,hiv->hkv', st["kd"][hs4].astype(BF16), v_new.astype(BF16),
                        preferred_element_type=F32)
        s = s * jnp.exp(st["gl"][hs4]) + kv
    sgdn_ref[...] = s
    for hd in range(HEADS):
        z = u_ref[:, C_GZ + hd * DK:C_GZ + (hd + 1) * DK]
        o = outs[0][hd] if nc == 1 else jnp.concatenate([outs[c][hd] for c in range(nc)], axis=0)
        on = o * lax.rsqrt(jnp.mean(o * o, axis=-1, keepdims=True) + EPS) * gnorm_ref[...]
        mix_ref[:, 2 * W + hd * DK:2 * W + (hd + 1) * DK] = on * _silu(z)

    out = jnp.dot(mix_ref[...].astype(BF16), wout_ref[...], preferred_element_type=F32)
    r = ALPHA * x + out
    mu = jnp.mean(r, axis=-1, keepdims=True)
    d = r - mu
    var = jnp.mean(d * d, axis=-1, keepdims=True)
    y_ref[...] = d * lax.rsqrt(var + EPS) * lnw_ref[...] + lnb_ref[...]


def _ret_tables(c):
    lg = jnp.log1p(-jnp.exp2(-5.0 - jnp.arange(HEADS, dtype=F32)))[:, None, None]
    idx = jnp.arange(c, dtype=F32)
    diff = idx[:, None] - idx[None, :]
    decay = jnp.where(diff >= 0, jnp.exp(lg * jnp.maximum(diff, 0.0)), 0.0)
    q_dec = jnp.broadcast_to(jnp.exp(lg[:, 0] * (idx + 1.0))[:, :, None], (HEADS, c, DK))
    k_dec = jnp.broadcast_to(jnp.exp(lg[:, 0] * (c - 1.0 - idx))[:, :, None], (HEADS, c, DK))
    s_dec = jnp.broadcast_to(jnp.exp(lg * c), (HEADS, 1, DK))
    return decay, q_dec, k_dec, s_dec


def _rope_tables(pos):
    half = DK // 2
    inv = ROPE_BASE ** (-jnp.arange(half, dtype=F32) / half)
    ang = pos[:, None] * inv[None, :]
    cos, sin = jnp.cos(ang), jnp.sin(ang)
    return jnp.concatenate([cos, cos], -1), jnp.concatenate([-sin, sin], -1)


def _prompt_layer(layer, x, pos, init, wts, *, tb, c_ret, c_gdn):
    bsz, tlen, _ = x.shape
    assert tlen % tb == 0 and tb % c_ret == 0 and tb % c_gdn == 0 and tb % SUBLANES == 0
    nt = tlen // tb
    cos2, sin2 = _rope_tables(pos)
    rtabs = _ret_tables(c_ret)

    def wspec(a):
        nd = a.ndim - 1
        return pl.BlockSpec((None,) + a.shape[1:], lambda b, t, _n=nd: (layer,) + (0,) * _n,
                            pipeline_mode=pl.Buffered(1))

    def cspec(a):
        nd = a.ndim
        return pl.BlockSpec(a.shape, lambda b, t, _n=nd: (0,) * _n, pipeline_mode=pl.Buffered(1))

    def ispec(a):
        nd = a.ndim - 1
        return pl.BlockSpec((None,) + a.shape[1:], lambda b, t, _n=nd: (0,) * (_n + 1),
                            pipeline_mode=pl.Buffered(1))

    def ospec(shape):
        nd = len(shape)
        return pl.BlockSpec((None,) + shape, lambda b, t, _n=nd: (b,) + (0,) * _n)

    in_specs = ([pl.BlockSpec((None, tb, D_MODEL), lambda b, t: (b, t, 0)),
                 pl.BlockSpec((tb, DK), lambda b, t: (t, 0)),
                 pl.BlockSpec((tb, DK), lambda b, t: (t, 0))]
                + [ispec(a) for a in init] + [wspec(a) for a in wts] + [cspec(a) for a in rtabs])
    out_shape = (jax.ShapeDtypeStruct((bsz, tlen, D_MODEL), F32),
                 jax.ShapeDtypeStruct((bsz, 1, W), F32),
                 jax.ShapeDtypeStruct((bsz, CONV_W - 1, W), F32),
                 jax.ShapeDtypeStruct((bsz, HEADS, DK, DK), F32),
                 jax.ShapeDtypeStruct((bsz, CONV_W - 1, 3 * W), F32),
                 jax.ShapeDtypeStruct((bsz, HEADS, DK, DK), F32))
    out_specs = (pl.BlockSpec((None, tb, D_MODEL), lambda b, t: (b, t, 0)),
                 ospec((1, W)), ospec((CONV_W - 1, W)), ospec((HEADS, DK, DK)),
                 ospec((CONV_W - 1, 3 * W)), ospec((HEADS, DK, DK)))
    scratch = [pltpu.VMEM((tb, D_MAIN), F32),
               pltpu.VMEM((SUBLANES, N_CONV), F32),
               pltpu.VMEM((tb, W), F32),
               pltpu.VMEM((tb, 3 * W), F32)]
    kern = functools.partial(_prompt_layer_kernel, tb=tb, c_ret=c_ret, c_gdn=c_gdn)
    return pl.pallas_call(
        kern, grid=(bsz, nt), in_specs=in_specs, out_specs=out_specs, out_shape=out_shape,
        scratch_shapes=scratch,
        compiler_params=pltpu.CompilerParams(dimension_semantics=("arbitrary", "arbitrary"),
                                             vmem_limit_bytes=56 * 1024 * 1024),
        name=f"prompt_layer{layer}_t{tlen}",
    )(x, cos2, sin2, *init, *wts, *rtabs)


def _prep_weights(w_in, rg_conv_w, rg_conv_b, rg_w_a, rg_b_a, rg_w_x, rg_b_x, rg_lambda,
                  ret_gn_w, ret_gn_b, gdn_conv_w, gdn_a_log, gdn_dt_bias, gdn_norm_w, w_out, ln_w, ln_b):
    eye = jnp.eye(RG_BLOCKS, dtype=F32)

    def bdiag(w):
        l, n, c, d = w.shape
        return jnp.einsum('lncd,nm->lncmd', w.astype(F32), eye).reshape(l, n * c, n * d)

    pad = LANES - 2 * HEADS
    w_main = w_in[:, :, :D_MAIN].astype(BF16)
    w_ab = jnp.pad(w_in[:, :, D_MAIN:], ((0, 0), (0, 0), (0, pad))).astype(BF16)
    wg = jnp.concatenate([bdiag(rg_w_a), bdiag(rg_w_x)], axis=-1).astype(BF16)
    bg = jnp.concatenate([rg_b_a, rg_b_x], axis=-1)[:, None, :].astype(F32)
    row = lambda a: a[:, None, :].astype(F32)
    padh = lambda a: jnp.pad(a.astype(F32), ((0, 0), (0, LANES - HEADS)))[:, None, :]
    return (w_main, w_ab, wg, bg, row(rg_lambda), rg_conv_w.astype(F32), row(rg_conv_b),
            row(ret_gn_w), row(ret_gn_b), gdn_conv_w.astype(F32), padh(gdn_a_log), padh(gdn_dt_bias),
            row(gdn_norm_w), w_out.astype(BF16), row(ln_w), row(ln_b))


def _sample_kernel(
        x_ref, cos_ref, sin_ref, gam_ref,
        h0_ref, rgb0_ref, sret0_ref, gb0_ref, sgdn0_ref,
        win_ref, wab_ref, wg_ref, bg_ref, lam_ref, rcw_ref, rcb_ref, gnw_ref, gnb_ref,
        gcw_ref, alog_ref, dtb_ref, gnorm_ref, wout_ref, lnw_ref, lnb_ref,
        y_ref, h_ref, rgb_ref, sret_ref, gb_ref, sgdn_ref,
        xcur_ref, u_ref, eg_ref, beta_ref, mix_ref,
        *, bb_rows, n_bb):
    layer = pl.program_id(0)
    bb = pl.program_id(1)

    @pl.when(jnp.logical_and(layer == 0, bb == 0))
    def _load_x():
        xcur_ref[...] = x_ref[...]

    @pl.when(bb == 0)
    def _project():
        xb = xcur_ref[...].astype(BF16)
        u_ref[...] = jnp.dot(xb, win_ref[...], preferred_element_type=F32)
        ab = jnp.dot(xb, wab_ref[...], preferred_element_type=F32)
        eg_ref[...] = jnp.exp(-jnp.exp(alog_ref[...]) * jax.nn.softplus(ab + dtb_ref[...]))
        beta_ref[...] = jax.nn.sigmoid(ab)

        cur = u_ref[:, C_RGX:C_RGX + W]
        xc = rcb_ref[...] + rcw_ref[CONV_W - 1:CONV_W, :] * cur
        for k in range(CONV_W - 1):
            xc = xc + rcw_ref[k:k + 1, :] * rgb0_ref[:, k * W:(k + 1) * W]
        rgb_ref[:, 0:W] = rgb0_ref[:, W:2 * W]
        rgb_ref[:, W:2 * W] = rgb0_ref[:, 2 * W:3 * W]
        rgb_ref[:, 2 * W:3 * W] = cur
        gates = jnp.dot(xc.astype(BF16), wg_ref[...], preferred_element_type=F32) + bg_ref[...]
        r = jax.nn.sigmoid(gates[:, :W])
        i = jax.nn.sigmoid(gates[:, W:])
        log_a = (-RG_C) * r * jax.nn.softplus(-lam_ref[...])
        a = jnp.exp(log_a)
        h = a * h0_ref[...] + jnp.sqrt(1.0 - a * a) * (i * xc)
        h_ref[...] = h
        mix_ref[:, 0:W] = h * _silu(u_ref[:, C_RGZ:C_RGZ + W])

        cosf = cos_ref[...]
        sinf = sin_ref[...]
        for hd in range(HEADS):
            qs = slice(C_RQ + hd * DK, C_RQ + (hd + 1) * DK)
            ks = slice(C_RK + hd * DK, C_RK + (hd + 1) * DK)
            q = u_ref[:, qs]
            k = u_ref[:, ks]
            u_ref[:, qs] = q * cosf + pltpu.roll(q, DK // 2, axis=1) * sinf
            u_ref[:, ks] = (k * cosf + pltpu.roll(k, DK // 2, axis=1) * sinf) * (DK ** -0.5)

        n3 = 3 * W
        for j in range(n3 // LANES):
            cs = slice(C_GQ + j * LANES, C_GQ + (j + 1) * LANES)
            ws = slice(j * LANES, (j + 1) * LANES)
            cur = u_ref[:, cs]
            acc = gcw_ref[CONV_W - 1:CONV_W, ws] * cur
            for k in range(CONV_W - 1):
                acc = acc + gcw_ref[k:k + 1, ws] * gb0_ref[:, k * n3 + j * LANES:k * n3 + (j + 1) * LANES]
            gb_ref[:, j * LANES:(j + 1) * LANES] = gb0_ref[:, n3 + j * LANES:n3 + (j + 1) * LANES]
            gb_ref[:, n3 + j * LANES:n3 + (j + 1) * LANES] = gb0_ref[:, 2 * n3 + j * LANES:2 * n3 + (j + 1) * LANES]
            gb_ref[:, 2 * n3 + j * LANES:2 * n3 + (j + 1) * LANES] = cur
            y = _silu(acc)
            if j < 2 * HEADS:
                y = y * lax.rsqrt(jnp.sum(y * y, axis=-1, keepdims=True) + EPS)
                if j < HEADS:
                    y = y * (DK ** -0.5)
            u_ref[:, cs] = y

    r0 = pl.multiple_of(bb * bb_rows, SUBLANES)
    rows = pl.ds(r0, bb_rows)
    egb = eg_ref[rows, :]
    btb = beta_ref[rows, :]
    for hd in range(HEADS):
        q = u_ref[rows, C_RQ + hd * DK:C_RQ + (hd + 1) * DK]
        k = u_ref[rows, C_RK + hd * DK:C_RK + (hd + 1) * DK]
        v = u_ref[rows, C_RV + hd * DK:C_RV + (hd + 1) * DK]
        z = u_ref[rows, C_RZ + hd * DK:C_RZ + (hd + 1) * DK]
        qk = jnp.sum(q * k, axis=-1, keepdims=True)
        qt = q.T
        kt = k.T
        gam = gam_ref[hd]
        o_rows = []
        for i in range(bb_rows):
            s = sret0_ref[i, hd]
            qc = jnp.broadcast_to(qt[:, i:i + 1], (DK, DK))
            kc = jnp.broadcast_to(kt[:, i:i + 1], (DK, DK))
            qs_ = jnp.sum(qc * s, axis=0, keepdims=True)
            o_rows.append(qk[i:i + 1, :] * v[i:i + 1, :] + gam * qs_)
            sret_ref[i, hd] = gam * s + kc * v[i:i + 1, :]
        o = jnp.concatenate(o_rows, axis=0)
        mu = jnp.mean(o, axis=-1, keepdims=True)
        d = o - mu
        on = d * lax.rsqrt(jnp.mean(d * d, axis=-1, keepdims=True) + EPS)
        gs = slice(hd * DK, (hd + 1) * DK)
        mix_ref[rows, W + hd * DK:W + (hd + 1) * DK] = (on * gnw_ref[:, gs] + gnb_ref[:, gs]) * _silu(z)

        q = u_ref[rows, C_GQ + hd * DK:C_GQ + (hd + 1) * DK]
        k = u_ref[rows, C_GK + hd * DK:C_GK + (hd + 1) * DK]
        v = u_ref[rows, C_GV + hd * DK:C_GV + (hd + 1) * DK]
        z = u_ref[rows, C_GZ + hd * DK:C_GZ + (hd + 1) * DK]
        qk = jnp.sum(q * k, axis=-1, keepdims=True)
        qt = q.T
        kt = k.T
        eg = jnp.broadcast_to(egb[:, hd:hd + 1], (bb_rows, DK))
        beta = jnp.broadcast_to(btb[:, HEADS + hd:HEADS + hd + 1], (bb_rows, DK))
        o_rows = []
        for i in range(bb_rows):
            s = sgdn0_ref[i, hd]
            qc = jnp.broadcast_to(qt[:, i:i + 1], (DK, DK))
            kc = jnp.broadcast_to(kt[:, i:i + 1], (DK, DK))
            ks_ = jnp.sum(kc * s, axis=0, keepdims=True)
            qs_ = jnp.sum(qc * s, axis=0, keepdims=True)
            eg_i = eg[i:i + 1, :]
            v_new = beta[i:i + 1, :] * (v[i:i + 1, :] - eg_i * ks_)
            o_rows.append(eg_i * qs_ + qk[i:i + 1, :] * v_new)
            sgdn_ref[i, hd] = s * eg_i + kc * v_new
        o = jnp.concatenate(o_rows, axis=0)
        on = o * lax.rsqrt(jnp.mean(o * o, axis=-1, keepdims=True) + EPS) * gnorm_ref[...]
        mix_ref[rows, 2 * W + hd * DK:2 * W + (hd + 1) * DK] = on * _silu(z)

    @pl.when(bb == n_bb - 1)
    def _finish():
        out = jnp.dot(mix_ref[...].astype(BF16), wout_ref[...], preferred_element_type=F32)
        r = ALPHA * xcur_ref[...] + out
        mu = jnp.mean(r, axis=-1, keepdims=True)
        d = r - mu
        var = jnp.mean(d * d, axis=-1, keepdims=True)
        y = d * lax.rsqrt(var + EPS) * lnw_ref[...] + lnb_ref[...]
        xcur_ref[...] = y
        y_ref[...] = y


def _sample_path(x, h0, rgb0, sret0, gb0, sgdn0, wts, *, bb_rows):
    nb = x.shape[0]
    n_bb = nb // bb_rows
    pos = jnp.arange(1, dtype=F32) + float(PAST_LEN)
    cos2, sin2 = _rope_tables(pos)
    lg = jnp.log1p(-jnp.exp2(-5.0 - jnp.arange(HEADS, dtype=F32)))
    gam = jnp.broadcast_to(jnp.exp(lg)[:, None, None], (HEADS, 1, DK))

    def const(a):
        nd = a.ndim
        return pl.BlockSpec(a.shape, lambda l, b, _n=nd: (0,) * _n)

    def per_layer(a, prefetch=False):
        nd = a.ndim - 1
        return pl.BlockSpec((None,) + a.shape[1:], lambda l, b, _n=nd: (l,) + (0,) * _n,
                            pipeline_mode=pl.Buffered(2 if prefetch else 1))

    def per_block(a):
        return pl.BlockSpec((None, bb_rows) + a.shape[2:], lambda l, b: (l, b, 0, 0, 0))

    ins = (x, cos2, sin2, gam, h0, rgb0, sret0, gb0, sgdn0) + tuple(wts)
    in_specs = ([const(x), const(cos2), const(sin2), const(gam),
                 per_layer(h0), per_layer(rgb0), per_block(sret0), per_layer(gb0), per_block(sgdn0)]
                + [per_layer(a, prefetch=(i == 0)) for i, a in enumerate(wts)])
    out_shape = (jax.ShapeDtypeStruct(x.shape, F32),
                 jax.ShapeDtypeStruct(h0.shape, F32), jax.ShapeDtypeStruct(rgb0.shape, F32),
                 jax.ShapeDtypeStruct(sret0.shape, F32), jax.ShapeDtypeStruct(gb0.shape, F32),
                 jax.ShapeDtypeStruct(sgdn0.shape, F32))
    out_specs = (const(x), per_layer(h0), per_layer(rgb0), per_block(sret0), per_layer(gb0), per_block(sgdn0))
    scratch = [pltpu.VMEM((nb, D_MODEL), F32),
               pltpu.VMEM((nb, D_MAIN), F32),
               pltpu.VMEM((nb, LANES), F32),
               pltpu.VMEM((nb, LANES), F32),
               pltpu.VMEM((nb, 3 * W), F32)]
    kern = functools.partial(_sample_kernel, bb_rows=bb_rows, n_bb=n_bb)
    return pl.pallas_call(
        kern, grid=(DEPTH, n_bb), in_specs=in_specs, out_specs=out_specs, out_shape=out_shape,
        scratch_shapes=scratch,
        compiler_params=pltpu.CompilerParams(dimension_semantics=("arbitrary", "arbitrary"),
                                             vmem_limit_bytes=58 * 1024 * 1024),
        name="sample_path",
    )(*ins)


def kernel(x_prompt, x_sample, state_rglru_h, state_rglru_conv, state_ret, state_gdn_conv, state_gdn,
           meta_tokens, w_in, rg_conv_w, rg_conv_b, rg_w_a, rg_b_a, rg_w_x, rg_b_x, rg_lambda,
           ret_gn_w, ret_gn_b, gdn_conv_w, gdn_a_log, gdn_dt_bias, gdn_norm_w, w_out, ln_w, ln_b):
    bp, seq, _ = x_prompt.shape
    nb = x_sample.shape[0]
    wts = _prep_weights(w_in, rg_conv_w, rg_conv_b, rg_w_a, rg_b_a, rg_w_x, rg_b_x, rg_lambda,
                        ret_gn_w, ret_gn_b, gdn_conv_w, gdn_a_log, gdn_dt_bias, gdn_norm_w, w_out, ln_w, ln_b)

    pos = jnp.arange(N_META + seq, dtype=F32)
    zeros = lambda *s: jnp.zeros(s, F32)
    init0 = (zeros(1, 1, W), zeros(1, CONV_W - 1, W), zeros(1, HEADS, DK, DK),
             zeros(1, CONV_W - 1, 3 * W), zeros(1, HEADS, DK, DK))
    xm = meta_tokens.astype(x_prompt.dtype)[None]
    xp = x_prompt
    new_p = [[] for _ in range(5)]
    for l in range(DEPTH):
        xm, *st_m = _prompt_layer(l, xm, pos[:N_META], init0, wts, tb=N_META, c_ret=N_META, c_gdn=N_META)
        xp, *st_p = _prompt_layer(l, xp, pos[N_META:], tuple(st_m), wts, tb=256, c_ret=256, c_gdn=64)
        for j in range(5):
            new_p[j].append(st_p[j])
    sp = [jnp.stack(a) for a in new_p]
    sp[0] = sp[0].reshape(DEPTH, bp, W)

    ys, sh, srgb, sret, sgb, sgdn = _sample_path(
        x_sample.reshape(nb, D_MODEL), state_rglru_h,
        state_rglru_conv.reshape(DEPTH, nb, (CONV_W - 1) * W), state_ret,
        state_gdn_conv.reshape(DEPTH, nb, (CONV_W - 1) * 3 * W), state_gdn, wts, bb_rows=8)
    return (xp, ys.reshape(x_sample.shape), sp[0], sp[1], sp[2], sp[3], sp[4],
            sh, srgb.reshape(state_rglru_conv.shape), sret, sgb.reshape(state_gdn_conv.shape), sgdn)
```

```python
import functools
import math

import jax
import jax.numpy as jnp
from jax import lax
from jax.experimental import pallas as pl
from jax.experimental.pallas import tpu as pltpu

F32 = jnp.float32
BF16 = jnp.bfloat16

D_MODEL = 1024
DEPTH = 4
N_META = 16
PAST_LEN = 16384
W = 512
CONV_W = 4
RG_BLOCKS = 8
RG_C = 8.0
HEADS = 4
DK = W // HEADS
ROPE_BASE = 10000.0
EPS = 1e-6
ALPHA = (2.0 * DEPTH) ** 0.25
D_MAIN = 10 * W
LANES = 128
SUBLANES = 8
N_CONV = 4 * W

C_RGX, C_RGZ, C_RQ, C_RK, C_RV, C_RZ, C_GQ, C_GK, C_GV, C_GZ = (i * W for i in range(10))


def _bdot(a, b):
    return jnp.dot(a.astype(BF16), b.astype(BF16), preferred_element_type=F32)


def _bdot_nt(a, b):
    return lax.dot_general(a.astype(BF16), b.astype(BF16), (((1,), (1,)), ((), ())), preferred_element_type=F32)


def _bdot_tn(a, b):
    return lax.dot_general(a.astype(BF16), b.astype(BF16), (((0,), (0,)), ((), ())), preferred_element_type=F32)


def _bmm(a, b):
    return jnp.einsum('nij,njk->nik', a.astype(BF16), b.astype(BF16), preferred_element_type=F32)


def _bmm_nt(a, b):
    return jnp.einsum('nik,njk->nij', a.astype(BF16), b.astype(BF16), preferred_element_type=F32)


def _silu(x):
    return x * jax.nn.sigmoid(x)


def _scan_rows(a, b, h0):
    n, lanes = a.shape
    g = n // SUBLANES
    a3 = a.reshape(g, SUBLANES, lanes)
    b3 = b.reshape(g, SUBLANES, lanes)
    sub = lax.broadcasted_iota(jnp.int32, a3.shape, 1)
    s = 1
    while s < SUBLANES:
        m = sub >= s
        a_s = pltpu.roll(a3, s, axis=1)
        b_s = pltpu.roll(b3, s, axis=1)
        b3 = jnp.where(m, a3 * b_s + b3, b3)
        a3 = jnp.where(m, a3 * a_s, a3)
        s *= 2
    carry = h0
    hs = []
    for i in range(g):
        hi = a3[i] * carry + b3[i]
        carry = hi[SUBLANES - 1:SUBLANES, :]
        hs.append(hi)
    return jnp.concatenate(hs, axis=0), carry


def _causal_conv_strip(x, hist, taps):
    r8 = lax.broadcasted_iota(jnp.int32, hist.shape, 0)
    acc = taps[CONV_W - 1] * x
    for d in range(1, CONV_W):
        xs = pltpu.roll(x, d, axis=0)
        top = jnp.where(r8 < d, pltpu.roll(hist, d, axis=0), xs[0:SUBLANES])
        acc = acc + taps[CONV_W - 1 - d] * jnp.concatenate([top, xs[SUBLANES:]], axis=0)
    return acc


def _prompt_layer_kernel(
        x_ref, cos_ref, sin_ref,
        h0_ref, rgb0_ref, sret0_ref, gb0_ref, sgdn0_ref,
        win_ref, wab_ref, wg_ref, bg_ref, lam_ref, rcw_ref, rcb_ref, gnw_ref, gnb_ref,
        gcw_ref, alog_ref, dtb_ref, gnorm_ref, wout_ref, lnw_ref, lnb_ref,
        rdecay_ref, rqdec_ref, rkdec_ref, rsdec_ref,
        y_ref, h_ref, rgb_ref, sret_ref, gb_ref, sgdn_ref,
        u_ref, hist_ref, xc_ref, mix_ref,
        *, tb, c_ret, c_gdn):
    t = pl.program_id(1)

    @pl.when(t == 0)
    def _init():
        h_ref[...] = h0_ref[...]
        sret_ref[...] = sret0_ref[...]
        sgdn_ref[...] = sgdn0_ref[...]
        hist_ref[SUBLANES - 3:SUBLANES, 0:W] = rgb0_ref[...]
        hist_ref[SUBLANES - 3:SUBLANES, W:N_CONV] = gb0_ref[...]

    x = x_ref[...]
    xb = x.astype(BF16)

    def project(c0, c1):
        u_ref[:, c0:c1] = jnp.dot(xb, win_ref[:, c0:c1], preferred_element_type=F32)

    project(C_RGX, C_RQ)
    ab = jnp.dot(xb, wab_ref[...], preferred_element_type=F32)
    project(C_GQ, D_MAIN)

    for j in range(W // LANES):
        cs = slice(C_RGX + j * LANES, C_RGX + (j + 1) * LANES)
        ws = slice(j * LANES, (j + 1) * LANES)
        xj = u_ref[:, cs]
        taps = [rcw_ref[k:k + 1, ws] for k in range(CONV_W)]
        xc_ref[:, ws] = _causal_conv_strip(xj, hist_ref[:, ws], taps) + rcb_ref[:, ws]
        hist_ref[:, ws] = xj[tb - SUBLANES:tb, :]
        rgb_ref[:, ws] = xj[tb - 3:tb, :]
    gates = jnp.dot(xc_ref[...].astype(BF16), wg_ref[...], preferred_element_type=F32) + bg_ref[...]
    for j in range(W // LANES):
        zs = slice(C_RGZ + j * LANES, C_RGZ + (j + 1) * LANES)
        ws = slice(j * LANES, (j + 1) * LANES)
        r = jax.nn.sigmoid(gates[:, j * LANES:(j + 1) * LANES])
        i = jax.nn.sigmoid(gates[:, W + j * LANES:W + (j + 1) * LANES])
        log_a = (-RG_C) * r * jax.nn.softplus(-lam_ref[:, ws])
        a = jnp.exp(log_a)
        b = jnp.sqrt(1.0 - a * a) * (i * xc_ref[:, ws])
        h, h_last = _scan_rows(a, b, h_ref[:, ws])
        h_ref[:, ws] = h_last
        mix_ref[:, ws] = h * _silu(u_ref[:, zs])

    project(C_RQ, C_GQ)
    cosf = cos_ref[...]
    sinf = sin_ref[...]
    for hd in range(HEADS):
        q = u_ref[:, C_RQ + hd * DK:C_RQ + (hd + 1) * DK]
        k = u_ref[:, C_RK + hd * DK:C_RK + (hd + 1) * DK]
        v = u_ref[:, C_RV + hd * DK:C_RV + (hd + 1) * DK]
        z = u_ref[:, C_RZ + hd * DK:C_RZ + (hd + 1) * DK]
        q = q * cosf + pltpu.roll(q, DK // 2, axis=1) * sinf
        k = (k * cosf + pltpu.roll(k, DK // 2, axis=1) * sinf) * (DK ** -0.5)
        s = sret_ref[hd]
        outs = []
        for c in range(tb // c_ret):
            rs = slice(c * c_ret, (c + 1) * c_ret)
            qc, kc, vc = q[rs], k[rs], v[rs]
            sc = _bdot_nt(qc, kc) * rdecay_ref[hd]
            outs.append(_bdot(sc, vc) + _bdot(qc * rqdec_ref[hd], s))
            s = s * rsdec_ref[hd] + _bdot_tn(kc * rkdec_ref[hd], vc)
        sret_ref[hd] = s
        o = outs[0] if len(outs) == 1 else jnp.concatenate(outs, axis=0)
        mu = jnp.mean(o, axis=-1, keepdims=True)
        d = o - mu
        on = d * lax.rsqrt(jnp.mean(d * d, axis=-1, keepdims=True) + EPS)
        gs = slice(hd * DK, (hd + 1) * DK)
        mix_ref[:, W + hd * DK:W + (hd + 1) * DK] = (on * gnw_ref[:, gs] + gnb_ref[:, gs]) * _silu(z)

    for j in range(3 * W // LANES):
        cs = slice(C_GQ + j * LANES, C_GQ + (j + 1) * LANES)
        ws = slice(j * LANES, (j + 1) * LANES)
        hs = slice(W + j * LANES, W + (j + 1) * LANES)
        xj = u_ref[:, cs]
        taps = [gcw_ref[k:k + 1, ws] for k in range(CONV_W)]
        u_ref[:, cs] = _silu(_causal_conv_strip(xj, hist_ref[:, hs], taps))
        hist_ref[:, hs] = xj[tb - SUBLANES:tb, :]
        gb_ref[:, ws] = xj[tb - 3:tb, :]

    nc = tb // c_gdn
    glog = -jnp.exp(alog_ref[...]) * jax.nn.softplus(ab + dtb_ref[...])
    beta_all = jax.nn.sigmoid(ab)
    in_chunk = lax.broadcasted_iota(jnp.int32, (tb, LANES), 0) % c_gdn
    gcs = glog
    sh = 1
    while sh < c_gdn:
        gcs = gcs + jnp.where(in_chunk >= sh, pltpu.roll(gcs, sh, axis=0), 0.0)
        sh *= 2

    ri = lax.broadcasted_iota(jnp.int32, (c_gdn, c_gdn), 0)
    ci = lax.broadcasted_iota(jnp.int32, (c_gdn, c_gdn), 1)
    tril = ri >= ci
    strict = ri > ci
    eye_f = jnp.where(ri == ci, 1.0, 0.0).astype(F32)
    n_lv = int(math.ceil(math.log2(c_gdn)))

    names = ("q", "k", "kb", "vb", "kbe", "qe", "kd", "gl", "dec")
    parts = {n: [[None] * HEADS for _ in range(nc)] for n in names}
    for hd in range(HEADS):
        q = u_ref[:, C_GQ + hd * DK:C_GQ + (hd + 1) * DK]
        k = u_ref[:, C_GK + hd * DK:C_GK + (hd + 1) * DK]
        v = u_ref[:, C_GV + hd * DK:C_GV + (hd + 1) * DK]
        q = q * lax.rsqrt(jnp.sum(q * q, axis=-1, keepdims=True) + EPS) * (DK ** -0.5)
        k = k * lax.rsqrt(jnp.sum(k * k, axis=-1, keepdims=True) + EPS)
        gc = jnp.broadcast_to(gcs[:, hd:hd + 1], (tb, LANES))
        beta = jnp.broadcast_to(beta_all[:, HEADS + hd:HEADS + hd + 1], (tb, LANES))
        egc = jnp.exp(gc)
        kb = k * beta
        vb = v * beta
        kbe = kb * egc
        qe = q * egc
        for c in range(nc):
            rs = slice(c * c_gdn, (c + 1) * c_gdn)
            gcc = gc[rs]
            gl = gcc[c_gdn - 1:c_gdn, :]
            diff = gcc[:, :c_gdn] - gcc.T[:c_gdn, :]
            parts["dec"][c][hd] = jnp.where(tril, jnp.exp(jnp.where(tril, diff, 0.0)), 0.0)
            parts["gl"][c][hd] = gl
            parts["kd"][c][hd] = k[rs] * jnp.exp(gl - gcc)
            for n, val in (("q", q), ("k", k), ("kb", kb), ("vb", vb), ("kbe", kbe), ("qe", qe)):
                parts[n][c][hd] = val[rs]
    st = {n: jnp.stack([parts[n][c][hd] for c in range(nc) for hd in range(HEADS)]) for n in names}

    a_low = jnp.where(strict, _bmm_nt(st["kb"], st["k"]) * st["dec"], 0.0)
    att = _bmm_nt(st["q"], st["k"]) * st["dec"]
    pt = jnp.concatenate([-a_low, jnp.broadcast_to(eye_f, a_low.shape)], axis=-1)
    right = lax.broadcasted_iota(jnp.int32, (c_gdn, 2 * c_gdn), 1) >= c_gdn
    for _ in range(n_lv):
        ptb = pt.astype(BF16)
        r = jnp.einsum('nij,njk->nik', ptb[..., :c_gdn], ptb, preferred_element_type=F32)
        pt = r + jnp.where(right, pt, 0.0)
    tinv = pt[..., c_gdn:]
    sol = _bmm(tinv, jnp.concatenate([st["vb"], st["kbe"]], axis=-1))
    uu, ww = sol[..., :DK], sol[..., DK:]

    s = sgdn_ref[...]
    outs = []
    for c in range(nc):
        hs4 = slice(c * HEADS, (c + 1) * HEADS)
        ws_ = _bmm(jnp.concatenate([ww[hs4], st["qe"][hs4]], axis=1), s)
        v_new = uu[hs4] - ws_[:, :c_gdn]
        outs.append(ws_[:, c_gdn:] + _bmm(att[hs4], v_new))
        kv = jnp.einsum('hik,hiv->hkv', st["kd"][hs4].astype(BF16), v_new.astype(BF16),
                        preferred_element_type=F32)
        s = s * jnp.exp(st["gl"][hs4]) + kv
    sgdn_ref[...] = s
    for hd in range(HEADS):
        z = u_ref[:, C_GZ + hd * DK:C_GZ + (hd + 1) * DK]
        o = outs[0][hd] if nc == 1 else jnp.concatenate([outs[c][hd] for c in range(nc)], axis=0)
        on = o * lax.rsqrt(jnp.mean(o * o, axis=-1, keepdims=True) + EPS) * gnorm_ref[...]
        mix_ref[:, 2 * W + hd * DK:2 * W + (hd + 1) * DK] = on * _silu(z)

    out = jnp.dot(mix_ref[...].astype(BF16), wout_ref[...], preferred_element_type=F32)
    r = ALPHA * x + out
    mu = jnp.mean(r, axis=-1, keepdims=True)
    d = r - mu
    var = jnp.mean(d * d, axis=-1, keepdims=True)
    y_ref[...] = d * lax.rsqrt(var + EPS) * lnw_ref[...] + lnb_ref[...]


def _ret_tables(c):
    lg = jnp.log1p(-jnp.exp2(-5.0 - jnp.arange(HEADS, dtype=F32)))[:, None, None]
    idx = jnp.arange(c, dtype=F32)
    diff = idx[:, None] - idx[None, :]
    decay = jnp.where(diff >= 0, jnp.exp(lg * jnp.maximum(diff, 0.0)), 0.0)
    q_dec = jnp.broadcast_to(jnp.exp(lg[:, 0] * (idx + 1.0))[:, :, None], (HEADS, c, DK))
    k_dec = jnp.broadcast_to(jnp.exp(lg[:, 0] * (c - 1.0 - idx))[:, :, None], (HEADS, c, DK))
    s_dec = jnp.broadcast_to(jnp.exp(lg * c), (HEADS, 1, DK))
    return decay, q_dec, k_dec, s_dec


def _rope_tables(pos):
    half = DK // 2
    inv = ROPE_BASE ** (-jnp.arange(half, dtype=F32) / half)
    ang = pos[:, None] * inv[None, :]
    cos, sin = jnp.cos(ang), jnp.sin(ang)
    return jnp.concatenate([cos, cos], -1), jnp.concatenate([-sin, sin], -1)


def _prompt_layer(layer, x, pos, init, wts, *, tb, c_ret, c_gdn):
    bsz, tlen, _ = x.shape
    assert tlen % tb == 0 and tb % c_ret == 0 and tb % c_gdn == 0 and tb % SUBLANES == 0
    nt = tlen // tb
    cos2, sin2 = _rope_tables(pos)
    rtabs = _ret_tables(c_ret)

    def wspec(a):
        nd = a.ndim - 1
        return pl.BlockSpec((None,) + a.shape[1:], lambda b, t, _n=nd: (layer,) + (0,) * _n,
                            pipeline_mode=pl.Buffered(1))

    def cspec(a):
        nd = a.ndim
        return pl.BlockSpec(a.shape, lambda b, t, _n=nd: (0,) * _n, pipeline_mode=pl.Buffered(1))

    def ispec(a):
        nd = a.ndim - 1
        return pl.BlockSpec((None,) + a.shape[1:], lambda b, t, _n=nd: (0,) * (_n + 1),
                            pipeline_mode=pl.Buffered(1))

    def ospec(shape):
        nd = len(shape)
        return pl.BlockSpec((None,) + shape, lambda b, t, _n=nd: (b,) + (0,) * _n)

    in_specs = ([pl.BlockSpec((None, tb, D_MODEL), lambda b, t: (b, t, 0)),
                 pl.BlockSpec((tb, DK), lambda b, t: (t, 0)),
                 pl.BlockSpec((tb, DK), lambda b, t: (t, 0))]
                + [ispec(a) for a in init] + [wspec(a) for a in wts] + [cspec(a) for a in rtabs])
    out_shape = (jax.ShapeDtypeStruct((bsz, tlen, D_MODEL), F32),
                 jax.ShapeDtypeStruct((bsz, 1, W), F32),
                 jax.ShapeDtypeStruct((bsz, CONV_W - 1, W), F32),
                 jax.ShapeDtypeStruct((bsz, HEADS, DK, DK), F32),
                 jax.ShapeDtypeStruct((bsz, CONV_W - 1, 3 * W), F32),
                 jax.ShapeDtypeStruct((bsz, HEADS, DK, DK), F32))
    out_specs = (pl.BlockSpec((None, tb, D_MODEL), lambda b, t: (b, t, 0)),
                 ospec((1, W)), ospec((CONV_W - 1, W)), ospec((HEADS, DK, DK)),
                 ospec((CONV_W - 1, 3 * W)), ospec((HEADS, DK, DK)))
    scratch = [pltpu.VMEM((tb, D_MAIN), F32),
               pltpu.VMEM((SUBLANES, N_CONV), F32),
               pltpu.VMEM((tb, W), F32),
               pltpu.VMEM((tb, 3 * W), F32)]
    kern = functools.partial(_prompt_layer_kernel, tb=tb, c_ret=c_ret, c_gdn=c_gdn)
    return pl.pallas_call(
        kern, grid=(bsz, nt), in_specs=in_specs, out_specs=out_specs, out_shape=out_shape,
        scratch_shapes=scratch,
        compiler_params=pltpu.CompilerParams(dimension_semantics=("arbitrary", "arbitrary"),
                                             vmem_limit_bytes=56 * 1024 * 1024),
        name=f"prompt_layer{layer}_t{tlen}",
    )(x, cos2, sin2, *init, *wts, *rtabs)


def _prep_weights(w_in, rg_conv_w, rg_conv_b, rg_w_a, rg_b_a, rg_w_x, rg_b_x, rg_lambda,
                  ret_gn_w, ret_gn_b, gdn_conv_w, gdn_a_log, gdn_dt_bias, gdn_norm_w, w_out, ln_w, ln_b):
    eye = jnp.eye(RG_BLOCKS, dtype=F32)

    def bdiag(w):
        l, n, c, d = w.shape
        return jnp.einsum('lncd,nm->lncmd', w.astype(F32), eye).reshape(l, n * c, n * d)

    pad = LANES - 2 * HEADS
    w_main = w_in[:, :, :D_MAIN].astype(BF16)
    w_ab = jnp.pad(w_in[:, :, D_MAIN:], ((0, 0), (0, 0), (0, pad))).astype(BF16)
    wg = jnp.concatenate([bdiag(rg_w_a), bdiag(rg_w_x)], axis=-1).astype(BF16)
    bg = jnp.concatenate([rg_b_a, rg_b_x], axis=-1)[:, None, :].astype(F32)
    row = lambda a: a[:, None, :].astype(F32)
    padh = lambda a: jnp.pad(a.astype(F32), ((0, 0), (0, LANES - HEADS)))[:, None, :]
    return (w_main, w_ab, wg, bg, row(rg_lambda), rg_conv_w.astype(F32), row(rg_conv_b),
            row(ret_gn_w), row(ret_gn_b), gdn_conv_w.astype(F32), padh(gdn_a_log), padh(gdn_dt_bias),
            row(gdn_norm_w), w_out.astype(BF16), row(ln_w), row(ln_b))


def _sample_kernel(
        x_ref, cos_ref, sin_ref, gam_ref,
        h0_ref, rgb0_ref, sret0_ref, gb0_ref, sgdn0_ref,
        win_ref, wab_ref, wg_ref, bg_ref, lam_ref, rcw_ref, rcb_ref, gnw_ref, gnb_ref,
        gcw_ref, alog_ref, dtb_ref, gnorm_ref, wout_ref, lnw_ref, lnb_ref,
        y_ref, h_ref, rgb_ref, sret_ref, gb_ref, sgdn_ref,
        xcur_ref, u_ref, eg_ref, beta_ref, mix_ref,
        *, bb_rows, n_bb):
    layer = pl.program_id(0)
    bb = pl.program_id(1)

    @pl.when(jnp.logical_and(layer == 0, bb == 0))
    def _load_x():
        xcur_ref[...] = x_ref[...]

    @pl.when(bb == 0)
    def _project():
        xb = xcur_ref[...].astype(BF16)
        u_ref[...] = jnp.dot(xb, win_ref[...], preferred_element_type=F32)
        ab = jnp.dot(xb, wab_ref[...], preferred_element_type=F32)
        eg_ref[...] = jnp.exp(-jnp.exp(alog_ref[...]) * jax.nn.softplus(ab + dtb_ref[...]))
        beta_ref[...] = jax.nn.sigmoid(ab)

        cur = u_ref[:, C_RGX:C_RGX + W]
        xc = rcb_ref[...] + rcw_ref[CONV_W - 1:CONV_W, :] * cur
        for k in range(CONV_W - 1):
            xc = xc + rcw_ref[k:k + 1, :] * rgb0_ref[:, k * W:(k + 1) * W]
        rgb_ref[:, 0:W] = rgb0_ref[:, W:2 * W]
        rgb_ref[:, W:2 * W] = rgb0_ref[:, 2 * W:3 * W]
        rgb_ref[:, 2 * W:3 * W] = cur
        gates = jnp.dot(xc.astype(BF16), wg_ref[...], preferred_element_type=F32) + bg_ref[...]
        r = jax.nn.sigmoid(gates[:, :W])
        i = jax.nn.sigmoid(gates[:, W:])
        log_a = (-RG_C) * r * jax.nn.softplus(-lam_ref[...])
        a = jnp.exp(log_a)
        h = a * h0_ref[...] + jnp.sqrt(1.0 - a * a) * (i * xc)
        h_ref[...] = h
        mix_ref[:, 0:W] = h * _silu(u_ref[:, C_RGZ:C_RGZ + W])

        cosf = cos_ref[...]
        sinf = sin_ref[...]
        for hd in range(HEADS):
            qs = slice(C_RQ + hd * DK, C_RQ + (hd + 1) * DK)
            ks = slice(C_RK + hd * DK, C_RK + (hd + 1) * DK)
            q = u_ref[:, qs]
            k = u_ref[:, ks]
            u_ref[:, qs] = q * cosf + pltpu.roll(q, DK // 2, axis=1) * sinf
            u_ref[:, ks] = (k * cosf + pltpu.roll(k, DK // 2, axis=1) * sinf) * (DK ** -0.5)

        n3 = 3 * W
        for j in range(n3 // LANES):
            cs = slice(C_GQ + j * LANES, C_GQ + (j + 1) * LANES)
            ws = slice(j * LANES, (j + 1) * LANES)
            cur = u_ref[:, cs]
            acc = gcw_ref[CONV_W - 1:CONV_W, ws] * cur
            for k in range(CONV_W - 1):
                acc = acc + gcw_ref[k:k + 1, ws] * gb0_ref[:, k * n3 + j * LANES:k * n3 + (j + 1) * LANES]
            gb_ref[:, j * LANES:(j + 1) * LANES] = gb0_ref[:, n3 + j * LANES:n3 + (j + 1) * LANES]
            gb_ref[:, n3 + j * LANES:n3 + (j + 1) * LANES] = gb0_ref[:, 2 * n3 + j * LANES:2 * n3 + (j + 1) * LANES]
            gb_ref[:, 2 * n3 + j * LANES:2 * n3 + (j + 1) * LANES] = cur
            y = _silu(acc)
            if j < 2 * HEADS:
                y = y * lax.rsqrt(jnp.sum(y * y, axis=-1, keepdims=True) + EPS)
                if j < HEADS:
                    y = y * (DK ** -0.5)
            u_ref[:, cs] = y

    r0 = pl.multiple_of(bb * bb_rows, SUBLANES)
    rows = pl.ds(r0, bb_rows)
    egb = eg_ref[rows, :]
    btb = beta_ref[rows, :]
    for hd in range(HEADS):
        q = u_ref[rows, C_RQ + hd * DK:C_RQ + (hd + 1) * DK]
        k = u_ref[rows, C_RK + hd * DK:C_RK + (hd + 1) * DK]
        v = u_ref[rows, C_RV + hd * DK:C_RV + (hd + 1) * DK]
        z = u_ref[rows, C_RZ + hd * DK:C_RZ + (hd + 1) * DK]
        qk = jnp.sum(q * k, axis=-1, keepdims=True)
        qt = q.T
        kt = k.T
        gam = gam_ref[hd]
        o_rows = []
        for i in range(bb_rows):
            s = sret0_ref[i, hd]
            qc = jnp.broadcast_to(qt[:, i:i + 1], (DK, DK))
            kc = jnp.broadcast_to(kt[:, i:i + 1], (DK, DK))
            qs_ = jnp.sum(qc * s, axis=0, keepdims=True)
            o_rows.append(qk[i:i + 1, :] * v[i:i + 1, :] + gam * qs_)
            sret_ref[i, hd] = gam * s + kc * v[i:i + 1, :]
        o = jnp.concatenate(o_rows, axis=0)
        mu = jnp.mean(o, axis=-1, keepdims=True)
        d = o - mu
        on = d * lax.rsqrt(jnp.mean(d * d, axis=-1, keepdims=True) + EPS)
        gs = slice(hd * DK, (hd + 1) * DK)
        mix_ref[rows, W + hd * DK:W + (hd + 1) * DK] = (on * gnw_ref[:, gs] + gnb_ref[:, gs]) * _silu(z)

        q = u_ref[rows, C_GQ + hd * DK:C_GQ + (hd + 1) * DK]
        k = u_ref[rows, C_GK + hd * DK:C_GK + (hd + 1) * DK]
        v = u_ref[rows, C_GV + hd * DK:C_GV + (hd + 1) * DK]
        z = u_ref[rows, C_GZ + hd * DK:C_GZ + (hd + 1) * DK]
        qk = jnp.sum(q * k, axis=-1, keepdims=True)
        qt = q.T
        kt = k.T
        eg = jnp.broadcast_to(egb[:, hd:hd + 1], (bb_rows, DK))
        beta = jnp.broadcast_to(btb[:, HEADS + hd:HEADS + hd + 1], (bb_rows, DK))
        o_rows = []
        for i in range(bb_rows):
            s = sgdn0_ref[i, hd]
            qc = jnp.broadcast_to(qt[:, i:i + 1], (DK, DK))
            kc = jnp.broadcast_to(kt[:, i:i + 1], (DK, DK))
            ks_ = jnp.sum(kc * s, axis=0, keepdims=True)
            qs_ = jnp.sum(qc * s, axis=0, keepdims=True)
            eg_i = eg[i:i + 1, :]
            v_new = beta[i:i + 1, :] * (v[i:i + 1, :] - eg_i * ks_)
            o_rows.append(eg_i * qs_ + qk[i:i + 1, :] * v_new)
            sgdn_ref[i, hd] = s * eg_i + kc * v_new
        o = jnp.concatenate(o_rows, axis=0)
        on = o * lax.rsqrt(jnp.mean(o * o, axis=-1, keepdims=True) + EPS) * gnorm_ref[...]
        mix_ref[rows, 2 * W + hd * DK:2 * W + (hd + 1) * DK] = on * _silu(z)

    @pl.when(bb == n_bb - 1)
    def _finish():
        out = jnp.dot(mix_ref[...].astype(BF16), wout_ref[...], preferred_element_type=F32)
        r = ALPHA * xcur_ref[...] + out
        mu = jnp.mean(r, axis=-1, keepdims=True)
        d = r - mu
        var = jnp.mean(d * d, axis=-1, keepdims=True)
        y = d * lax.rsqrt(var + EPS) * lnw_ref[...] + lnb_ref[...]
        xcur_ref[...] = y
        y_ref[...] = y


def _sample_path(x, h0, rgb0, sret0, gb0, sgdn0, wts, *, bb_rows):
    nb = x.shape[0]
    n_bb = nb // bb_rows
    pos = jnp.arange(1, dtype=F32) + float(PAST_LEN)
    cos2, sin2 = _rope_tables(pos)
    lg = jnp.log1p(-jnp.exp2(-5.0 - jnp.arange(HEADS, dtype=F32)))
    gam = jnp.broadcast_to(jnp.exp(lg)[:, None, None], (HEADS, 1, DK))

    def const(a):
        nd = a.ndim
        return pl.BlockSpec(a.shape, lambda l, b, _n=nd: (0,) * _n)

    def per_layer(a, prefetch=False):
        nd = a.ndim - 1
        return pl.BlockSpec((None,) + a.shape[1:], lambda l, b, _n=nd: (l,) + (0,) * _n,
                            pipeline_mode=pl.Buffered(2 if prefetch else 1))

    def per_block(a):
        return pl.BlockSpec((None, bb_rows) + a.shape[2:], lambda l, b: (l, b, 0, 0, 0))

    ins = (x, cos2, sin2, gam, h0, rgb0, sret0, gb0, sgdn0) + tuple(wts)
    in_specs = ([const(x), const(cos2), const(sin2), const(gam),
                 per_layer(h0), per_layer(rgb0), per_block(sret0), per_layer(gb0), per_block(sgdn0)]
                + [per_layer(a, prefetch=(i == 0)) for i, a in enumerate(wts)])
    out_shape = (jax.ShapeDtypeStruct(x.shape, F32),
                 jax.ShapeDtypeStruct(h0.shape, F32), jax.ShapeDtypeStruct(rgb0.shape, F32),
                 jax.ShapeDtypeStruct(sret0.shape, F32), jax.ShapeDtypeStruct(gb0.shape, F32),
                 jax.ShapeDtypeStruct(sgdn0.shape, F32))
    out_specs = (const(x), per_layer(h0), per_layer(rgb0), per_block(sret0), per_layer(gb0), per_block(sgdn0))
    scratch = [pltpu.VMEM((nb, D_MODEL), F32),
               pltpu.VMEM((nb, D_MAIN), F32),
               pltpu.VMEM((nb, LANES), F32),
               pltpu.VMEM((nb, LANES), F32),
               pltpu.VMEM((nb, 3 * W), F32)]
    kern = functools.partial(_sample_kernel, bb_rows=bb_rows, n_bb=n_bb)
    return pl.pallas_call(
        kern, grid=(DEPTH, n_bb), in_specs=in_specs, out_specs=out_specs, out_shape=out_shape,
        scratch_shapes=scratch,
        compiler_params=pltpu.CompilerParams(dimension_semantics=("arbitrary", "arbitrary"),
                                             vmem_limit_bytes=58 * 1024 * 1024),
        name="sample_path",
    )(*ins)


def kernel(x_prompt, x_sample, state_rglru_h, state_rglru_conv, state_ret, state_gdn_conv, state_gdn,
           meta_tokens, w_in, rg_conv_w, rg_conv_b, rg_w_a, rg_b_a, rg_w_x, rg_b_x, rg_lambda,
           ret_gn_w, ret_gn_b, gdn_conv_w, gdn_a_log, gdn_dt_bias, gdn_norm_w, w_out, ln_w, ln_b):
    bp, seq, _ = x_prompt.shape
    nb = x_sample.shape[0]
    wts = _prep_weights(w_in, rg_conv_w, rg_conv_b, rg_w_a, rg_b_a, rg_w_x, rg_b_x, rg_lambda,
                        ret_gn_w, ret_gn_b, gdn_conv_w, gdn_a_log, gdn_dt_bias, gdn_norm_w, w_out, ln_w, ln_b)

    pos = jnp.arange(N_META + seq, dtype=F32)
    zeros = lambda *s: jnp.zeros(s, F32)
    init0 = (zeros(1, 1, W), zeros(1, CONV_W - 1, W), zeros(1, HEADS, DK, DK),
             zeros(1, CONV_W - 1, 3 * W), zeros(1, HEADS, DK, DK))
    xm = meta_tokens.astype(x_prompt.dtype)[None]
    xp = x_prompt
    new_p = [[] for _ in range(5)]
    for l in range(DEPTH):
        xm, *st_m = _prompt_layer(l, xm, pos[:N_META], init0, wts, tb=N_META, c_ret=N_META, c_gdn=N_META)
        xp, *st_p = _prompt_layer(l, xp, pos[N_META:], tuple(st_m), wts, tb=256, c_ret=256, c_gdn=64)
        for j in range(5):
            new_p[j].append(st_p[j])
    sp = [jnp.stack(a) for a in new_p]
    sp[0] = sp[0].reshape(DEPTH, bp, W)

    ys, sh, srgb, sret, sgb, sgdn = _sample_path(
        x_sample.reshape(nb, D_MODEL), state_rglru_h,
        state_rglru_conv.reshape(DEPTH, nb, (CONV_W - 1) * W), state_ret,
        state_gdn_conv.reshape(DEPTH, nb, (CONV_W - 1) * 3 * W), state_gdn, wts, bb_rows=8)
    return (xp, ys.reshape(x_sample.shape), sp[0], sp[1], sp[2], sp[3], sp[4],
            sh, srgb.reshape(state_rglru_conv.shape), sret, sgb.reshape(state_gdn_conv.shape), sgdn)
```

```python
import functools
import math

import jax
import jax.numpy as jnp
from jax import lax
from jax.experimental import pallas as pl
from jax.experimental.pallas import tpu as pltpu

F32 = jnp.float32
BF16 = jnp.bfloat16

D_MODEL = 1024
DEPTH = 4
N_META = 16
PAST_LEN = 16384
W = 512
CONV_W = 4
RG_BLOCKS = 8
RG_C = 8.0
HEADS = 4
DK = W // HEADS
ROPE_BASE = 10000.0
EPS = 1e-6
ALPHA = (2.0 * DEPTH) ** 0.25
D_MAIN = 10 * W
LANES = 128
SUBLANES = 8
N_CONV = 4 * W

C_RGX, C_RGZ, C_RQ, C_RK, C_RV, C_RZ, C_GQ, C_GK, C_GV, C_GZ = (i * W for i in range(10))


def _bdot(a, b):
    return jnp.dot(a.astype(BF16), b.astype(BF16), preferred_element_type=F32)


def _bdot_nt(a, b):
    return lax.dot_general(a.astype(BF16), b.astype(BF16), (((1,), (1,)), ((), ())), preferred_element_type=F32)


def _bdot_tn(a, b):
    return lax.dot_general(a.astype(BF16), b.astype(BF16), (((0,), (0,)), ((), ())), preferred_element_type=F32)


def _bmm(a, b):
    return jnp.einsum('nij,njk->nik', a.astype(BF16), b.astype(BF16), preferred_element_type=F32)


def _bmm_nt(a, b):
    return jnp.einsum('nik,njk->nij', a.astype(BF16), b.astype(BF16), preferred_element_type=F32)


def _silu(x):
    return x * jax.nn.sigmoid(x)


def _scan_rows(a, b, h0):
    n, lanes = a.shape
    g = n // SUBLANES
    a3 = a.reshape(g, SUBLANES, lanes)
    b3 = b.reshape(g, SUBLANES, lanes)
    sub = lax.broadcasted_iota(jnp.int32, a3.shape, 1)
    s = 1
    while s < SUBLANES:
        m = sub >= s
        a_s = pltpu.roll(a3, s, axis=1)
        b_s = pltpu.roll(b3, s, axis=1)
        b3 = jnp.where(m, a3 * b_s + b3, b3)
        a3 = jnp.where(m, a3 * a_s, a3)
        s *= 2
    carry = h0
    hs = []
    for i in range(g):
        hi = a3[i] * carry + b3[i]
        carry = hi[SUBLANES - 1:SUBLANES, :]
        hs.append(hi)
    return jnp.concatenate(hs, axis=0), carry


def _causal_conv_strip(x, hist, taps):
    r8 = lax.broadcasted_iota(jnp.int32, hist.shape, 0)
    acc = taps[CONV_W - 1] * x
    for d in range(1, CONV_W):
        xs = pltpu.roll(x, d, axis=0)
        top = jnp.where(r8 < d, pltpu.roll(hist, d, axis=0), xs[0:SUBLANES])
        acc = acc + taps[CONV_W - 1 - d] * jnp.concatenate([top, xs[SUBLANES:]], axis=0)
    return acc


def _prompt_layer_kernel(
        x_ref, cos_ref, sin_ref,
        h0_ref, rgb0_ref, sret0_ref, gb0_ref, sgdn0_ref,
        win_ref, wab_ref, wg_ref, bg_ref, lam_ref, rcw_ref, rcb_ref, gnw_ref, gnb_ref,
        gcw_ref, alog_ref, dtb_ref, gnorm_ref, wout_ref, lnw_ref, lnb_ref,
        rdecay_ref, rqdec_ref, rkdec_ref, rsdec_ref,
        y_ref, h_ref, rgb_ref, sret_ref, gb_ref, sgdn_ref,
        u_ref, hist_ref, xc_ref, mix_ref,
        *, tb, c_ret, c_gdn):
    t = pl.program_id(1)

    @pl.when(t == 0)
    def _init():
        h_ref[...] = h0_ref[...]
        sret_ref[...] = sret0_ref[...]
        sgdn_ref[...] = sgdn0_ref[...]
        hist_ref[SUBLANES - 3:SUBLANES, 0:W] = rgb0_ref[...]
        hist_ref[SUBLANES - 3:SUBLANES, W:N_CONV] = gb0_ref[...]

    x = x_ref[...]
    xb = x.astype(BF16)

    def project(c0, c1):
        u_ref[:, c0:c1] = jnp.dot(xb, win_ref[:, c0:c1], preferred_element_type=F32)

    project(C_RGX, C_RQ)
    ab = jnp.dot(xb, wab_ref[...], preferred_element_type=F32)
    project(C_GQ, D_MAIN)

    for j in range(W // LANES):
        cs = slice(C_RGX + j * LANES, C_RGX + (j + 1) * LANES)
        ws = slice(j * LANES, (j + 1) * LANES)
        xj = u_ref[:, cs]
        taps = [rcw_ref[k:k + 1, ws] for k in range(CONV_W)]
        xc_ref[:, ws] = _causal_conv_strip(xj, hist_ref[:, ws], taps) + rcb_ref[:, ws]
        hist_ref[:, ws] = xj[tb - SUBLANES:tb, :]
        rgb_ref[:, ws] = xj[tb - 3:tb, :]
    gates = jnp.dot(xc_ref[...].astype(BF16), wg_ref[...], preferred_element_type=F32) + bg_ref[...]
    for j in range(W // LANES):
        zs = slice(C_RGZ + j * LANES, C_RGZ + (j + 1) * LANES)
        ws = slice(j * LANES, (j + 1) * LANES)
        r = jax.nn.sigmoid(gates[:, j * LANES:(j + 1) * LANES])
        i = jax.nn.sigmoid(gates[:, W + j * LANES:W + (j + 1) * LANES])
        log_a = (-RG_C) * r * jax.nn.softplus(-lam_ref[:, ws])
        a = jnp.exp(log_a)
        b = jnp.sqrt(1.0 - a * a) * (i * xc_ref[:, ws])
        h, h_last = _scan_rows(a, b, h_ref[:, ws])
        h_ref[:, ws] = h_last
        mix_ref[:, ws] = h * _silu(u_ref[:, zs])

    project(C_RQ, C_GQ)
    cosf = cos_ref[...]
    sinf = sin_ref[...]
    for hd in range(HEADS):
        q = u_ref[:, C_RQ + hd * DK:C_RQ + (hd + 1) * DK]
        k = u_ref[:, C_RK + hd * DK:C_RK + (hd + 1) * DK]
        v = u_ref[:, C_RV + hd * DK:C_RV + (hd + 1) * DK]
        z = u_ref[:, C_RZ + hd * DK:C_RZ + (hd + 1) * DK]
        q = q * cosf + pltpu.roll(q, DK // 2, axis=1) * sinf
        k = (k * cosf + pltpu.roll(k, DK // 2, axis=1) * sinf) * (DK ** -0.5)
        s = sret_ref[hd]
        outs = []
        for c in range(tb // c_ret):
            rs = slice(c * c_ret, (c + 1) * c_ret)
            qc, kc, vc = q[rs], k[rs], v[rs]
            sc = _bdot_nt(qc, kc) * rdecay_ref[hd]
            outs.append(_bdot(sc, vc) + _bdot(qc * rqdec_ref[hd], s))
            s = s * rsdec_ref[hd] + _bdot_tn(kc * rkdec_ref[hd], vc)
        sret_ref[hd] = s
        o = outs[0] if len(outs) == 1 else jnp.concatenate(outs, axis=0)
        mu = jnp.mean(o, axis=-1, keepdims=True)
        d = o - mu
        on = d * lax.rsqrt(jnp.mean(d * d, axis=-1, keepdims=True) + EPS)
        gs = slice(hd * DK, (hd + 1) * DK)
        mix_ref[:, W + hd * DK:W + (hd + 1) * DK] = (on * gnw_ref[:, gs] + gnb_ref[:, gs]) * _silu(z)

    for j in range(3 * W // LANES):
        cs = slice(C_GQ + j * LANES, C_GQ + (j + 1) * LANES)
        ws = slice(j * LANES, (j + 1) * LANES)
        hs = slice(W + j * LANES, W + (j + 1) * LANES)
        xj = u_ref[:, cs]
        taps = [gcw_ref[k:k + 1, ws] for k in range(CONV_W)]
        u_ref[:, cs] = _silu(_causal_conv_strip(xj, hist_ref[:, hs], taps))
        hist_ref[:, hs] = xj[tb - SUBLANES:tb, :]
        gb_ref[:, ws] = xj[tb - 3:tb, :]

    nc = tb // c_gdn
    glog = -jnp.exp(alog_ref[...]) * jax.nn.softplus(ab + dtb_ref[...])
    beta_all = jax.nn.sigmoid(ab)
    in_chunk = lax.broadcasted_iota(jnp.int32, (tb, LANES), 0) % c_gdn
    gcs = glog
    sh = 1
    while sh < c_gdn:
        gcs = gcs + jnp.where(in_chunk >= sh, pltpu.roll(gcs, sh, axis=0), 0.0)
        sh *= 2

    ri = lax.broadcasted_iota(jnp.int32, (c_gdn, c_gdn), 0)
    ci = lax.broadcasted_iota(jnp.int32, (c_gdn, c_gdn), 1)
    tril = ri >= ci
    strict = ri > ci
    eye_f = jnp.where(ri == ci, 1.0, 0.0).astype(F32)
    n_lv = int(math.ceil(math.log2(c_gdn)))

    names = ("q", "k", "kb", "vb", "kbe", "qe", "kd", "gl", "dec")
    parts = {n: [[None] * HEADS for _ in range(nc)] for n in names}
    for hd in range(HEADS):
        q = u_ref[:, C_GQ + hd * DK:C_GQ + (hd + 1) * DK]
        k = u_ref[:, C_GK + hd * DK:C_GK + (hd + 1) * DK]
        v = u_ref[:, C_GV + hd * DK:C_GV + (hd + 1) * DK]
        q = q * lax.rsqrt(jnp.sum(q * q, axis=-1, keepdims=True) + EPS) * (DK ** -0.5)
        k = k * lax.rsqrt(jnp.sum(k * k, axis=-1, keepdims=True) + EPS)
        gc = jnp.broadcast_to(gcs[:, hd:hd + 1], (tb, LANES))
        beta = jnp.broadcast_to(beta_all[:, HEADS + hd:HEADS + hd + 1], (tb, LANES))
        egc = jnp.exp(gc)
        kb = k * beta
        vb = v * beta
        kbe = kb * egc
        qe = q * egc
        for c in range(nc):
            rs = slice(c * c_gdn, (c + 1) * c_gdn)
            gcc = gc[rs]
            gl = gcc[c_gdn - 1:c_gdn, :]
            diff = gcc[:, :c_gdn] - gcc.T[:c_gdn, :]
            parts["dec"][c][hd] = jnp.where(tril, jnp.exp(jnp.where(tril, diff, 0.0)), 0.0)
            parts["gl"][c][hd] = gl
            parts["kd"][c][hd] = k[rs] * jnp.exp(gl - gcc)
            for n, val in (("q", q), ("k", k), ("kb", kb), ("vb", vb), ("kbe", kbe), ("qe", qe)):
                parts[n][c][hd] = val[rs]
    st = {n: jnp.stack([parts[n][c][hd] for c in range(nc) for hd in range(HEADS)]) for n in names}

    a_low = jnp.where(strict, _bmm_nt(st["kb"], st["k"]) * st["dec"], 0.0)
    att = _bmm_nt(st["q"], st["k"]) * st["dec"]
    pt = jnp.concatenate([-a_low, jnp.broadcast_to(eye_f, a_low.shape)], axis=-1)
    right = lax.broadcasted_iota(jnp.int32, (c_gdn, 2 * c_gdn), 1) >= c_gdn
    for _ in range(n_lv):
        ptb = pt.astype(BF16)
        r = jnp.einsum('nij,njk->nik', ptb[..., :c_gdn], ptb, preferred_element_type=F32)
        pt = r + jnp.where(right, pt, 0.0)
    tinv = pt[..., c_gdn:]
    sol = _bmm(tinv, jnp.concatenate([st["vb"], st["kbe"]], axis=-1))
    uu, ww = sol[..., :DK], sol[..., DK:]

    s = sgdn_ref[...]
    outs = []
    for c in range(nc):
        hs4 = slice(c * HEADS, (c + 1) * HEADS)
        ws_ = _bmm(jnp.concatenate([ww[hs4], st["qe"][hs4]], axis=1), s)
        v_new = uu[hs4] - ws_[:, :c_gdn]
        outs.append(ws_[:, c_gdn:] + _bmm(att[hs4], v_new))
        kv = jnp.einsum('hik,hiv->hkv', st["kd"][hs4].astype(BF16), v_new.astype(BF16),
                        preferred_element_type=F32)
        s = s * jnp.exp(st["gl"][hs4]) + kv
    sgdn_ref[...] = s
    for hd in range(HEADS):
        z = u_ref[:, C_GZ + hd * DK:C_GZ + (hd + 1) * DK]
        o = outs[0][hd] if nc == 1 else jnp.concatenate([outs[c][hd] for c in range(nc)], axis=0)
        on = o * lax.rsqrt(jnp.mean(o * o, axis=-1, keepdims=True) + EPS) * gnorm_ref[...]
        mix_ref[:, 2 * W + hd * DK:2 * W + (hd + 1) * DK] = on * _silu(z)

    out = jnp.dot(mix_ref[...].astype(BF16), wout_ref[...], preferred_element_type=F32)
    r = ALPHA * x + out
    mu = jnp.mean(r, axis=-1, keepdims=True)
    d = r - mu
    var = jnp.mean(d * d, axis=-1, keepdims=True)
    y_ref[...] = d * lax.rsqrt(var + EPS) * lnw_ref[...] + lnb_ref[...]


def _ret_tables(c):
    lg = jnp.log1p(-jnp.exp2(-5.0 - jnp.arange(HEADS, dtype=F32)))[:, None, None]
    idx = jnp.arange(c, dtype=F32)
    diff = idx[:, None] - idx[None, :]
    decay = jnp.where(diff >= 0, jnp.exp(lg * jnp.maximum(diff, 0.0)), 0.0)
    q_dec = jnp.broadcast_to(jnp.exp(lg[:, 0] * (idx + 1.0))[:, :, None], (HEADS, c, DK))
    k_dec = jnp.broadcast_to(jnp.exp(lg[:, 0] * (c - 1.0 - idx))[:, :, None], (HEADS, c, DK))
    s_dec = jnp.broadcast_to(jnp.exp(lg * c), (HEADS, 1, DK))
    return decay, q_dec, k_dec, s_dec


def _rope_tables(pos):
    half = DK // 2
    inv = ROPE_BASE ** (-jnp.arange(half, dtype=F32) / half)
    ang = pos[:, None] * inv[None, :]
    cos, sin = jnp.cos(ang), jnp.sin(ang)
    return jnp.concatenate([cos, cos], -1), jnp.concatenate([-sin, sin], -1)


def _prompt_layer(layer, x, pos, init, wts, *, tb, c_ret, c_gdn):
    bsz, tlen, _ = x.shape
    assert tlen % tb == 0 and tb % c_ret == 0 and tb % c_gdn == 0 and tb % SUBLANES == 0
    nt = tlen // tb
    cos2, sin2 = _rope_tables(pos)
    rtabs = _ret_tables(c_ret)

    def wspec(a):
        nd = a.ndim - 1
        return pl.BlockSpec((None,) + a.shape[1:], lambda b, t, _n=nd: (layer,) + (0,) * _n,
                            pipeline_mode=pl.Buffered(1))

    def cspec(a):
        nd = a.ndim
        return pl.BlockSpec(a.shape, lambda b, t, _n=nd: (0,) * _n, pipeline_mode=pl.Buffered(1))

    def ispec(a):
        nd = a.ndim - 1
        return pl.BlockSpec((None,) + a.shape[1:], lambda b, t, _n=nd: (0,) * (_n + 1),
                            pipeline_mode=pl.Buffered(1))

    def ospec(shape):
        nd = len(shape)
        return pl.BlockSpec((None,) + shape, lambda b, t, _n=nd: (b,) + (0,) * _n)

    in_specs = ([pl.BlockSpec((None, tb, D_MODEL), lambda b, t: (b, t, 0)),
                 pl.BlockSpec((tb, DK), lambda b, t: (t, 0)),
                 pl.BlockSpec((tb, DK), lambda b, t: (t, 0))]
                + [ispec(a) for a in init] + [wspec(a) for a in wts] + [cspec(a) for a in rtabs])
    out_shape = (jax.ShapeDtypeStruct((bsz, tlen, D_MODEL), F32),
                 jax.ShapeDtypeStruct((bsz, 1, W), F32),
                 jax.ShapeDtypeStruct((bsz, CONV_W - 1, W), F32),
                 jax.ShapeDtypeStruct((bsz, HEADS, DK, DK), F32),
                 jax.ShapeDtypeStruct((bsz, CONV_W - 1, 3 * W), F32),
                 jax.ShapeDtypeStruct((bsz, HEADS, DK, DK), F32))
    out_specs = (pl.BlockSpec((None, tb, D_MODEL), lambda b, t: (b, t, 0)),
                 ospec((1, W)), ospec((CONV_W - 1, W)), ospec((HEADS, DK, DK)),
                 ospec((CONV_W - 1, 3 * W)), ospec((HEADS, DK, DK)))
    scratch = [pltpu.VMEM((tb, D_MAIN), F32),
               pltpu.VMEM((SUBLANES, N_CONV), F32),
               pltpu.VMEM((tb, W), F32),
               pltpu.VMEM((tb, 3 * W), F32)]
    kern = functools.partial(_prompt_layer_kernel, tb=tb, c_ret=c_ret, c_gdn=c_gdn)
    return pl.pallas_call(
        kern, grid=(bsz, nt), in_specs=in_specs, out_specs=out_specs, out_shape=out_shape,
        scratch_shapes=scratch,
        compiler_params=pltpu.CompilerParams(dimension_semantics=("arbitrary", "arbitrary"),
                                             vmem_limit_bytes=56 * 1024 * 1024),
        name=f"prompt_layer{layer}_t{tlen}",
    )(x, cos2, sin2, *init, *wts, *rtabs)


def _cast_kernel(x_ref, o_ref):
    o_ref[...] = x_ref[...].astype(o_ref.dtype)


def _cast_main_columns(w_in):
    depth, d_model, _ = w_in.shape
    cols = 4 * LANES
    return pl.pallas_call(
        _cast_kernel, grid=(depth, D_MAIN // cols),
        in_specs=[pl.BlockSpec((None, d_model, cols), lambda l, j: (l, 0, j))],
        out_specs=pl.BlockSpec((None, d_model, cols), lambda l, j: (l, 0, j)),
        out_shape=jax.ShapeDtypeStruct((depth, d_model, D_MAIN), BF16),
        name="cast_w_in",
    )(w_in)


def _prep_weights(w_in, rg_conv_w, rg_conv_b, rg_w_a, rg_b_a, rg_w_x, rg_b_x, rg_lambda,
                  ret_gn_w, ret_gn_b, gdn_conv_w, gdn_a_log, gdn_dt_bias, gdn_norm_w, w_out, ln_w, ln_b):
    eye = jnp.eye(RG_BLOCKS, dtype=F32)

    def bdiag(w):
        l, n, c, d = w.shape
        return jnp.einsum('lncd,nm->lncmd', w.astype(F32), eye).reshape(l, n * c, n * d)

    pad = LANES - 2 * HEADS
    w_main = _cast_main_columns(w_in)
    w_ab = jnp.pad(w_in[:, :, D_MAIN:], ((0, 0), (0, 0), (0, pad))).astype(BF16)
    wg = jnp.concatenate([bdiag(rg_w_a), bdiag(rg_w_x)], axis=-1).astype(BF16)
    bg = jnp.concatenate([rg_b_a, rg_b_x], axis=-1)[:, None, :].astype(F32)
    row = lambda a: a[:, None, :].astype(F32)
    padh = lambda a: jnp.pad(a.astype(F32), ((0, 0), (0, LANES - HEADS)))[:, None, :]
    return (w_main, w_ab, wg, bg, row(rg_lambda), rg_conv_w.astype(F32), row(rg_conv_b),
            row(ret_gn_w), row(ret_gn_b), gdn_conv_w.astype(F32), padh(gdn_a_log), padh(gdn_dt_bias),
            row(gdn_norm_w), w_out.astype(BF16), row(ln_w), row(ln_b))


def _sample_kernel(
        x_ref, cos_ref, sin_ref, gam_ref,
        h0_ref, rgb0_ref, sret0_ref, gb0_ref, sgdn0_ref,
        win_ref, wab_ref, wg_ref, bg_ref, lam_ref, rcw_ref, rcb_ref, gnw_ref, gnb_ref,
        gcw_ref, alog_ref, dtb_ref, gnorm_ref, wout_ref, lnw_ref, lnb_ref,
        y_ref, h_ref, rgb_ref, sret_ref, gb_ref, sgdn_ref,
        xcur_ref, u_ref, eg_ref, beta_ref, mix_ref,
        *, bb_rows, n_bb):
    layer = pl.program_id(0)
    bb = pl.program_id(1)

    @pl.when(jnp.logical_and(layer == 0, bb == 0))
    def _load_x():
        xcur_ref[...] = x_ref[...]

    @pl.when(bb == 0)
    def _project():
        xb = xcur_ref[...].astype(BF16)
        u_ref[...] = jnp.dot(xb, win_ref[...], preferred_element_type=F32)
        ab = jnp.dot(xb, wab_ref[...], preferred_element_type=F32)
        eg_ref[...] = jnp.exp(-jnp.exp(alog_ref[...]) * jax.nn.softplus(ab + dtb_ref[...]))
        beta_ref[...] = jax.nn.sigmoid(ab)

        cur = u_ref[:, C_RGX:C_RGX + W]
        xc = rcb_ref[...] + rcw_ref[CONV_W - 1:CONV_W, :] * cur
        for k in range(CONV_W - 1):
            xc = xc + rcw_ref[k:k + 1, :] * rgb0_ref[:, k * W:(k + 1) * W]
        rgb_ref[:, 0:W] = rgb0_ref[:, W:2 * W]
        rgb_ref[:, W:2 * W] = rgb0_ref[:, 2 * W:3 * W]
        rgb_ref[:, 2 * W:3 * W] = cur
        gates = jnp.dot(xc.astype(BF16), wg_ref[...], preferred_element_type=F32) + bg_ref[...]
        r = jax.nn.sigmoid(gates[:, :W])
        i = jax.nn.sigmoid(gates[:, W:])
        log_a = (-RG_C) * r * jax.nn.softplus(-lam_ref[...])
        a = jnp.exp(log_a)
        h = a * h0_ref[...] + jnp.sqrt(1.0 - a * a) * (i * xc)
        h_ref[...] = h
        mix_ref[:, 0:W] = h * _silu(u_ref[:, C_RGZ:C_RGZ + W])

        cosf = cos_ref[...]
        sinf = sin_ref[...]
        for hd in range(HEADS):
            qs = slice(C_RQ + hd * DK, C_RQ + (hd + 1) * DK)
            ks = slice(C_RK + hd * DK, C_RK + (hd + 1) * DK)
            q = u_ref[:, qs]
            k = u_ref[:, ks]
            u_ref[:, qs] = q * cosf + pltpu.roll(q, DK // 2, axis=1) * sinf
            u_ref[:, ks] = (k * cosf + pltpu.roll(k, DK // 2, axis=1) * sinf) * (DK ** -0.5)

        n3 = 3 * W
        for j in range(n3 // LANES):
            cs = slice(C_GQ + j * LANES, C_GQ + (j + 1) * LANES)
            ws = slice(j * LANES, (j + 1) * LANES)
            cur = u_ref[:, cs]
            acc = gcw_ref[CONV_W - 1:CONV_W, ws] * cur
            for k in range(CONV_W - 1):
                acc = acc + gcw_ref[k:k + 1, ws] * gb0_ref[:, k * n3 + j * LANES:k * n3 + (j + 1) * LANES]
            gb_ref[:, j * LANES:(j + 1) * LANES] = gb0_ref[:, n3 + j * LANES:n3 + (j + 1) * LANES]
            gb_ref[:, n3 + j * LANES:n3 + (j + 1) * LANES] = gb0_ref[:, 2 * n3 + j * LANES:2 * n3 + (j + 1) * LANES]
            gb_ref[:, 2 * n3 + j * LANES:2 * n3 + (j + 1) * LANES] = cur
            y = _silu(acc)
            if j < 2 * HEADS:
                y = y * lax.rsqrt(jnp.sum(y * y, axis=-1, keepdims=True) + EPS)
                if j < HEADS:
                    y = y * (DK ** -0.5)
            u_ref[:, cs] = y

    r0 = pl.multiple_of(bb * bb_rows, SUBLANES)
    rows = pl.ds(r0, bb_rows)
    egb = eg_ref[rows, :]
    btb = beta_ref[rows, :]
    rid = lax.broadcasted_iota(jnp.int32, (bb_rows, DK), 0)
    rid2 = lax.broadcasted_iota(jnp.int32, (2 * bb_rows, DK), 0)

    def own_rows(x):
        return jnp.concatenate([jnp.where(rid == i, x, 0.0) for i in range(bb_rows)], axis=1)

    for hd in range(HEADS):
        q = u_ref[rows, C_RQ + hd * DK:C_RQ + (hd + 1) * DK]
        k = u_ref[rows, C_RK + hd * DK:C_RK + (hd + 1) * DK]
        v = u_ref[rows, C_RV + hd * DK:C_RV + (hd + 1) * DK]
        z = u_ref[rows, C_RZ + hd * DK:C_RZ + (hd + 1) * DK]
        qk = jnp.sum(q * k, axis=-1, keepdims=True)
        gam = gam_ref[hd]
        qs_ = jnp.zeros((bb_rows, DK), F32)
        for i in range(bb_rows):
            qs_ = jnp.where(rid == i, _bdot(q, sret0_ref[i, hd]), qs_)
        o = qk * v + gam * qs_
        kv = _bdot_tn(k, own_rows(v))
        for i in range(bb_rows):
            sret_ref[i, hd] = gam * sret0_ref[i, hd] + kv[:, i * DK:(i + 1) * DK]
        mu = jnp.mean(o, axis=-1, keepdims=True)
        d = o - mu
        on = d * lax.rsqrt(jnp.mean(d * d, axis=-1, keepdims=True) + EPS)
        gs = slice(hd * DK, (hd + 1) * DK)
        mix_ref[rows, W + hd * DK:W + (hd + 1) * DK] = (on * gnw_ref[:, gs] + gnb_ref[:, gs]) * _silu(z)

        q = u_ref[rows, C_GQ + hd * DK:C_GQ + (hd + 1) * DK]
        k = u_ref[rows, C_GK + hd * DK:C_GK + (hd + 1) * DK]
        v = u_ref[rows, C_GV + hd * DK:C_GV + (hd + 1) * DK]
        z = u_ref[rows, C_GZ + hd * DK:C_GZ + (hd + 1) * DK]
        qk = jnp.sum(q * k, axis=-1, keepdims=True)
        eg = jnp.broadcast_to(egb[:, hd:hd + 1], (bb_rows, DK))
        beta = jnp.broadcast_to(btb[:, HEADS + hd:HEADS + hd + 1], (bb_rows, DK))
        kq = jnp.concatenate([k, q], axis=0)
        kqs = jnp.zeros((2 * bb_rows, DK), F32)
        for i in range(bb_rows):
            kqs = jnp.where(rid2 % bb_rows == i, _bdot(kq, sgdn0_ref[i, hd]), kqs)
        v_new = beta * (v - eg * kqs[:bb_rows])
        o = eg * kqs[bb_rows:] + qk * v_new
        kv = _bdot_tn(k, own_rows(v_new))
        for i in range(bb_rows):
            sgdn_ref[i, hd] = sgdn0_ref[i, hd] * eg[i:i + 1, :] + kv[:, i * DK:(i + 1) * DK]
        on = o * lax.rsqrt(jnp.mean(o * o, axis=-1, keepdims=True) + EPS) * gnorm_ref[...]
        mix_ref[rows, 2 * W + hd * DK:2 * W + (hd + 1) * DK] = on * _silu(z)

    @pl.when(bb == n_bb - 1)
    def _finish():
        out = jnp.dot(mix_ref[...].astype(BF16), wout_ref[...], preferred_element_type=F32)
        r = ALPHA * xcur_ref[...] + out
        mu = jnp.mean(r, axis=-1, keepdims=True)
        d = r - mu
        var = jnp.mean(d * d, axis=-1, keepdims=True)
        y = d * lax.rsqrt(var + EPS) * lnw_ref[...] + lnb_ref[...]
        xcur_ref[...] = y
        y_ref[...] = y


def _sample_path(x, h0, rgb0, sret0, gb0, sgdn0, wts, *, bb_rows):
    nb = x.shape[0]
    n_bb = nb // bb_rows
    pos = jnp.arange(1, dtype=F32) + float(PAST_LEN)
    cos2, sin2 = _rope_tables(pos)
    lg = jnp.log1p(-jnp.exp2(-5.0 - jnp.arange(HEADS, dtype=F32)))
    gam = jnp.broadcast_to(jnp.exp(lg)[:, None, None], (HEADS, 1, DK))

    def const(a):
        nd = a.ndim
        return pl.BlockSpec(a.shape, lambda l, b, _n=nd: (0,) * _n)

    def per_layer(a, prefetch=False):
        nd = a.ndim - 1
        return pl.BlockSpec((None,) + a.shape[1:], lambda l, b, _n=nd: (l,) + (0,) * _n,
                            pipeline_mode=pl.Buffered(2 if prefetch else 1))

    def per_block(a):
        return pl.BlockSpec((None, bb_rows) + a.shape[2:], lambda l, b: (l, b, 0, 0, 0))

    ins = (x, cos2, sin2, gam, h0, rgb0, sret0, gb0, sgdn0) + tuple(wts)
    in_specs = ([const(x), const(cos2), const(sin2), const(gam),
                 per_layer(h0), per_layer(rgb0), per_block(sret0), per_layer(gb0), per_block(sgdn0)]
                + [per_layer(a, prefetch=(i == 0)) for i, a in enumerate(wts)])
    out_shape = (jax.ShapeDtypeStruct(x.shape, F32),
                 jax.ShapeDtypeStruct(h0.shape, F32), jax.ShapeDtypeStruct(rgb0.shape, F32),
                 jax.ShapeDtypeStruct(sret0.shape, F32), jax.ShapeDtypeStruct(gb0.shape, F32),
                 jax.ShapeDtypeStruct(sgdn0.shape, F32))
    out_specs = (const(x), per_layer(h0), per_layer(rgb0), per_block(sret0), per_layer(gb0), per_block(sgdn0))
    scratch = [pltpu.VMEM((nb, D_MODEL), F32),
               pltpu.VMEM((nb, D_MAIN), F32),
               pltpu.VMEM((nb, LANES), F32),
               pltpu.VMEM((nb, LANES), F32),
               pltpu.VMEM((nb, 3 * W), F32)]
    kern = functools.partial(_sample_kernel, bb_rows=bb_rows, n_bb=n_bb)
    return pl.pallas_call(
        kern, grid=(DEPTH, n_bb), in_specs=in_specs, out_specs=out_specs, out_shape=out_shape,
        scratch_shapes=scratch,
        compiler_params=pltpu.CompilerParams(dimension_semantics=("arbitrary", "arbitrary"),
                                             vmem_limit_bytes=58 * 1024 * 1024),
        name="sample_path",
    )(*ins)


def kernel(x_prompt, x_sample, state_rglru_h, state_rglru_conv, state_ret, state_gdn_conv, state_gdn,
           meta_tokens, w_in, rg_conv_w, rg_conv_b, rg_w_a, rg_b_a, rg_w_x, rg_b_x, rg_lambda,
           ret_gn_w, ret_gn_b, gdn_conv_w, gdn_a_log, gdn_dt_bias, gdn_norm_w, w_out, ln_w, ln_b):
    bp, seq, _ = x_prompt.shape
    nb = x_sample.shape[0]
    wts = _prep_weights(w_in, rg_conv_w, rg_conv_b, rg_w_a, rg_b_a, rg_w_x, rg_b_x, rg_lambda,
                        ret_gn_w, ret_gn_b, gdn_conv_w, gdn_a_log, gdn_dt_bias, gdn_norm_w, w_out, ln_w, ln_b)

    pos = jnp.arange(N_META + seq, dtype=F32)
    zeros = lambda *s: jnp.zeros(s, F32)
    init0 = (zeros(1, 1, W), zeros(1, CONV_W - 1, W), zeros(1, HEADS, DK, DK),
             zeros(1, CONV_W - 1, 3 * W), zeros(1, HEADS, DK, DK))
    xm = meta_tokens.astype(x_prompt.dtype)[None]
    xp = x_prompt
    new_p = [[] for _ in range(5)]
    for l in range(DEPTH):
        xm, *st_m = _prompt_layer(l, xm, pos[:N_META], init0, wts, tb=N_META, c_ret=N_META, c_gdn=N_META)
        xp, *st_p = _prompt_layer(l, xp, pos[N_META:], tuple(st_m), wts, tb=512, c_ret=256, c_gdn=64)
        for j in range(5):
            new_p[j].append(st_p[j])
    sp = [jnp.stack(a) for a in new_p]
    sp[0] = sp[0].reshape(DEPTH, bp, W)

    ys, sh, srgb, sret, sgb, sgdn = _sample_path(
        x_sample.reshape(nb, D_MODEL), state_rglru_h,
        state_rglru_conv.reshape(DEPTH, nb, (CONV_W - 1) * W), state_ret,
        state_gdn_conv.reshape(DEPTH, nb, (CONV_W - 1) * 3 * W), state_gdn, wts, bb_rows=8)
    return (xp, ys.reshape(x_sample.shape), sp[0], sp[1], sp[2], sp[3], sp[4],
            sh, srgb.reshape(state_rglru_conv.shape), sret, sgb.reshape(state_gdn_conv.shape), sgdn)
```

```python
import functools
import math

import jax
import jax.numpy as jnp
from jax import lax
from jax.experimental import pallas as pl
from jax.experimental.pallas import tpu as pltpu

F32 = jnp.float32
BF16 = jnp.bfloat16

D_MODEL = 1024
DEPTH = 4
N_META = 16
PAST_LEN = 16384
W = 512
CONV_W = 4
RG_BLOCKS = 8
RG_C = 8.0
HEADS = 4
DK = W // HEADS
ROPE_BASE = 10000.0
EPS = 1e-6
ALPHA = (2.0 * DEPTH) ** 0.25
D_MAIN = 10 * W
LANES = 128
SUBLANES = 8
N_CONV = 4 * W

C_RGX, C_RGZ, C_RQ, C_RK, C_RV, C_RZ, C_GQ, C_GK, C_GV, C_GZ = (i * W for i in range(10))


def _bdot(a, b):
    return jnp.dot(a.astype(BF16), b.astype(BF16), preferred_element_type=F32)


def _bdot_nt(a, b):
    return lax.dot_general(a.astype(BF16), b.astype(BF16), (((1,), (1,)), ((), ())), preferred_element_type=F32)


def _bdot_tn(a, b):
    return lax.dot_general(a.astype(BF16), b.astype(BF16), (((0,), (0,)), ((), ())), preferred_element_type=F32)


def _bmm(a, b):
    return jnp.einsum('nij,njk->nik', a.astype(BF16), b.astype(BF16), preferred_element_type=F32)


def _bmm_nt(a, b):
    return jnp.einsum('nik,njk->nij', a.astype(BF16), b.astype(BF16), preferred_element_type=F32)


def _silu(x):
    return x * jax.nn.sigmoid(x)


def _scan_rows(a, b, h0):
    n, lanes = a.shape
    g = n // SUBLANES
    a3 = a.reshape(g, SUBLANES, lanes)
    b3 = b.reshape(g, SUBLANES, lanes)
    sub = lax.broadcasted_iota(jnp.int32, a3.shape, 1)
    s = 1
    while s < SUBLANES:
        m = sub >= s
        a_s = pltpu.roll(a3, s, axis=1)
        b_s = pltpu.roll(b3, s, axis=1)
        b3 = jnp.where(m, a3 * b_s + b3, b3)
        a3 = jnp.where(m, a3 * a_s, a3)
        s *= 2
    carry = h0
    hs = []
    for i in range(g):
        hi = a3[i] * carry + b3[i]
        carry = hi[SUBLANES - 1:SUBLANES, :]
        hs.append(hi)
    return jnp.concatenate(hs, axis=0), carry


def _causal_conv_strip(x, hist, taps):
    r8 = lax.broadcasted_iota(jnp.int32, hist.shape, 0)
    acc = taps[CONV_W - 1] * x
    for d in range(1, CONV_W):
        xs = pltpu.roll(x, d, axis=0)
        top = jnp.where(r8 < d, pltpu.roll(hist, d, axis=0), xs[0:SUBLANES])
        acc = acc + taps[CONV_W - 1 - d] * jnp.concatenate([top, xs[SUBLANES:]], axis=0)
    return acc


def _prompt_layer_kernel(
        x_ref, cos_ref, sin_ref,
        h0_ref, rgb0_ref, sret0_ref, gb0_ref, sgdn0_ref,
        win_ref, wab_ref, wg_ref, bg_ref, lam_ref, rcw_ref, rcb_ref, gnw_ref, gnb_ref,
        gcw_ref, alog_ref, dtb_ref, gnorm_ref, wout_ref, lnw_ref, lnb_ref,
        rdecay_ref, rqdec_ref, rkdec_ref, rsdec_ref,
        y_ref, h_ref, rgb_ref, sret_ref, gb_ref, sgdn_ref,
        u_ref, hist_ref, xc_ref, mix_ref,
        *, tb, c_ret, c_gdn):
    t = pl.program_id(1)

    @pl.when(t == 0)
    def _init():
        h_ref[...] = h0_ref[...]
        sret_ref[...] = sret0_ref[...]
        sgdn_ref[...] = sgdn0_ref[...]
        hist_ref[SUBLANES - 3:SUBLANES, 0:W] = rgb0_ref[...]
        hist_ref[SUBLANES - 3:SUBLANES, W:N_CONV] = gb0_ref[...]

    x = x_ref[...]
    xb = x.astype(BF16)

    def project(c0, c1):
        u_ref[:, c0:c1] = _bdot_nt(xb, win_ref[c0:c1, :])

    project(C_RGX, C_RQ)
    ab = _bdot_nt(xb, wab_ref[...])
    project(C_GQ, D_MAIN)

    for j in range(W // LANES):
        cs = slice(C_RGX + j * LANES, C_RGX + (j + 1) * LANES)
        ws = slice(j * LANES, (j + 1) * LANES)
        xj = u_ref[:, cs]
        taps = [rcw_ref[k:k + 1, ws] for k in range(CONV_W)]
        xc_ref[:, ws] = _causal_conv_strip(xj, hist_ref[:, ws], taps) + rcb_ref[:, ws]
        hist_ref[:, ws] = xj[tb - SUBLANES:tb, :]
        rgb_ref[:, ws] = xj[tb - 3:tb, :]
    gates = jnp.dot(xc_ref[...].astype(BF16), wg_ref[...], preferred_element_type=F32) + bg_ref[...]
    for j in range(W // LANES):
        zs = slice(C_RGZ + j * LANES, C_RGZ + (j + 1) * LANES)
        ws = slice(j * LANES, (j + 1) * LANES)
        r = jax.nn.sigmoid(gates[:, j * LANES:(j + 1) * LANES])
        i = jax.nn.sigmoid(gates[:, W + j * LANES:W + (j + 1) * LANES])
        log_a = (-RG_C) * r * jax.nn.softplus(-lam_ref[:, ws])
        a = jnp.exp(log_a)
        b = jnp.sqrt(1.0 - a * a) * (i * xc_ref[:, ws])
        h, h_last = _scan_rows(a, b, h_ref[:, ws])
        h_ref[:, ws] = h_last
        mix_ref[:, ws] = h * _silu(u_ref[:, zs])

    project(C_RQ, C_GQ)
    cosf = cos_ref[...]
    sinf = sin_ref[...]
    for hd in range(HEADS):
        q = u_ref[:, C_RQ + hd * DK:C_RQ + (hd + 1) * DK]
        k = u_ref[:, C_RK + hd * DK:C_RK + (hd + 1) * DK]
        v = u_ref[:, C_RV + hd * DK:C_RV + (hd + 1) * DK]
        z = u_ref[:, C_RZ + hd * DK:C_RZ + (hd + 1) * DK]
        q = q * cosf + pltpu.roll(q, DK // 2, axis=1) * sinf
        k = (k * cosf + pltpu.roll(k, DK // 2, axis=1) * sinf) * (DK ** -0.5)
        s = sret_ref[hd]
        outs = []
        for c in range(tb // c_ret):
            rs = slice(c * c_ret, (c + 1) * c_ret)
            qc, kc, vc = q[rs], k[rs], v[rs]
            sc = _bdot_nt(qc, kc) * rdecay_ref[hd]
            outs.append(_bdot(sc, vc) + _bdot(qc * rqdec_ref[hd], s))
            s = s * rsdec_ref[hd] + _bdot_tn(kc * rkdec_ref[hd], vc)
        sret_ref[hd] = s
        o = outs[0] if len(outs) == 1 else jnp.concatenate(outs, axis=0)
        mu = jnp.mean(o, axis=-1, keepdims=True)
        d = o - mu
        on = d * lax.rsqrt(jnp.mean(d * d, axis=-1, keepdims=True) + EPS)
        gs = slice(hd * DK, (hd + 1) * DK)
        mix_ref[:, W + hd * DK:W + (hd + 1) * DK] = (on * gnw_ref[:, gs] + gnb_ref[:, gs]) * _silu(z)

    for j in range(3 * W // LANES):
        cs = slice(C_GQ + j * LANES, C_GQ + (j + 1) * LANES)
        ws = slice(j * LANES, (j + 1) * LANES)
        hs = slice(W + j * LANES, W + (j + 1) * LANES)
        xj = u_ref[:, cs]
        taps = [gcw_ref[k:k + 1, ws] for k in range(CONV_W)]
        u_ref[:, cs] = _silu(_causal_conv_strip(xj, hist_ref[:, hs], taps))
        hist_ref[:, hs] = xj[tb - SUBLANES:tb, :]
        gb_ref[:, ws] = xj[tb - 3:tb, :]

    nc = tb // c_gdn
    glog = -jnp.exp(alog_ref[...]) * jax.nn.softplus(ab + dtb_ref[...])
    beta_all = jax.nn.sigmoid(ab)
    in_chunk = lax.broadcasted_iota(jnp.int32, (tb, LANES), 0) % c_gdn
    gcs = glog
    sh = 1
    while sh < c_gdn:
        gcs = gcs + jnp.where(in_chunk >= sh, pltpu.roll(gcs, sh, axis=0), 0.0)
        sh *= 2

    ri = lax.broadcasted_iota(jnp.int32, (c_gdn, c_gdn), 0)
    ci = lax.broadcasted_iota(jnp.int32, (c_gdn, c_gdn), 1)
    tril = ri >= ci
    strict = ri > ci
    eye_f = jnp.where(ri == ci, 1.0, 0.0).astype(F32)
    n_lv = int(math.ceil(math.log2(c_gdn)))

    names = ("q", "k", "kb", "vb", "kbe", "qe", "kd", "gl", "dec")
    parts = {n: [[None] * HEADS for _ in range(nc)] for n in names}
    for hd in range(HEADS):
        q = u_ref[:, C_GQ + hd * DK:C_GQ + (hd + 1) * DK]
        k = u_ref[:, C_GK + hd * DK:C_GK + (hd + 1) * DK]
        v = u_ref[:, C_GV + hd * DK:C_GV + (hd + 1) * DK]
        q = q * lax.rsqrt(jnp.sum(q * q, axis=-1, keepdims=True) + EPS) * (DK ** -0.5)
        k = k * lax.rsqrt(jnp.sum(k * k, axis=-1, keepdims=True) + EPS)
        gc = jnp.broadcast_to(gcs[:, hd:hd + 1], (tb, LANES))
        beta = jnp.broadcast_to(beta_all[:, HEADS + hd:HEADS + hd + 1], (tb, LANES))
        egc = jnp.exp(gc)
        kb = k * beta
        vb = v * beta
        kbe = kb * egc
        qe = q * egc
        for c in range(nc):
            rs = slice(c * c_gdn, (c + 1) * c_gdn)
            gcc = gc[rs]
            gl = gcc[c_gdn - 1:c_gdn, :]
            diff = gcc[:, :c_gdn] - gcc.T[:c_gdn, :]
            parts["dec"][c][hd] = jnp.where(tril, jnp.exp(jnp.where(tril, diff, 0.0)), 0.0)
            parts["gl"][c][hd] = gl
            parts["kd"][c][hd] = k[rs] * jnp.exp(gl - gcc)
            for n, val in (("q", q), ("k", k), ("kb", kb), ("vb", vb), ("kbe", kbe), ("qe", qe)):
                parts[n][c][hd] = val[rs]
    st = {n: jnp.stack([parts[n][c][hd] for c in range(nc) for hd in range(HEADS)]) for n in names}

    a_low = jnp.where(strict, _bmm_nt(st["kb"], st["k"]) * st["dec"], 0.0)
    att = _bmm_nt(st["q"], st["k"]) * st["dec"]
    pt = jnp.concatenate([-a_low, jnp.broadcast_to(eye_f, a_low.shape)], axis=-1)
    right = lax.broadcasted_iota(jnp.int32, (c_gdn, 2 * c_gdn), 1) >= c_gdn
    for _ in range(n_lv):
        ptb = pt.astype(BF16)
        r = jnp.einsum('nij,njk->nik', ptb[..., :c_gdn], ptb, preferred_element_type=F32)
        pt = r + jnp.where(right, pt, 0.0)
    tinv = pt[..., c_gdn:]
    sol = _bmm(tinv, jnp.concatenate([st["vb"], st["kbe"]], axis=-1))
    uu, ww = sol[..., :DK], sol[..., DK:]

    s = sgdn_ref[...]
    outs = []
    for c in range(nc):
        hs4 = slice(c * HEADS, (c + 1) * HEADS)
        ws_ = _bmm(jnp.concatenate([ww[hs4], st["qe"][hs4]], axis=1), s)
        v_new = uu[hs4] - ws_[:, :c_gdn]
        outs.append(ws_[:, c_gdn:] + _bmm(att[hs4], v_new))
        kv = jnp.einsum('hik,hiv->hkv', st["kd"][hs4].astype(BF16), v_new.astype(BF16),
                        preferred_element_type=F32)
        s = s * jnp.exp(st["gl"][hs4]) + kv
    sgdn_ref[...] = s
    for hd in range(HEADS):
        z = u_ref[:, C_GZ + hd * DK:C_GZ + (hd + 1) * DK]
        o = outs[0][hd] if nc == 1 else jnp.concatenate([outs[c][hd] for c in range(nc)], axis=0)
        on = o * lax.rsqrt(jnp.mean(o * o, axis=-1, keepdims=True) + EPS) * gnorm_ref[...]
        mix_ref[:, 2 * W + hd * DK:2 * W + (hd + 1) * DK] = on * _silu(z)

    out = jnp.dot(mix_ref[...].astype(BF16), wout_ref[...], preferred_element_type=F32)
    r = ALPHA * x + out
    mu = jnp.mean(r, axis=-1, keepdims=True)
    d = r - mu
    var = jnp.mean(d * d, axis=-1, keepdims=True)
    y_ref[...] = d * lax.rsqrt(var + EPS) * lnw_ref[...] + lnb_ref[...]


def _ret_tables(c):
    lg = jnp.log1p(-jnp.exp2(-5.0 - jnp.arange(HEADS, dtype=F32)))[:, None, None]
    idx = jnp.arange(c, dtype=F32)
    diff = idx[:, None] - idx[None, :]
    decay = jnp.where(diff >= 0, jnp.exp(lg * jnp.maximum(diff, 0.0)), 0.0)
    q_dec = jnp.broadcast_to(jnp.exp(lg[:, 0] * (idx + 1.0))[:, :, None], (HEADS, c, DK))
    k_dec = jnp.broadcast_to(jnp.exp(lg[:, 0] * (c - 1.0 - idx))[:, :, None], (HEADS, c, DK))
    s_dec = jnp.broadcast_to(jnp.exp(lg * c), (HEADS, 1, DK))
    return decay, q_dec, k_dec, s_dec


def _rope_tables(pos):
    half = DK // 2
    inv = ROPE_BASE ** (-jnp.arange(half, dtype=F32) / half)
    ang = pos[:, None] * inv[None, :]
    cos, sin = jnp.cos(ang), jnp.sin(ang)
    return jnp.concatenate([cos, cos], -1), jnp.concatenate([-sin, sin], -1)


def _prompt_layer(layer, x, pos, init, wts, *, tb, c_ret, c_gdn):
    bsz, tlen, _ = x.shape
    assert tlen % tb == 0 and tb % c_ret == 0 and tb % c_gdn == 0 and tb % SUBLANES == 0
    nt = tlen // tb
    cos2, sin2 = _rope_tables(pos)
    rtabs = _ret_tables(c_ret)

    def wspec(a):
        nd = a.ndim - 1
        return pl.BlockSpec((None,) + a.shape[1:], lambda b, t, _n=nd: (layer,) + (0,) * _n,
                            pipeline_mode=pl.Buffered(1))

    def cspec(a):
        nd = a.ndim
        return pl.BlockSpec(a.shape, lambda b, t, _n=nd: (0,) * _n, pipeline_mode=pl.Buffered(1))

    def ispec(a):
        nd = a.ndim - 1
        return pl.BlockSpec((None,) + a.shape[1:], lambda b, t, _n=nd: (0,) * (_n + 1),
                            pipeline_mode=pl.Buffered(1))

    def ospec(shape):
        nd = len(shape)
        return pl.BlockSpec((None,) + shape, lambda b, t, _n=nd: (b,) + (0,) * _n)

    in_specs = ([pl.BlockSpec((None, tb, D_MODEL), lambda b, t: (b, t, 0)),
                 pl.BlockSpec((tb, DK), lambda b, t: (t, 0)),
                 pl.BlockSpec((tb, DK), lambda b, t: (t, 0))]
                + [ispec(a) for a in init] + [wspec(a) for a in wts] + [cspec(a) for a in rtabs])
    out_shape = (jax.ShapeDtypeStruct((bsz, tlen, D_MODEL), F32),
                 jax.ShapeDtypeStruct((bsz, 1, W), F32),
                 jax.ShapeDtypeStruct((bsz, CONV_W - 1, W), F32),
                 jax.ShapeDtypeStruct((bsz, HEADS, DK, DK), F32),
                 jax.ShapeDtypeStruct((bsz, CONV_W - 1, 3 * W), F32),
                 jax.ShapeDtypeStruct((bsz, HEADS, DK, DK), F32))
    out_specs = (pl.BlockSpec((None, tb, D_MODEL), lambda b, t: (b, t, 0)),
                 ospec((1, W)), ospec((CONV_W - 1, W)), ospec((HEADS, DK, DK)),
                 ospec((CONV_W - 1, 3 * W)), ospec((HEADS, DK, DK)))
    scratch = [pltpu.VMEM((tb, D_MAIN), F32),
               pltpu.VMEM((SUBLANES, N_CONV), F32),
               pltpu.VMEM((tb, W), F32),
               pltpu.VMEM((tb, 3 * W), F32)]
    kern = functools.partial(_prompt_layer_kernel, tb=tb, c_ret=c_ret, c_gdn=c_gdn)
    return pl.pallas_call(
        kern, grid=(bsz, nt), in_specs=in_specs, out_specs=out_specs, out_shape=out_shape,
        scratch_shapes=scratch,
        compiler_params=pltpu.CompilerParams(dimension_semantics=("arbitrary", "arbitrary"),
                                             vmem_limit_bytes=56 * 1024 * 1024),
        name=f"prompt_layer{layer}_t{tlen}",
    )(x, cos2, sin2, *init, *wts, *rtabs)


def _cast_kernel(x_ref, o_ref):
    o_ref[...] = x_ref[...].astype(o_ref.dtype)


def _cast_main_rows(w_t):
    depth, _, d_model = w_t.shape
    rows = 4 * LANES
    return pl.pallas_call(
        _cast_kernel, grid=(depth, D_MAIN // rows),
        in_specs=[pl.BlockSpec((None, rows, d_model), lambda l, j: (l, j, 0))],
        out_specs=pl.BlockSpec((None, rows, d_model), lambda l, j: (l, j, 0)),
        out_shape=jax.ShapeDtypeStruct((depth, D_MAIN, d_model), BF16),
        name="cast_w_in",
    )(w_t)


def _prep_weights(w_in, rg_conv_w, rg_conv_b, rg_w_a, rg_b_a, rg_w_x, rg_b_x, rg_lambda,
                  ret_gn_w, ret_gn_b, gdn_conv_w, gdn_a_log, gdn_dt_bias, gdn_norm_w, w_out, ln_w, ln_b):
    eye = jnp.eye(RG_BLOCKS, dtype=F32)

    def bdiag(w):
        l, n, c, d = w.shape
        return jnp.einsum('lncd,nm->lncmd', w.astype(F32), eye).reshape(l, n * c, n * d)

    pad = LANES - 2 * HEADS
    w_t = jnp.transpose(w_in, (0, 2, 1))
    w_main = _cast_main_rows(w_t)
    w_ab = jnp.pad(w_t[:, D_MAIN:, :], ((0, 0), (0, pad), (0, 0))).astype(BF16)
    wg = jnp.concatenate([bdiag(rg_w_a), bdiag(rg_w_x)], axis=-1).astype(BF16)
    bg = jnp.concatenate([rg_b_a, rg_b_x], axis=-1)[:, None, :].astype(F32)
    row = lambda a: a[:, None, :].astype(F32)
    padh = lambda a: jnp.pad(a.astype(F32), ((0, 0), (0, LANES - HEADS)))[:, None, :]
    return (w_main, w_ab, wg, bg, row(rg_lambda), rg_conv_w.astype(F32), row(rg_conv_b),
            row(ret_gn_w), row(ret_gn_b), gdn_conv_w.astype(F32), padh(gdn_a_log), padh(gdn_dt_bias),
            row(gdn_norm_w), w_out.astype(BF16), row(ln_w), row(ln_b))


def _sample_kernel(
        x_ref, cos_ref, sin_ref, gam_ref,
        h0_ref, rgb0_ref, sret0_ref, gb0_ref, sgdn0_ref,
        win_ref, wab_ref, wg_ref, bg_ref, lam_ref, rcw_ref, rcb_ref, gnw_ref, gnb_ref,
        gcw_ref, alog_ref, dtb_ref, gnorm_ref, wout_ref, lnw_ref, lnb_ref,
        y_ref, h_ref, rgb_ref, sret_ref, gb_ref, sgdn_ref,
        xcur_ref, u_ref, eg_ref, beta_ref, mix_ref,
        *, bb_rows, n_bb):
    layer = pl.program_id(0)
    bb = pl.program_id(1)

    @pl.when(jnp.logical_and(layer == 0, bb == 0))
    def _load_x():
        xcur_ref[...] = x_ref[...]

    @pl.when(bb == 0)
    def _project():
        xb = xcur_ref[...].astype(BF16)
        u_ref[...] = _bdot_nt(xb, win_ref[...])
        ab = _bdot_nt(xb, wab_ref[...])
        eg_ref[...] = jnp.exp(-jnp.exp(alog_ref[...]) * jax.nn.softplus(ab + dtb_ref[...]))
        beta_ref[...] = jax.nn.sigmoid(ab)

        cur = u_ref[:, C_RGX:C_RGX + W]
        xc = rcb_ref[...] + rcw_ref[CONV_W - 1:CONV_W, :] * cur
        for k in range(CONV_W - 1):
            xc = xc + rcw_ref[k:k + 1, :] * rgb0_ref[k]
        rgb_ref[0] = rgb0_ref[1]
        rgb_ref[1] = rgb0_ref[2]
        rgb_ref[2] = cur
        gates = jnp.dot(xc.astype(BF16), wg_ref[...], preferred_element_type=F32) + bg_ref[...]
        r = jax.nn.sigmoid(gates[:, :W])
        i = jax.nn.sigmoid(gates[:, W:])
        log_a = (-RG_C) * r * jax.nn.softplus(-lam_ref[...])
        a = jnp.exp(log_a)
        h = a * h0_ref[...] + jnp.sqrt(1.0 - a * a) * (i * xc)
        h_ref[...] = h
        mix_ref[:, 0:W] = h * _silu(u_ref[:, C_RGZ:C_RGZ + W])

        cosf = cos_ref[...]
        sinf = sin_ref[...]
        for hd in range(HEADS):
            qs = slice(C_RQ + hd * DK, C_RQ + (hd + 1) * DK)
            ks = slice(C_RK + hd * DK, C_RK + (hd + 1) * DK)
            q = u_ref[:, qs]
            k = u_ref[:, ks]
            u_ref[:, qs] = q * cosf + pltpu.roll(q, DK // 2, axis=1) * sinf
            u_ref[:, ks] = (k * cosf + pltpu.roll(k, DK // 2, axis=1) * sinf) * (DK ** -0.5)

        for j in range(3 * W // LANES):
            cs = slice(C_GQ + j * LANES, C_GQ + (j + 1) * LANES)
            ws = slice(j * LANES, (j + 1) * LANES)
            cur = u_ref[:, cs]
            acc = gcw_ref[CONV_W - 1:CONV_W, ws] * cur
            for k in range(CONV_W - 1):
                acc = acc + gcw_ref[k:k + 1, ws] * gb0_ref[k, :, ws]
            gb_ref[0, :, ws] = gb0_ref[1, :, ws]
            gb_ref[1, :, ws] = gb0_ref[2, :, ws]
            gb_ref[2, :, ws] = cur
            y = _silu(acc)
            if j < 2 * HEADS:
                y = y * lax.rsqrt(jnp.sum(y * y, axis=-1, keepdims=True) + EPS)
                if j < HEADS:
                    y = y * (DK ** -0.5)
            u_ref[:, cs] = y

    r0 = pl.multiple_of(bb * bb_rows, SUBLANES)
    rows = pl.ds(r0, bb_rows)
    egb = eg_ref[rows, :]
    btb = beta_ref[rows, :]
    rid = lax.broadcasted_iota(jnp.int32, (bb_rows, DK), 0)
    rid2 = lax.broadcasted_iota(jnp.int32, (2 * bb_rows, DK), 0)

    def own_rows(x):
        return jnp.concatenate([jnp.where(rid == i, x, 0.0) for i in range(bb_rows)], axis=1)

    for hd in range(HEADS):
        q = u_ref[rows, C_RQ + hd * DK:C_RQ + (hd + 1) * DK]
        k = u_ref[rows, C_RK + hd * DK:C_RK + (hd + 1) * DK]
        v = u_ref[rows, C_RV + hd * DK:C_RV + (hd + 1) * DK]
        z = u_ref[rows, C_RZ + hd * DK:C_RZ + (hd + 1) * DK]
        qk = jnp.sum(q * k, axis=-1, keepdims=True)
        gam = gam_ref[hd]
        qs_ = jnp.zeros((bb_rows, DK), F32)
        for i in range(bb_rows):
            qs_ = jnp.where(rid == i, _bdot(q, sret0_ref[i, hd]), qs_)
        o = qk * v + gam * qs_
        kv = _bdot_tn(k, own_rows(v))
        for i in range(bb_rows):
            sret_ref[i, hd] = gam * sret0_ref[i, hd] + kv[:, i * DK:(i + 1) * DK]
        mu = jnp.mean(o, axis=-1, keepdims=True)
        d = o - mu
        on = d * lax.rsqrt(jnp.mean(d * d, axis=-1, keepdims=True) + EPS)
        gs = slice(hd * DK, (hd + 1) * DK)
        mix_ref[rows, W + hd * DK:W + (hd + 1) * DK] = (on * gnw_ref[:, gs] + gnb_ref[:, gs]) * _silu(z)

        q = u_ref[rows, C_GQ + hd * DK:C_GQ + (hd + 1) * DK]
        k = u_ref[rows, C_GK + hd * DK:C_GK + (hd + 1) * DK]
        v = u_ref[rows, C_GV + hd * DK:C_GV + (hd + 1) * DK]
        z = u_ref[rows, C_GZ + hd * DK:C_GZ + (hd + 1) * DK]
        qk = jnp.sum(q * k, axis=-1, keepdims=True)
        eg = jnp.broadcast_to(egb[:, hd:hd + 1], (bb_rows, DK))
        beta = jnp.broadcast_to(btb[:, HEADS + hd:HEADS + hd + 1], (bb_rows, DK))
        kq = jnp.concatenate([k, q], axis=0)
        kqs = jnp.zeros((2 * bb_rows, DK), F32)
        for i in range(bb_rows):
            kqs = jnp.where(rid2 % bb_rows == i, _bdot(kq, sgdn0_ref[i, hd]), kqs)
        v_new = beta * (v - eg * kqs[:bb_rows])
        o = eg * kqs[bb_rows:] + qk * v_new
        kv = _bdot_tn(k, own_rows(v_new))
        for i in range(bb_rows):
            sgdn_ref[i, hd] = sgdn0_ref[i, hd] * eg[i:i + 1, :] + kv[:, i * DK:(i + 1) * DK]
        on = o * lax.rsqrt(jnp.mean(o * o, axis=-1, keepdims=True) + EPS) * gnorm_ref[...]
        mix_ref[rows, 2 * W + hd * DK:2 * W + (hd + 1) * DK] = on * _silu(z)

    @pl.when(bb == n_bb - 1)
    def _finish():
        out = jnp.dot(mix_ref[...].astype(BF16), wout_ref[...], preferred_element_type=F32)
        r = ALPHA * xcur_ref[...] + out
        mu = jnp.mean(r, axis=-1, keepdims=True)
        d = r - mu
        var = jnp.mean(d * d, axis=-1, keepdims=True)
        y = d * lax.rsqrt(var + EPS) * lnw_ref[...] + lnb_ref[...]
        xcur_ref[...] = y
        y_ref[...] = y


def _sample_path(x, h0, rgb0, sret0, gb0, sgdn0, wts, *, bb_rows):
    nb = x.shape[0]
    n_bb = nb // bb_rows
    pos = jnp.arange(1, dtype=F32) + float(PAST_LEN)
    cos2, sin2 = _rope_tables(pos)
    lg = jnp.log1p(-jnp.exp2(-5.0 - jnp.arange(HEADS, dtype=F32)))
    gam = jnp.broadcast_to(jnp.exp(lg)[:, None, None], (HEADS, 1, DK))

    def const(a):
        nd = a.ndim
        return pl.BlockSpec(a.shape, lambda l, b, _n=nd: (0,) * _n)

    def per_layer(a, prefetch=False):
        nd = a.ndim - 1
        return pl.BlockSpec((None,) + a.shape[1:], lambda l, b, _n=nd: (l,) + (0,) * _n,
                            pipeline_mode=pl.Buffered(2 if prefetch else 1))

    def per_block(a):
        return pl.BlockSpec((None, bb_rows) + a.shape[2:], lambda l, b: (l, b, 0, 0, 0))

    ins = (x, cos2, sin2, gam, h0, rgb0, sret0, gb0, sgdn0) + tuple(wts)
    in_specs = ([const(x), const(cos2), const(sin2), const(gam),
                 per_layer(h0), per_layer(rgb0), per_block(sret0), per_layer(gb0), per_block(sgdn0)]
                + [per_layer(a, prefetch=(i == 0)) for i, a in enumerate(wts)])
    out_shape = (jax.ShapeDtypeStruct(x.shape, F32),
                 jax.ShapeDtypeStruct(h0.shape, F32), jax.ShapeDtypeStruct(rgb0.shape, F32),
                 jax.ShapeDtypeStruct(sret0.shape, F32), jax.ShapeDtypeStruct(gb0.shape, F32),
                 jax.ShapeDtypeStruct(sgdn0.shape, F32))
    out_specs = (const(x), per_layer(h0), per_layer(rgb0), per_block(sret0), per_layer(gb0), per_block(sgdn0))
    scratch = [pltpu.VMEM((nb, D_MODEL), F32),
               pltpu.VMEM((nb, D_MAIN), F32),
               pltpu.VMEM((nb, LANES), F32),
               pltpu.VMEM((nb, LANES), F32),
               pltpu.VMEM((nb, 3 * W), F32)]
    kern = functools.partial(_sample_kernel, bb_rows=bb_rows, n_bb=n_bb)
    return pl.pallas_call(
        kern, grid=(DEPTH, n_bb), in_specs=in_specs, out_specs=out_specs, out_shape=out_shape,
        scratch_shapes=scratch,
        compiler_params=pltpu.CompilerParams(dimension_semantics=("arbitrary", "arbitrary"),
                                             vmem_limit_bytes=58 * 1024 * 1024),
        name="sample_path",
    )(*ins)


def kernel(x_prompt, x_sample, state_rglru_h, state_rglru_conv, state_ret, state_gdn_conv, state_gdn,
           meta_tokens, w_in, rg_conv_w, rg_conv_b, rg_w_a, rg_b_a, rg_w_x, rg_b_x, rg_lambda,
           ret_gn_w, ret_gn_b, gdn_conv_w, gdn_a_log, gdn_dt_bias, gdn_norm_w, w_out, ln_w, ln_b):
    bp, seq, _ = x_prompt.shape
    nb = x_sample.shape[0]
    wts = _prep_weights(w_in, rg_conv_w, rg_conv_b, rg_w_a, rg_b_a, rg_w_x, rg_b_x, rg_lambda,
                        ret_gn_w, ret_gn_b, gdn_conv_w, gdn_a_log, gdn_dt_bias, gdn_norm_w, w_out, ln_w, ln_b)

    pos = jnp.arange(N_META + seq, dtype=F32)
    zeros = lambda *s: jnp.zeros(s, F32)
    init0 = (zeros(1, 1, W), zeros(1, CONV_W - 1, W), zeros(1, HEADS, DK, DK),
             zeros(1, CONV_W - 1, 3 * W), zeros(1, HEADS, DK, DK))
    xm = meta_tokens.astype(x_prompt.dtype)[None]
    xp = x_prompt
    new_p = [[] for _ in range(5)]
    for l in range(DEPTH):
        xm, *st_m = _prompt_layer(l, xm, pos[:N_META], init0, wts, tb=N_META, c_ret=N_META, c_gdn=N_META)
        xp, *st_p = _prompt_layer(l, xp, pos[N_META:], tuple(st_m), wts, tb=512, c_ret=256, c_gdn=64)
        for j in range(5):
            new_p[j].append(st_p[j])
    sp = [jnp.stack(a) for a in new_p]
    sp[0] = sp[0].reshape(DEPTH, bp, W)

    tap_major = lambda a: jnp.transpose(a, (0, 2, 1, 3))
    ys, sh, srgb, sret, sgb, sgdn = _sample_path(
        x_sample.reshape(nb, D_MODEL), state_rglru_h, tap_major(state_rglru_conv), state_ret,
        tap_major(state_gdn_conv), state_gdn, wts, bb_rows=8)
    return (xp, ys.reshape(x_sample.shape), sp[0], sp[1], sp[2], sp[3], sp[4],
            sh, tap_major(srgb), sret, tap_major(sgb), sgdn)
```

```python
import functools
import math

import jax
import jax.numpy as jnp
from jax import lax
from jax.experimental import pallas as pl
from jax.experimental.pallas import tpu as pltpu

F32 = jnp.float32
BF16 = jnp.bfloat16

D_MODEL = 1024
DEPTH = 4
N_META = 16
PAST_LEN = 16384
W = 512
CONV_W = 4
RG_BLOCKS = 8
RG_C = 8.0
HEADS = 4
DK = W // HEADS
ROPE_BASE = 10000.0
EPS = 1e-6
ALPHA = (2.0 * DEPTH) ** 0.25
D_MAIN = 10 * W
LANES = 128
SUBLANES = 8
N_CONV = 4 * W

C_RGX, C_RGZ, C_RQ, C_RK, C_RV, C_RZ, C_GQ, C_GK, C_GV, C_GZ = (i * W for i in range(10))


def _bdot(a, b):
    return jnp.dot(a.astype(BF16), b.astype(BF16), preferred_element_type=F32)


def _bdot_nt(a, b):
    return lax.dot_general(a.astype(BF16), b.astype(BF16), (((1,), (1,)), ((), ())), preferred_element_type=F32)


def _bdot_tn(a, b):
    return lax.dot_general(a.astype(BF16), b.astype(BF16), (((0,), (0,)), ((), ())), preferred_element_type=F32)


def _bmm(a, b):
    return jnp.einsum('nij,njk->nik', a.astype(BF16), b.astype(BF16), preferred_element_type=F32)


def _bmm_nt(a, b):
    return jnp.einsum('nik,njk->nij', a.astype(BF16), b.astype(BF16), preferred_element_type=F32)


def _silu(x):
    return x * jax.nn.sigmoid(x)


def _scan_rows(a, b, h0):
    n, lanes = a.shape
    g = n // SUBLANES
    a3 = a.reshape(g, SUBLANES, lanes)
    b3 = b.reshape(g, SUBLANES, lanes)
    sub = lax.broadcasted_iota(jnp.int32, a3.shape, 1)
    s = 1
    while s < SUBLANES:
        m = sub >= s
        a_s = pltpu.roll(a3, s, axis=1)
        b_s = pltpu.roll(b3, s, axis=1)
        b3 = jnp.where(m, a3 * b_s + b3, b3)
        a3 = jnp.where(m, a3 * a_s, a3)
        s *= 2
    carry = h0
    hs = []
    for i in range(g):
        hi = a3[i] * carry + b3[i]
        carry = hi[SUBLANES - 1:SUBLANES, :]
        hs.append(hi)
    return jnp.concatenate(hs, axis=0), carry


def _causal_conv_strip(x, hist, taps):
    r8 = lax.broadcasted_iota(jnp.int32, hist.shape, 0)
    acc = taps[CONV_W - 1] * x
    for d in range(1, CONV_W):
        xs = pltpu.roll(x, d, axis=0)
        top = jnp.where(r8 < d, pltpu.roll(hist, d, axis=0), xs[0:SUBLANES])
        acc = acc + taps[CONV_W - 1 - d] * jnp.concatenate([top, xs[SUBLANES:]], axis=0)
    return acc


def _prompt_layer_kernel(
        x_ref, cos_ref, sin_ref,
        h0_ref, rgb0_ref, sret0_ref, gb0_ref, sgdn0_ref,
        win_ref, wab_ref, wg_ref, bg_ref, lam_ref, rcw_ref, rcb_ref, gnw_ref, gnb_ref,
        gcw_ref, alog_ref, dtb_ref, gnorm_ref, wout_ref, lnw_ref, lnb_ref,
        rdecay_ref, rqdec_ref, rkdec_ref, rsdec_ref,
        y_ref, h_ref, rgb_ref, sret_ref, gb_ref, sgdn_ref,
        u_ref, hist_ref, xc_ref, mix_ref,
        *, tb, c_ret, c_gdn):
    t = pl.program_id(1)

    @pl.when(t == 0)
    def _init():
        h_ref[...] = h0_ref[...]
        sret_ref[...] = sret0_ref[...]
        sgdn_ref[...] = sgdn0_ref[...]
        hist_ref[SUBLANES - 3:SUBLANES, 0:W] = rgb0_ref[...]
        hist_ref[SUBLANES - 3:SUBLANES, W:N_CONV] = gb0_ref[...]

    x = x_ref[...]
    xb = x.astype(BF16)

    def project(c0, c1):
        u_ref[:, c0:c1] = _bdot_nt(xb, win_ref[c0:c1, :])

    project(C_RGX, C_RQ)
    ab = _bdot_nt(xb, wab_ref[...])
    project(C_GQ, D_MAIN)

    for j in range(W // LANES):
        cs = slice(C_RGX + j * LANES, C_RGX + (j + 1) * LANES)
        ws = slice(j * LANES, (j + 1) * LANES)
        xj = u_ref[:, cs]
        taps = [rcw_ref[k:k + 1, ws] for k in range(CONV_W)]
        xc_ref[:, ws] = _causal_conv_strip(xj, hist_ref[:, ws], taps) + rcb_ref[:, ws]
        hist_ref[:, ws] = xj[tb - SUBLANES:tb, :]
        rgb_ref[:, ws] = xj[tb - 3:tb, :]
    gates = jnp.dot(xc_ref[...].astype(BF16), wg_ref[...], preferred_element_type=F32) + bg_ref[...]
    for j in range(W // LANES):
        zs = slice(C_RGZ + j * LANES, C_RGZ + (j + 1) * LANES)
        ws = slice(j * LANES, (j + 1) * LANES)
        r = jax.nn.sigmoid(gates[:, j * LANES:(j + 1) * LANES])
        i = jax.nn.sigmoid(gates[:, W + j * LANES:W + (j + 1) * LANES])
        log_a = (-RG_C) * r * jax.nn.softplus(-lam_ref[:, ws])
        a = jnp.exp(log_a)
        b = jnp.sqrt(1.0 - a * a) * (i * xc_ref[:, ws])
        h, h_last = _scan_rows(a, b, h_ref[:, ws])
        h_ref[:, ws] = h_last
        mix_ref[:, ws] = h * _silu(u_ref[:, zs])

    project(C_RQ, C_GQ)
    cosf = cos_ref[...]
    sinf = sin_ref[...]
    for hd in range(HEADS):
        q = u_ref[:, C_RQ + hd * DK:C_RQ + (hd + 1) * DK]
        k = u_ref[:, C_RK + hd * DK:C_RK + (hd + 1) * DK]
        v = u_ref[:, C_RV + hd * DK:C_RV + (hd + 1) * DK]
        z = u_ref[:, C_RZ + hd * DK:C_RZ + (hd + 1) * DK]
        q = q * cosf + pltpu.roll(q, DK // 2, axis=1) * sinf
        k = (k * cosf + pltpu.roll(k, DK // 2, axis=1) * sinf) * (DK ** -0.5)
        s = sret_ref[hd]
        outs = []
        for c in range(tb // c_ret):
            rs = slice(c * c_ret, (c + 1) * c_ret)
            qc, kc, vc = q[rs], k[rs], v[rs]
            sc = _bdot_nt(qc, kc) * rdecay_ref[hd]
            outs.append(_bdot(sc, vc) + _bdot(qc * rqdec_ref[hd], s))
            s = s * rsdec_ref[hd] + _bdot_tn(kc * rkdec_ref[hd], vc)
        sret_ref[hd] = s
        o = outs[0] if len(outs) == 1 else jnp.concatenate(outs, axis=0)
        mu = jnp.mean(o, axis=-1, keepdims=True)
        d = o - mu
        on = d * lax.rsqrt(jnp.mean(d * d, axis=-1, keepdims=True) + EPS)
        gs = slice(hd * DK, (hd + 1) * DK)
        mix_ref[:, W + hd * DK:W + (hd + 1) * DK] = (on * gnw_ref[:, gs] + gnb_ref[:, gs]) * _silu(z)

    for j in range(3 * W // LANES):
        cs = slice(C_GQ + j * LANES, C_GQ + (j + 1) * LANES)
        ws = slice(j * LANES, (j + 1) * LANES)
        hs = slice(W + j * LANES, W + (j + 1) * LANES)
        xj = u_ref[:, cs]
        taps = [gcw_ref[k:k + 1, ws] for k in range(CONV_W)]
        u_ref[:, cs] = _silu(_causal_conv_strip(xj, hist_ref[:, hs], taps))
        hist_ref[:, hs] = xj[tb - SUBLANES:tb, :]
        gb_ref[:, ws] = xj[tb - 3:tb, :]

    nc = tb // c_gdn
    glog = -jnp.exp(alog_ref[...]) * jax.nn.softplus(ab + dtb_ref[...])
    beta_all = jax.nn.sigmoid(ab)
    in_chunk = lax.broadcasted_iota(jnp.int32, (tb, LANES), 0) % c_gdn
    gcs = glog
    sh = 1
    while sh < c_gdn:
        gcs = gcs + jnp.where(in_chunk >= sh, pltpu.roll(gcs, sh, axis=0), 0.0)
        sh *= 2

    ri = lax.broadcasted_iota(jnp.int32, (c_gdn, c_gdn), 0)
    ci = lax.broadcasted_iota(jnp.int32, (c_gdn, c_gdn), 1)
    tril = ri >= ci
    strict = ri > ci
    eye_f = jnp.where(ri == ci, 1.0, 0.0).astype(F32)
    n_lv = int(math.ceil(math.log2(c_gdn)))

    names = ("q", "k", "kb", "vb", "kbe", "qe", "kd", "gl", "dec")
    parts = {n: [[None] * HEADS for _ in range(nc)] for n in names}
    for hd in range(HEADS):
        q = u_ref[:, C_GQ + hd * DK:C_GQ + (hd + 1) * DK]
        k = u_ref[:, C_GK + hd * DK:C_GK + (hd + 1) * DK]
        v = u_ref[:, C_GV + hd * DK:C_GV + (hd + 1) * DK]
        q = q * lax.rsqrt(jnp.sum(q * q, axis=-1, keepdims=True) + EPS) * (DK ** -0.5)
        k = k * lax.rsqrt(jnp.sum(k * k, axis=-1, keepdims=True) + EPS)
        gc = jnp.broadcast_to(gcs[:, hd:hd + 1], (tb, LANES))
        beta = jnp.broadcast_to(beta_all[:, HEADS + hd:HEADS + hd + 1], (tb, LANES))
        egc = jnp.exp(gc)
        kb = k * beta
        vb = v * beta
        kbe = kb * egc
        qe = q * egc
        for c in range(nc):
            rs = slice(c * c_gdn, (c + 1) * c_gdn)
            gcc = gc[rs]
            gl = gcc[c_gdn - 1:c_gdn, :]
            diff = gcc[:, :c_gdn] - gcc.T[:c_gdn, :]
            parts["dec"][c][hd] = jnp.where(tril, jnp.exp(jnp.where(tril, diff, 0.0)), 0.0)
            parts["gl"][c][hd] = gl
            parts["kd"][c][hd] = k[rs] * jnp.exp(gl - gcc)
            for n, val in (("q", q), ("k", k), ("kb", kb), ("vb", vb), ("kbe", kbe), ("qe", qe)):
                parts[n][c][hd] = val[rs]
    st = {n: jnp.stack([parts[n][c][hd] for c in range(nc) for hd in range(HEADS)]) for n in names}

    a_low = jnp.where(strict, _bmm_nt(st["kb"], st["k"]) * st["dec"], 0.0)
    att = _bmm_nt(st["q"], st["k"]) * st["dec"]
    pt = jnp.concatenate([-a_low, jnp.broadcast_to(eye_f, a_low.shape)], axis=-1)
    right = lax.broadcasted_iota(jnp.int32, (c_gdn, 2 * c_gdn), 1) >= c_gdn
    for _ in range(n_lv):
        ptb = pt.astype(BF16)
        r = jnp.einsum('nij,njk->nik', ptb[..., :c_gdn], ptb, preferred_element_type=F32)
        pt = r + jnp.where(right, pt, 0.0)
    tinv = pt[..., c_gdn:]
    sol = _bmm(tinv, jnp.concatenate([st["vb"], st["kbe"]], axis=-1))
    uu, ww = sol[..., :DK], sol[..., DK:]

    s = sgdn_ref[...]
    outs = []
    for c in range(nc):
        hs4 = slice(c * HEADS, (c + 1) * HEADS)
        ws_ = _bmm(jnp.concatenate([ww[hs4], st["qe"][hs4]], axis=1), s)
        v_new = uu[hs4] - ws_[:, :c_gdn]
        outs.append(ws_[:, c_gdn:] + _bmm(att[hs4], v_new))
        kv = jnp.einsum('hik,hiv->hkv', st["kd"][hs4].astype(BF16), v_new.astype(BF16),
                        preferred_element_type=F32)
        s = s * jnp.exp(st["gl"][hs4]) + kv
    sgdn_ref[...] = s
    for hd in range(HEADS):
        z = u_ref[:, C_GZ + hd * DK:C_GZ + (hd + 1) * DK]
        o = outs[0][hd] if nc == 1 else jnp.concatenate([outs[c][hd] for c in range(nc)], axis=0)
        on = o * lax.rsqrt(jnp.mean(o * o, axis=-1, keepdims=True) + EPS) * gnorm_ref[...]
        mix_ref[:, 2 * W + hd * DK:2 * W + (hd + 1) * DK] = on * _silu(z)

    out = jnp.dot(mix_ref[...].astype(BF16), wout_ref[...], preferred_element_type=F32)
    r = ALPHA * x + out
    mu = jnp.mean(r, axis=-1, keepdims=True)
    d = r - mu
    var = jnp.mean(d * d, axis=-1, keepdims=True)
    y_ref[...] = d * lax.rsqrt(var + EPS) * lnw_ref[...] + lnb_ref[...]


def _ret_tables(c):
    lg = jnp.log1p(-jnp.exp2(-5.0 - jnp.arange(HEADS, dtype=F32)))[:, None, None]
    idx = jnp.arange(c, dtype=F32)
    diff = idx[:, None] - idx[None, :]
    decay = jnp.where(diff >= 0, jnp.exp(lg * jnp.maximum(diff, 0.0)), 0.0)
    q_dec = jnp.broadcast_to(jnp.exp(lg[:, 0] * (idx + 1.0))[:, :, None], (HEADS, c, DK))
    k_dec = jnp.broadcast_to(jnp.exp(lg[:, 0] * (c - 1.0 - idx))[:, :, None], (HEADS, c, DK))
    s_dec = jnp.broadcast_to(jnp.exp(lg * c), (HEADS, 1, DK))
    return decay, q_dec, k_dec, s_dec


def _rope_tables(pos):
    half = DK // 2
    inv = ROPE_BASE ** (-jnp.arange(half, dtype=F32) / half)
    ang = pos[:, None] * inv[None, :]
    cos, sin = jnp.cos(ang), jnp.sin(ang)
    return jnp.concatenate([cos, cos], -1), jnp.concatenate([-sin, sin], -1)


N_PROMPT_IN, N_PROMPT_OUT, N_PROMPT_SCRATCH = 28, 6, 4
N_SAMPLE_IN, N_SAMPLE_OUT, N_SAMPLE_SCRATCH = 9, 6, 4


def _fused_layer_kernel(*refs, tb, c_ret, c_gdn, nt, n_alias):
    it = iter(refs)
    take = lambda n: [next(it) for _ in range(n)]
    p_in, s_in = take(N_PROMPT_IN), take(N_SAMPLE_IN)
    take(n_alias)
    p_out, s_out = take(N_PROMPT_OUT), take(N_SAMPLE_OUT)
    p_scr, s_scr = take(N_PROMPT_SCRATCH), take(N_SAMPLE_SCRATCH)
    (win_ref, wab_ref, wg_ref, bg_ref, lam_ref, rcw_ref, rcb_ref, gnw_ref, gnb_ref,
     gcw_ref, alog_ref, dtb_ref, gnorm_ref, wout_ref, lnw_ref, lnb_ref) = p_in[8:24]
    xs_ref, scos_ref, ssin_ref, gam_ref, sh0_ref, srgb0_ref, sret0_ref, sgb0_ref, sgdn0_ref = s_in
    ys_ref, sh_ref, srgb_ref, sret_ref, sgb_ref, sgdn_ref = s_out
    su_ref, seg_ref, sbeta_ref, smix_ref = s_scr

    step = pl.program_id(0) * nt + pl.program_id(1)
    last = pl.num_programs(0) * nt - 1
    per_group = SUBLANES // SAMPLE_ROWS
    first = (step % per_group) * SAMPLE_ROWS
    rows = pl.ds(pl.multiple_of((step // per_group) * SUBLANES, SUBLANES), SUBLANES)
    _sample_project(step == 0, xs_ref, scos_ref, ssin_ref, win_ref, wab_ref, alog_ref, dtb_ref,
                    su_ref, seg_ref, sbeta_ref, smix_ref)
    _sample_group(first == 0, rows, sh0_ref, srgb0_ref, sgb0_ref, wg_ref, bg_ref, lam_ref, rcw_ref, rcb_ref,
                  gcw_ref, sh_ref, srgb_ref, sgb_ref, su_ref, smix_ref)
    _sample_states(first, rows, gam_ref, sret0_ref, sgdn0_ref, gnw_ref, gnb_ref, gnorm_ref,
                   sret_ref, sgdn_ref, su_ref, seg_ref, sbeta_ref, smix_ref)
    _prompt_layer_kernel(*p_in, *p_out, *p_scr, tb=tb, c_ret=c_ret, c_gdn=c_gdn)
    _sample_finish(step == last, xs_ref, wout_ref, lnw_ref, lnb_ref, ys_ref, smix_ref)


def _prompt_layer(layer, x, pos, init, wts, *, tb, c_ret, c_gdn, sample=None):
    bsz, tlen, _ = x.shape
    assert tlen % tb == 0 and tb % c_ret == 0 and tb % c_gdn == 0 and tb % SUBLANES == 0
    nt = tlen // tb
    cos2, sin2 = _rope_tables(pos)
    rtabs = _ret_tables(c_ret)

    def wspec(a):
        nd = a.ndim - 1
        return pl.BlockSpec((None,) + a.shape[1:], lambda b, t, _n=nd: (layer,) + (0,) * _n,
                            pipeline_mode=pl.Buffered(1))

    def cspec(a):
        nd = a.ndim
        return pl.BlockSpec(a.shape, lambda b, t, _n=nd: (0,) * _n, pipeline_mode=pl.Buffered(1))

    def ispec(a):
        nd = a.ndim - 1
        return pl.BlockSpec((None,) + a.shape[1:], lambda b, t, _n=nd: (0,) * (_n + 1),
                            pipeline_mode=pl.Buffered(1))

    def ospec(shape):
        nd = len(shape)
        return pl.BlockSpec((None,) + shape, lambda b, t, _n=nd: (b,) + (0,) * _n)

    in_specs = ([pl.BlockSpec((None, tb, D_MODEL), lambda b, t: (b, t, 0)),
                 pl.BlockSpec((tb, DK), lambda b, t: (t, 0)),
                 pl.BlockSpec((tb, DK), lambda b, t: (t, 0))]
                + [ispec(a) for a in init] + [wspec(a) for a in wts] + [cspec(a) for a in rtabs])
    out_shape = (jax.ShapeDtypeStruct((bsz, tlen, D_MODEL), F32),
                 jax.ShapeDtypeStruct((bsz, 1, W), F32),
                 jax.ShapeDtypeStruct((bsz, CONV_W - 1, W), F32),
                 jax.ShapeDtypeStruct((bsz, HEADS, DK, DK), F32),
                 jax.ShapeDtypeStruct((bsz, CONV_W - 1, 3 * W), F32),
                 jax.ShapeDtypeStruct((bsz, HEADS, DK, DK), F32))
    out_specs = (pl.BlockSpec((None, tb, D_MODEL), lambda b, t: (b, t, 0)),
                 ospec((1, W)), ospec((CONV_W - 1, W)), ospec((HEADS, DK, DK)),
                 ospec((CONV_W - 1, 3 * W)), ospec((HEADS, DK, DK)))
    scratch = [pltpu.VMEM((tb, D_MAIN), F32),
               pltpu.VMEM((SUBLANES, N_CONV), F32),
               pltpu.VMEM((tb, W), F32),
               pltpu.VMEM((tb, 3 * W), F32)]
    ins = (x, cos2, sin2, *init, *wts, *rtabs)
    params = dict(dimension_semantics=("arbitrary", "arbitrary"))
    if sample is None:
        kern = functools.partial(_prompt_layer_kernel, tb=tb, c_ret=c_ret, c_gdn=c_gdn)
        return pl.pallas_call(
            kern, grid=(bsz, nt), in_specs=in_specs, out_specs=out_specs, out_shape=out_shape,
            scratch_shapes=scratch,
            compiler_params=pltpu.CompilerParams(vmem_limit_bytes=56 * 1024 * 1024, **params),
            name=f"prompt_layer{layer}_t{tlen}",
        )(*ins)

    xs, sh0, srgb0, sret0, sgb0, sgdn0, ret_acc, gdn_acc = sample
    nb = xs.shape[0]
    assert nb == bsz * nt * SAMPLE_ROWS, "one group of SAMPLE_ROWS sample sequences per grid step"
    scos, ssin = _rope_tables(jnp.arange(1, dtype=F32) + float(PAST_LEN))
    lg = jnp.log1p(-jnp.exp2(-5.0 - jnp.arange(HEADS, dtype=F32)))
    gam = jnp.broadcast_to(jnp.exp(lg)[:, None, None], (HEADS, 1, DK))
    whole = lambda s: pl.BlockSpec(tuple(s), lambda b, t, _n=len(s): (0,) * _n, pipeline_mode=pl.Buffered(1))
    of_layer = lambda a: pl.BlockSpec((None,) + a.shape[1:], lambda b, t, _n=a.ndim - 1: (layer,) + (0,) * _n,
                                      pipeline_mode=pl.Buffered(1))
    rows_of_step = lambda a: pl.BlockSpec((None, SAMPLE_ROWS) + a.shape[2:],
                                          lambda b, t: (layer, b * nt + t, 0, 0, 0))
    per_group = SUBLANES // SAMPLE_ROWS
    group_in = lambda a: pl.BlockSpec((None, CONV_W - 1, SUBLANES, a.shape[3]),
                                      lambda b, t: (layer, 0, (b * nt + t) // per_group, 0))
    group_out = lambda a: pl.BlockSpec((CONV_W - 1, SUBLANES, a.shape[3]),
                                       lambda b, t: (0, (b * nt + t) // per_group, 0))
    s_ins = (xs, scos, ssin, gam, sh0, srgb0, sret0, sgb0, sgdn0)
    s_specs = [whole(xs.shape), whole(scos.shape), whole(ssin.shape), whole(gam.shape),
               of_layer(sh0), group_in(srgb0), rows_of_step(sret0), group_in(sgb0), rows_of_step(sgdn0)]
    aliases, alias_ins = {}, ()
    if ret_acc is not None:
        alias_ins = (ret_acc, gdn_acc)
        base = N_PROMPT_IN + N_SAMPLE_IN
        aliases = {base: N_PROMPT_OUT + 3, base + 1: N_PROMPT_OUT + 5}
    s_shape = (jax.ShapeDtypeStruct(xs.shape, F32), jax.ShapeDtypeStruct(sh0.shape[1:], F32),
               jax.ShapeDtypeStruct(srgb0.shape[1:], F32), jax.ShapeDtypeStruct(sret0.shape, F32),
               jax.ShapeDtypeStruct(sgb0.shape[1:], F32), jax.ShapeDtypeStruct(sgdn0.shape, F32))
    s_out_specs = (whole(xs.shape), whole(sh0.shape[1:]), group_out(srgb0), rows_of_step(sret0),
                   group_out(sgb0), rows_of_step(sgdn0))
    s_scratch = [pltpu.VMEM((nb, D_MAIN), F32), pltpu.VMEM((nb, LANES), F32), pltpu.VMEM((nb, LANES), F32),
                 pltpu.VMEM((nb, 3 * W), F32)]
    kern = functools.partial(_fused_layer_kernel, tb=tb, c_ret=c_ret, c_gdn=c_gdn, nt=nt, n_alias=len(alias_ins))
    return pl.pallas_call(
        kern, grid=(bsz, nt),
        in_specs=in_specs + s_specs + [pl.BlockSpec(memory_space=pl.ANY)] * len(alias_ins),
        out_specs=out_specs + s_out_specs, out_shape=out_shape + s_shape,
        scratch_shapes=scratch + s_scratch, input_output_aliases=aliases,
        compiler_params=pltpu.CompilerParams(vmem_limit_bytes=60 * 1024 * 1024, **params),
        name=f"fused_layer{layer}_t{tlen}",
    )(*ins, *s_ins, *alias_ins)


def _cast_kernel(x_ref, o_ref):
    o_ref[...] = x_ref[...].astype(o_ref.dtype)


def _cast_main_rows(w_t):
    depth, _, d_model = w_t.shape
    rows = 4 * LANES
    return pl.pallas_call(
        _cast_kernel, grid=(depth, D_MAIN // rows),
        in_specs=[pl.BlockSpec((None, rows, d_model), lambda l, j: (l, j, 0))],
        out_specs=pl.BlockSpec((None, rows, d_model), lambda l, j: (l, j, 0)),
        out_shape=jax.ShapeDtypeStruct((depth, D_MAIN, d_model), BF16),
        name="cast_w_in",
    )(w_t)


def _prep_weights(w_in, rg_conv_w, rg_conv_b, rg_w_a, rg_b_a, rg_w_x, rg_b_x, rg_lambda,
                  ret_gn_w, ret_gn_b, gdn_conv_w, gdn_a_log, gdn_dt_bias, gdn_norm_w, w_out, ln_w, ln_b):
    eye = jnp.eye(RG_BLOCKS, dtype=F32)

    def bdiag(w):
        l, n, c, d = w.shape
        return jnp.einsum('lncd,nm->lncmd', w.astype(F32), eye).reshape(l, n * c, n * d)

    pad = LANES - 2 * HEADS
    w_t = jnp.transpose(w_in, (0, 2, 1))
    w_main = _cast_main_rows(w_t)
    w_ab = jnp.pad(w_t[:, D_MAIN:, :], ((0, 0), (0, pad), (0, 0))).astype(BF16)
    wg = jnp.concatenate([bdiag(rg_w_a), bdiag(rg_w_x)], axis=-1).astype(BF16)
    bg = jnp.concatenate([rg_b_a, rg_b_x], axis=-1)[:, None, :].astype(F32)
    row = lambda a: a[:, None, :].astype(F32)
    padh = lambda a: jnp.pad(a.astype(F32), ((0, 0), (0, LANES - HEADS)))[:, None, :]
    return (w_main, w_ab, wg, bg, row(rg_lambda), rg_conv_w.astype(F32), row(rg_conv_b),
            row(ret_gn_w), row(ret_gn_b), gdn_conv_w.astype(F32), padh(gdn_a_log), padh(gdn_dt_bias),
            row(gdn_norm_w), w_out.astype(BF16), row(ln_w), row(ln_b))


SAMPLE_ROWS = 4


def _sample_project(first_step, x_ref, cos_ref, sin_ref, win_ref, wab_ref, alog_ref, dtb_ref,
                    u_ref, eg_ref, beta_ref, mix_ref):
    @pl.when(first_step)
    def _project():
        xb = x_ref[...].astype(BF16)
        mix_ref[...] = jnp.zeros(mix_ref.shape, F32)
        u_ref[...] = _bdot_nt(xb, win_ref[...])
        ab = _bdot_nt(xb, wab_ref[...])
        eg_ref[...] = jnp.exp(-jnp.exp(alog_ref[...]) * jax.nn.softplus(ab + dtb_ref[...]))
        beta_ref[...] = jax.nn.sigmoid(ab)
        cosf = cos_ref[...]
        sinf = sin_ref[...]
        for hd in range(HEADS):
            qs = slice(C_RQ + hd * DK, C_RQ + (hd + 1) * DK)
            ks = slice(C_RK + hd * DK, C_RK + (hd + 1) * DK)
            q = u_ref[:, qs]
            k = u_ref[:, ks]
            u_ref[:, qs] = q * cosf + pltpu.roll(q, DK // 2, axis=1) * sinf
            u_ref[:, ks] = (k * cosf + pltpu.roll(k, DK // 2, axis=1) * sinf) * (DK ** -0.5)


def _sample_group(group_start, rows, h0_ref, rgb0_ref, gb0_ref, wg_ref, bg_ref, lam_ref, rcw_ref, rcb_ref, gcw_ref,
                  h_ref, rgb_ref, gb_ref, u_ref, mix_ref):
    @pl.when(group_start)
    def _group():
        cur = u_ref[rows, C_RGX:C_RGX + W]
        xc = rcb_ref[...] + rcw_ref[CONV_W - 1:CONV_W, :] * cur
        for k in range(CONV_W - 1):
            xc = xc + rcw_ref[k:k + 1, :] * rgb0_ref[k]
        rgb_ref[0] = rgb0_ref[1]
        rgb_ref[1] = rgb0_ref[2]
        rgb_ref[2] = cur
        gates = jnp.dot(xc.astype(BF16), wg_ref[...], preferred_element_type=F32) + bg_ref[...]
        r = jax.nn.sigmoid(gates[:, :W])
        i = jax.nn.sigmoid(gates[:, W:])
        log_a = (-RG_C) * r * jax.nn.softplus(-lam_ref[...])
        a = jnp.exp(log_a)
        h = a * h0_ref[rows, :] + jnp.sqrt(1.0 - a * a) * (i * xc)
        h_ref[rows, :] = h
        mix_ref[rows, 0:W] = h * _silu(u_ref[rows, C_RGZ:C_RGZ + W])

        for j in range(3 * W // LANES):
            cs = slice(C_GQ + j * LANES, C_GQ + (j + 1) * LANES)
            ws = slice(j * LANES, (j + 1) * LANES)
            cur = u_ref[rows, cs]
            acc = gcw_ref[CONV_W - 1:CONV_W, ws] * cur
            for k in range(CONV_W - 1):
                acc = acc + gcw_ref[k:k + 1, ws] * gb0_ref[k, :, ws]
            gb_ref[0, :, ws] = gb0_ref[1, :, ws]
            gb_ref[1, :, ws] = gb0_ref[2, :, ws]
            gb_ref[2, :, ws] = cur
            y = _silu(acc)
            if j < 2 * HEADS:
                y = y * lax.rsqrt(jnp.sum(y * y, axis=-1, keepdims=True) + EPS)
                if j < HEADS:
                    y = y * (DK ** -0.5)
            u_ref[rows, cs] = y


def _sample_states(first, rows, gam_ref, sret0_ref, sgdn0_ref, gnw_ref, gnb_ref, gnorm_ref,
                   sret_ref, sgdn_ref, u_ref, eg_ref, beta_ref, mix_ref):
    n = SAMPLE_ROWS
    egb = eg_ref[rows, :]
    btb = beta_ref[rows, :]
    rid = lax.broadcasted_iota(jnp.int32, (SUBLANES, DK), 0)
    rid2 = lax.broadcasted_iota(jnp.int32, (2 * SUBLANES, DK), 0) % SUBLANES
    mine = jnp.logical_and(rid >= first, rid < first + n)

    def own_rows(x):
        return jnp.concatenate([jnp.where(rid == first + i, x, 0.0) for i in range(n)], axis=1)

    for hd in range(HEADS):
        q = u_ref[rows, C_RQ + hd * DK:C_RQ + (hd + 1) * DK]
        k = u_ref[rows, C_RK + hd * DK:C_RK + (hd + 1) * DK]
        v = u_ref[rows, C_RV + hd * DK:C_RV + (hd + 1) * DK]
        z = u_ref[rows, C_RZ + hd * DK:C_RZ + (hd + 1) * DK]
        qk = jnp.sum(q * k, axis=-1, keepdims=True)
        gam = gam_ref[hd]
        qs_ = jnp.zeros((SUBLANES, DK), F32)
        for i in range(n):
            qs_ = jnp.where(rid == first + i, _bdot(q, sret0_ref[i, hd]), qs_)
        o = qk * v + gam * qs_
        kv = _bdot_tn(k, own_rows(v))
        for i in range(n):
            sret_ref[i, hd] = gam * sret0_ref[i, hd] + kv[:, i * DK:(i + 1) * DK]
        mu = jnp.mean(o, axis=-1, keepdims=True)
        d = o - mu
        on = d * lax.rsqrt(jnp.mean(d * d, axis=-1, keepdims=True) + EPS)
        gs = slice(hd * DK, (hd + 1) * DK)
        cs = slice(W + hd * DK, W + (hd + 1) * DK)
        mix_ref[rows, cs] = jnp.where(mine, (on * gnw_ref[:, gs] + gnb_ref[:, gs]) * _silu(z), mix_ref[rows, cs])

        q = u_ref[rows, C_GQ + hd * DK:C_GQ + (hd + 1) * DK]
        k = u_ref[rows, C_GK + hd * DK:C_GK + (hd + 1) * DK]
        v = u_ref[rows, C_GV + hd * DK:C_GV + (hd + 1) * DK]
        z = u_ref[rows, C_GZ + hd * DK:C_GZ + (hd + 1) * DK]
        qk = jnp.sum(q * k, axis=-1, keepdims=True)
        eg = jnp.broadcast_to(egb[:, hd:hd + 1], (SUBLANES, DK))
        beta = jnp.broadcast_to(btb[:, HEADS + hd:HEADS + hd + 1], (SUBLANES, DK))
        kq = jnp.concatenate([k, q], axis=0)
        kqs = jnp.zeros((2 * SUBLANES, DK), F32)
        for i in range(n):
            kqs = jnp.where(rid2 == first + i, _bdot(kq, sgdn0_ref[i, hd]), kqs)
        v_new = beta * (v - eg * kqs[:SUBLANES])
        o = eg * kqs[SUBLANES:] + qk * v_new
        kv = _bdot_tn(k, own_rows(v_new))
        for i in range(n):
            eg_i = jnp.sum(jnp.where(rid == first + i, eg, 0.0), axis=0, keepdims=True)
            sgdn_ref[i, hd] = sgdn0_ref[i, hd] * eg_i + kv[:, i * DK:(i + 1) * DK]
        on = o * lax.rsqrt(jnp.mean(o * o, axis=-1, keepdims=True) + EPS) * gnorm_ref[...]
        cs = slice(2 * W + hd * DK, 2 * W + (hd + 1) * DK)
        mix_ref[rows, cs] = jnp.where(mine, on * _silu(z), mix_ref[rows, cs])


def _sample_finish(last_step, x_ref, wout_ref, lnw_ref, lnb_ref, y_ref, mix_ref):
    @pl.when(last_step)
    def _finish():
        out = jnp.dot(mix_ref[...].astype(BF16), wout_ref[...], preferred_element_type=F32)
        r = ALPHA * x_ref[...] + out
        mu = jnp.mean(r, axis=-1, keepdims=True)
        d = r - mu
        var = jnp.mean(d * d, axis=-1, keepdims=True)
        y_ref[...] = d * lax.rsqrt(var + EPS) * lnw_ref[...] + lnb_ref[...]


def kernel(x_prompt, x_sample, state_rglru_h, state_rglru_conv, state_ret, state_gdn_conv, state_gdn,
           meta_tokens, w_in, rg_conv_w, rg_conv_b, rg_w_a, rg_b_a, rg_w_x, rg_b_x, rg_lambda,
           ret_gn_w, ret_gn_b, gdn_conv_w, gdn_a_log, gdn_dt_bias, gdn_norm_w, w_out, ln_w, ln_b):
    bp, seq, _ = x_prompt.shape
    nb = x_sample.shape[0]
    wts = _prep_weights(w_in, rg_conv_w, rg_conv_b, rg_w_a, rg_b_a, rg_w_x, rg_b_x, rg_lambda,
                        ret_gn_w, ret_gn_b, gdn_conv_w, gdn_a_log, gdn_dt_bias, gdn_norm_w, w_out, ln_w, ln_b)

    pos = jnp.arange(N_META + seq, dtype=F32)
    zeros = lambda *s: jnp.zeros(s, F32)
    init0 = (zeros(1, 1, W), zeros(1, CONV_W - 1, W), zeros(1, HEADS, DK, DK),
             zeros(1, CONV_W - 1, 3 * W), zeros(1, HEADS, DK, DK))
    tap_major = lambda a: jnp.transpose(a, (0, 2, 1, 3))
    srgb0, sgb0 = tap_major(state_rglru_conv), tap_major(state_gdn_conv)
    xm = meta_tokens.astype(x_prompt.dtype)[None]
    xp = x_prompt
    xs = x_sample.reshape(nb, D_MODEL)
    sret = sgdn = None
    new_p = [[] for _ in range(5)]
    new_s = [[] for _ in range(3)]
    for l in range(DEPTH):
        xm, *st_m = _prompt_layer(l, xm, pos[:N_META], init0, wts, tb=N_META, c_ret=N_META, c_gdn=N_META)
        xp, *rest = _prompt_layer(l, xp, pos[N_META:], tuple(st_m), wts, tb=512, c_ret=256, c_gdn=64,
                                  sample=(xs, state_rglru_h, srgb0, state_ret, sgb0, state_gdn, sret, sgdn))
        st_p = rest[:5]
        xs, sh_l, srgb_l, sret, sgb_l, sgdn = rest[5:]
        for j in range(5):
            new_p[j].append(st_p[j])
        for j, a in enumerate((sh_l, srgb_l, sgb_l)):
            new_s[j].append(a)
    sp = [jnp.stack(a) for a in new_p]
    sp[0] = sp[0].reshape(DEPTH, bp, W)
    sh, srgb, sgb = (jnp.stack(a) for a in new_s)
    return (xp, xs.reshape(x_sample.shape), sp[0], sp[1], sp[2], sp[3], sp[4],
            sh, tap_major(srgb), sret, tap_major(sgb), sgdn)
```

```python
import functools
import math

import jax
import jax.numpy as jnp
from jax import lax
from jax.experimental import pallas as pl
from jax.experimental.pallas import tpu as pltpu

F32 = jnp.float32
BF16 = jnp.bfloat16

D_MODEL = 1024
DEPTH = 4
N_META = 16
PAST_LEN = 16384
W = 512
CONV_W = 4
RG_BLOCKS = 8
RG_C = 8.0
HEADS = 4
DK = W // HEADS
ROPE_BASE = 10000.0
EPS = 1e-6
ALPHA = (2.0 * DEPTH) ** 0.25
D_MAIN = 10 * W
LANES = 128
SUBLANES = 8
N_CONV = 4 * W

C_RGX, C_RGZ, C_RQ, C_RK, C_RV, C_RZ, C_GQ, C_GK, C_GV, C_GZ = (i * W for i in range(10))


def _bdot(a, b):
    return jnp.dot(a.astype(BF16), b.astype(BF16), preferred_element_type=F32)


def _bdot_nt(a, b):
    return lax.dot_general(a.astype(BF16), b.astype(BF16), (((1,), (1,)), ((), ())), preferred_element_type=F32)


def _bdot_tn(a, b):
    return lax.dot_general(a.astype(BF16), b.astype(BF16), (((0,), (0,)), ((), ())), preferred_element_type=F32)


def _bmm(a, b):
    return jnp.einsum('nij,njk->nik', a.astype(BF16), b.astype(BF16), preferred_element_type=F32)


def _bmm_nt(a, b):
    return jnp.einsum('nik,njk->nij', a.astype(BF16), b.astype(BF16), preferred_element_type=F32)


def _silu(x):
    return x * jax.nn.sigmoid(x)


def _scan_rows(a, b, h0):
    n, lanes = a.shape
    g = n // SUBLANES
    a3 = a.reshape(g, SUBLANES, lanes)
    b3 = b.reshape(g, SUBLANES, lanes)
    sub = lax.broadcasted_iota(jnp.int32, a3.shape, 1)
    s = 1
    while s < SUBLANES:
        m = sub >= s
        a_s = pltpu.roll(a3, s, axis=1)
        b_s = pltpu.roll(b3, s, axis=1)
        b3 = jnp.where(m, a3 * b_s + b3, b3)
        a3 = jnp.where(m, a3 * a_s, a3)
        s *= 2
    carry = h0
    hs = []
    for i in range(g):
        hi = a3[i] * carry + b3[i]
        carry = hi[SUBLANES - 1:SUBLANES, :]
        hs.append(hi)
    return jnp.concatenate(hs, axis=0), carry


def _causal_conv_strip(x, hist, taps):
    r8 = lax.broadcasted_iota(jnp.int32, hist.shape, 0)
    acc = taps[CONV_W - 1] * x
    for d in range(1, CONV_W):
        xs = pltpu.roll(x, d, axis=0)
        top = jnp.where(r8 < d, pltpu.roll(hist, d, axis=0), xs[0:SUBLANES])
        acc = acc + taps[CONV_W - 1 - d] * jnp.concatenate([top, xs[SUBLANES:]], axis=0)
    return acc


def _prompt_layer_kernel(
        x_ref, cos_ref, sin_ref,
        h0_ref, rgb0_ref, sret0_ref, gb0_ref, sgdn0_ref,
        win_ref, wab_ref, wg_ref, bg_ref, lam_ref, rcw_ref, rcb_ref, gnw_ref, gnb_ref,
        gcw_ref, alog_ref, dtb_ref, gnorm_ref, wout_ref, lnw_ref, lnb_ref,
        rdecay_ref, rqdec_ref, rkdec_ref, rsdec_ref,
        y_ref, h_ref, rgb_ref, sret_ref, gb_ref, sgdn_ref,
        u_ref, hist_ref, xc_ref, mix_ref,
        *, tb, c_ret, c_gdn):
    t = pl.program_id(1)

    @pl.when(t == 0)
    def _init():
        h_ref[...] = h0_ref[...]
        sret_ref[...] = sret0_ref[...]
        sgdn_ref[...] = sgdn0_ref[...]
        hist_ref[SUBLANES - 3:SUBLANES, 0:W] = rgb0_ref[...]
        hist_ref[SUBLANES - 3:SUBLANES, W:N_CONV] = gb0_ref[...]

    x = x_ref[...]
    xb = x.astype(BF16)

    def project(c0, c1):
        u_ref[:, c0:c1] = _bdot_nt(xb, win_ref[c0:c1, :])

    project(C_RGX, C_RQ)
    ab = _bdot_nt(xb, wab_ref[...])
    project(C_GQ, D_MAIN)

    for j in range(W // LANES):
        cs = slice(C_RGX + j * LANES, C_RGX + (j + 1) * LANES)
        ws = slice(j * LANES, (j + 1) * LANES)
        xj = u_ref[:, cs]
        taps = [rcw_ref[k:k + 1, ws] for k in range(CONV_W)]
        xc_ref[:, ws] = _causal_conv_strip(xj, hist_ref[:, ws], taps) + rcb_ref[:, ws]
        hist_ref[:, ws] = xj[tb - SUBLANES:tb, :]
        rgb_ref[:, ws] = xj[tb - 3:tb, :]
    gates = jnp.dot(xc_ref[...].astype(BF16), wg_ref[...], preferred_element_type=F32) + bg_ref[...]
    for j in range(W // LANES):
        zs = slice(C_RGZ + j * LANES, C_RGZ + (j + 1) * LANES)
        ws = slice(j * LANES, (j + 1) * LANES)
        r = jax.nn.sigmoid(gates[:, j * LANES:(j + 1) * LANES])
        i = jax.nn.sigmoid(gates[:, W + j * LANES:W + (j + 1) * LANES])
        log_a = (-RG_C) * r * jax.nn.softplus(-lam_ref[:, ws])
        a = jnp.exp(log_a)
        b = jnp.sqrt(1.0 - a * a) * (i * xc_ref[:, ws])
        h, h_last = _scan_rows(a, b, h_ref[:, ws])
        h_ref[:, ws] = h_last
        mix_ref[:, ws] = h * _silu(u_ref[:, zs])

    project(C_RQ, C_GQ)
    cosf = cos_ref[...]
    sinf = sin_ref[...]
    for hd in range(HEADS):
        q = u_ref[:, C_RQ + hd * DK:C_RQ + (hd + 1) * DK]
        k = u_ref[:, C_RK + hd * DK:C_RK + (hd + 1) * DK]
        v = u_ref[:, C_RV + hd * DK:C_RV + (hd + 1) * DK]
        z = u_ref[:, C_RZ + hd * DK:C_RZ + (hd + 1) * DK]
        q = q * cosf + pltpu.roll(q, DK // 2, axis=1) * sinf
        k = (k * cosf + pltpu.roll(k, DK // 2, axis=1) * sinf) * (DK ** -0.5)
        s = sret_ref[hd]
        outs = []
        for c in range(tb // c_ret):
            rs = slice(c * c_ret, (c + 1) * c_ret)
            qc, kc, vc = q[rs], k[rs], v[rs]
            sc = _bdot_nt(qc, kc) * rdecay_ref[hd]
            outs.append(_bdot(sc, vc) + _bdot(qc * rqdec_ref[hd], s))
            s = s * rsdec_ref[hd] + _bdot_tn(kc * rkdec_ref[hd], vc)
        sret_ref[hd] = s
        o = outs[0] if len(outs) == 1 else jnp.concatenate(outs, axis=0)
        mu = jnp.mean(o, axis=-1, keepdims=True)
        d = o - mu
        on = d * lax.rsqrt(jnp.mean(d * d, axis=-1, keepdims=True) + EPS)
        gs = slice(hd * DK, (hd + 1) * DK)
        mix_ref[:, W + hd * DK:W + (hd + 1) * DK] = (on * gnw_ref[:, gs] + gnb_ref[:, gs]) * _silu(z)

    for j in range(3 * W // LANES):
        cs = slice(C_GQ + j * LANES, C_GQ + (j + 1) * LANES)
        ws = slice(j * LANES, (j + 1) * LANES)
        hs = slice(W + j * LANES, W + (j + 1) * LANES)
        xj = u_ref[:, cs]
        taps = [gcw_ref[k:k + 1, ws] for k in range(CONV_W)]
        u_ref[:, cs] = _silu(_causal_conv_strip(xj, hist_ref[:, hs], taps))
        hist_ref[:, hs] = xj[tb - SUBLANES:tb, :]
        gb_ref[:, ws] = xj[tb - 3:tb, :]

    nc = tb // c_gdn
    glog = -jnp.exp(alog_ref[...]) * jax.nn.softplus(ab + dtb_ref[...])
    beta_all = jax.nn.sigmoid(ab)
    in_chunk = lax.broadcasted_iota(jnp.int32, (tb, LANES), 0) % c_gdn
    gcs = glog
    sh = 1
    while sh < c_gdn:
        gcs = gcs + jnp.where(in_chunk >= sh, pltpu.roll(gcs, sh, axis=0), 0.0)
        sh *= 2

    ri = lax.broadcasted_iota(jnp.int32, (c_gdn, c_gdn), 0)
    ci = lax.broadcasted_iota(jnp.int32, (c_gdn, c_gdn), 1)
    tril = ri >= ci
    strict = ri > ci
    eye_f = jnp.where(ri == ci, 1.0, 0.0).astype(F32)
    n_lv = int(math.ceil(math.log2(c_gdn)))

    names = ("q", "k", "kb", "vb", "kbe", "qe", "kd", "gl", "dec")
    parts = {n: [[None] * HEADS for _ in range(nc)] for n in names}
    for hd in range(HEADS):
        q = u_ref[:, C_GQ + hd * DK:C_GQ + (hd + 1) * DK]
        k = u_ref[:, C_GK + hd * DK:C_GK + (hd + 1) * DK]
        v = u_ref[:, C_GV + hd * DK:C_GV + (hd + 1) * DK]
        q = q * lax.rsqrt(jnp.sum(q * q, axis=-1, keepdims=True) + EPS) * (DK ** -0.5)
        k = k * lax.rsqrt(jnp.sum(k * k, axis=-1, keepdims=True) + EPS)
        gc = jnp.broadcast_to(gcs[:, hd:hd + 1], (tb, LANES))
        beta = jnp.broadcast_to(beta_all[:, HEADS + hd:HEADS + hd + 1], (tb, LANES))
        egc = jnp.exp(gc)
        kb = k * beta
        vb = v * beta
        kbe = kb * egc
        qe = q * egc
        for c in range(nc):
            rs = slice(c * c_gdn, (c + 1) * c_gdn)
            gcc = gc[rs]
            gl = gcc[c_gdn - 1:c_gdn, :]
            diff = gcc[:, :c_gdn] - gcc.T[:c_gdn, :]
            parts["dec"][c][hd] = jnp.where(tril, jnp.exp(jnp.where(tril, diff, 0.0)), 0.0)
            parts["gl"][c][hd] = gl
            parts["kd"][c][hd] = k[rs] * jnp.exp(gl - gcc)
            for n, val in (("q", q), ("k", k), ("kb", kb), ("vb", vb), ("kbe", kbe), ("qe", qe)):
                parts[n][c][hd] = val[rs]
    st = {n: jnp.stack([parts[n][c][hd] for c in range(nc) for hd in range(HEADS)]) for n in names}

    a_low = jnp.where(strict, _bmm_nt(st["kb"], st["k"]) * st["dec"], 0.0)
    att = _bmm_nt(st["q"], st["k"]) * st["dec"]
    pt = jnp.concatenate([-a_low, jnp.broadcast_to(eye_f, a_low.shape)], axis=-1)
    right = lax.broadcasted_iota(jnp.int32, (c_gdn, 2 * c_gdn), 1) >= c_gdn
    for _ in range(n_lv):
        ptb = pt.astype(BF16)
        r = jnp.einsum('nij,njk->nik', ptb[..., :c_gdn], ptb, preferred_element_type=F32)
        pt = r + jnp.where(right, pt, 0.0)
    tinv = pt[..., c_gdn:]
    sol = _bmm(tinv, jnp.concatenate([st["vb"], st["kbe"]], axis=-1))
    uu, ww = sol[..., :DK], sol[..., DK:]

    s = sgdn_ref[...]
    outs = []
    for c in range(nc):
        hs4 = slice(c * HEADS, (c + 1) * HEADS)
        ws_ = _bmm(jnp.concatenate([ww[hs4], st["qe"][hs4]], axis=1), s)
        v_new = uu[hs4] - ws_[:, :c_gdn]
        outs.append(ws_[:, c_gdn:] + _bmm(att[hs4], v_new))
        kv = jnp.einsum('hik,hiv->hkv', st["kd"][hs4].astype(BF16), v_new.astype(BF16),
                        preferred_element_type=F32)
        s = s * jnp.exp(st["gl"][hs4]) + kv
    sgdn_ref[...] = s
    for hd in range(HEADS):
        z = u_ref[:, C_GZ + hd * DK:C_GZ + (hd + 1) * DK]
        o = outs[0][hd] if nc == 1 else jnp.concatenate([outs[c][hd] for c in range(nc)], axis=0)
        on = o * lax.rsqrt(jnp.mean(o * o, axis=-1, keepdims=True) + EPS) * gnorm_ref[...]
        mix_ref[:, 2 * W + hd * DK:2 * W + (hd + 1) * DK] = on * _silu(z)

    out = jnp.dot(mix_ref[...].astype(BF16), wout_ref[...], preferred_element_type=F32)
    r = ALPHA * x + out
    mu = jnp.mean(r, axis=-1, keepdims=True)
    d = r - mu
    var = jnp.mean(d * d, axis=-1, keepdims=True)
    y_ref[...] = d * lax.rsqrt(var + EPS) * lnw_ref[...] + lnb_ref[...]


def _ret_tables(c):
    lg = jnp.log1p(-jnp.exp2(-5.0 - jnp.arange(HEADS, dtype=F32)))[:, None, None]
    idx = jnp.arange(c, dtype=F32)
    diff = idx[:, None] - idx[None, :]
    decay = jnp.where(diff >= 0, jnp.exp(lg * jnp.maximum(diff, 0.0)), 0.0)
    q_dec = jnp.broadcast_to(jnp.exp(lg[:, 0] * (idx + 1.0))[:, :, None], (HEADS, c, DK))
    k_dec = jnp.broadcast_to(jnp.exp(lg[:, 0] * (c - 1.0 - idx))[:, :, None], (HEADS, c, DK))
    s_dec = jnp.broadcast_to(jnp.exp(lg * c), (HEADS, 1, DK))
    return decay, q_dec, k_dec, s_dec


def _rope_tables(pos):
    half = DK // 2
    inv = ROPE_BASE ** (-jnp.arange(half, dtype=F32) / half)
    ang = pos[:, None] * inv[None, :]
    cos, sin = jnp.cos(ang), jnp.sin(ang)
    return jnp.concatenate([cos, cos], -1), jnp.concatenate([-sin, sin], -1)


def _prompt_layer(layer, x, pos, init, wts, *, tb, c_ret, c_gdn):
    bsz, tlen, _ = x.shape
    assert tlen % tb == 0 and tb % c_ret == 0 and tb % c_gdn == 0 and tb % SUBLANES == 0
    nt = tlen // tb
    cos2, sin2 = _rope_tables(pos)
    rtabs = _ret_tables(c_ret)

    def wspec(a):
        nd = a.ndim - 1
        return pl.BlockSpec((None,) + a.shape[1:], lambda b, t, _n=nd: (layer,) + (0,) * _n,
                            pipeline_mode=pl.Buffered(1))

    def cspec(a):
        nd = a.ndim
        return pl.BlockSpec(a.shape, lambda b, t, _n=nd: (0,) * _n, pipeline_mode=pl.Buffered(1))

    def ispec(a):
        nd = a.ndim - 1
        return pl.BlockSpec((None,) + a.shape[1:], lambda b, t, _n=nd: (0,) * (_n + 1),
                            pipeline_mode=pl.Buffered(1))

    def ospec(shape):
        nd = len(shape)
        return pl.BlockSpec((None,) + shape, lambda b, t, _n=nd: (b,) + (0,) * _n)

    in_specs = ([pl.BlockSpec((None, tb, D_MODEL), lambda b, t: (b, t, 0)),
                 pl.BlockSpec((tb, DK), lambda b, t: (t, 0)),
                 pl.BlockSpec((tb, DK), lambda b, t: (t, 0))]
                + [ispec(a) for a in init] + [wspec(a) for a in wts] + [cspec(a) for a in rtabs])
    out_shape = (jax.ShapeDtypeStruct((bsz, tlen, D_MODEL), F32),
                 jax.ShapeDtypeStruct((bsz, 1, W), F32),
                 jax.ShapeDtypeStruct((bsz, CONV_W - 1, W), F32),
                 jax.ShapeDtypeStruct((bsz, HEADS, DK, DK), F32),
                 jax.ShapeDtypeStruct((bsz, CONV_W - 1, 3 * W), F32),
                 jax.ShapeDtypeStruct((bsz, HEADS, DK, DK), F32))
    out_specs = (pl.BlockSpec((None, tb, D_MODEL), lambda b, t: (b, t, 0)),
                 ospec((1, W)), ospec((CONV_W - 1, W)), ospec((HEADS, DK, DK)),
                 ospec((CONV_W - 1, 3 * W)), ospec((HEADS, DK, DK)))
    scratch = [pltpu.VMEM((tb, D_MAIN), F32),
               pltpu.VMEM((SUBLANES, N_CONV), F32),
               pltpu.VMEM((tb, W), F32),
               pltpu.VMEM((tb, 3 * W), F32)]
    kern = functools.partial(_prompt_layer_kernel, tb=tb, c_ret=c_ret, c_gdn=c_gdn)
    return pl.pallas_call(
        kern, grid=(bsz, nt), in_specs=in_specs, out_specs=out_specs, out_shape=out_shape,
        scratch_shapes=scratch,
        compiler_params=pltpu.CompilerParams(dimension_semantics=("arbitrary", "arbitrary"),
                                             vmem_limit_bytes=56 * 1024 * 1024),
        name=f"prompt_layer{layer}_t{tlen}",
    )(x, cos2, sin2, *init, *wts, *rtabs)


def _meta_kernel(*refs, n_tok):
    x_ref, body, y_ref, xcarry_ref = refs[0], refs[1:-1], refs[28], refs[-1]

    @pl.when(pl.program_id(0) == 0)
    def _first_layer():
        xcarry_ref[...] = x_ref[...]

    _prompt_layer_kernel(xcarry_ref, *body, tb=n_tok, c_ret=n_tok, c_gdn=n_tok)
    xcarry_ref[...] = y_ref[...]


def _meta_prefix(x, pos, init, wts):
    _, n_tok, _ = x.shape
    depth = wts[0].shape[0]
    cos2, sin2 = _rope_tables(pos)
    rtabs = _ret_tables(n_tok)
    const = lambda a: pl.BlockSpec(a.shape, lambda l, t, _n=a.ndim: (0,) * _n)
    squeeze0 = lambda a: pl.BlockSpec((None,) + a.shape[1:], lambda l, t, _n=a.ndim - 1: (0,) * (_n + 1))
    per_layer = lambda shape: pl.BlockSpec((None,) + shape, lambda l, t, _n=len(shape): (l,) + (0,) * _n)
    in_specs = ([squeeze0(x), const(cos2), const(sin2)] + [squeeze0(a) for a in init]
                + [per_layer(a.shape[1:]) for a in wts] + [const(a) for a in rtabs])
    st_shapes = ((1, W), (CONV_W - 1, W), (HEADS, DK, DK), (CONV_W - 1, 3 * W), (HEADS, DK, DK))
    out_shape = tuple(jax.ShapeDtypeStruct((depth,) + s, F32) for s in ((n_tok, D_MODEL),) + st_shapes)
    out_specs = tuple(per_layer(s) for s in ((n_tok, D_MODEL),) + st_shapes)
    scratch = [pltpu.VMEM((n_tok, D_MAIN), F32), pltpu.VMEM((SUBLANES, N_CONV), F32),
               pltpu.VMEM((n_tok, W), F32), pltpu.VMEM((n_tok, 3 * W), F32),
               pltpu.VMEM((n_tok, D_MODEL), F32)]
    outs = pl.pallas_call(
        functools.partial(_meta_kernel, n_tok=n_tok), grid=(depth, 1),
        in_specs=in_specs, out_specs=out_specs, out_shape=out_shape, scratch_shapes=scratch,
        compiler_params=pltpu.CompilerParams(dimension_semantics=("arbitrary", "arbitrary"),
                                             vmem_limit_bytes=48 * 1024 * 1024),
        name="meta_prefix",
    )(x, cos2, sin2, *init, *wts, *rtabs)
    return outs[1:]


def _cast_kernel(x_ref, o_ref):
    o_ref[...] = x_ref[...].astype(o_ref.dtype)


def _cast_main_rows(w_t):
    depth, _, d_model = w_t.shape
    rows = D_MAIN // 4
    return pl.pallas_call(
        _cast_kernel, grid=(depth, D_MAIN // rows),
        in_specs=[pl.BlockSpec((None, rows, d_model), lambda l, j: (l, j, 0))],
        out_specs=pl.BlockSpec((None, rows, d_model), lambda l, j: (l, j, 0)),
        out_shape=jax.ShapeDtypeStruct((depth, D_MAIN, d_model), BF16),
        compiler_params=pltpu.CompilerParams(vmem_limit_bytes=32 * 1024 * 1024),
        name="cast_w_in",
    )(w_t)


def _prep_weights(w_in, rg_conv_w, rg_conv_b, rg_w_a, rg_b_a, rg_w_x, rg_b_x, rg_lambda,
                  ret_gn_w, ret_gn_b, gdn_conv_w, gdn_a_log, gdn_dt_bias, gdn_norm_w, w_out, ln_w, ln_b):
    eye = jnp.eye(RG_BLOCKS, dtype=F32)

    def bdiag(w):
        l, n, c, d = w.shape
        return jnp.einsum('lncd,nm->lncmd', w.astype(F32), eye).reshape(l, n * c, n * d)

    pad = LANES - 2 * HEADS
    w_t = jnp.transpose(w_in, (0, 2, 1))
    w_main = _cast_main_rows(w_t)
    w_ab = jnp.pad(w_t[:, D_MAIN:, :], ((0, 0), (0, pad), (0, 0))).astype(BF16)
    wg = jnp.concatenate([bdiag(rg_w_a), bdiag(rg_w_x)], axis=-1).astype(BF16)
    bg = jnp.concatenate([rg_b_a, rg_b_x], axis=-1)[:, None, :].astype(F32)
    row = lambda a: a[:, None, :].astype(F32)
    padh = lambda a: jnp.pad(a.astype(F32), ((0, 0), (0, LANES - HEADS)))[:, None, :]
    return (w_main, w_ab, wg, bg, row(rg_lambda), rg_conv_w.astype(F32), row(rg_conv_b),
            row(ret_gn_w), row(ret_gn_b), gdn_conv_w.astype(F32), padh(gdn_a_log), padh(gdn_dt_bias),
            row(gdn_norm_w), w_out.astype(BF16), row(ln_w), row(ln_b))


def _sample_kernel(
        x_ref, cos_ref, sin_ref, gam_ref,
        h0_ref, rgb0_ref, sret0_ref, gb0_ref, sgdn0_ref,
        win_ref, wab_ref, wg_ref, bg_ref, lam_ref, rcw_ref, rcb_ref, gnw_ref, gnb_ref,
        gcw_ref, alog_ref, dtb_ref, gnorm_ref, wout_ref, lnw_ref, lnb_ref,
        y_ref, h_ref, rgb_ref, sret_ref, gb_ref, sgdn_ref,
        xcur_ref, u_ref, eg_ref, beta_ref, mix_ref,
        *, bb_rows, n_bb):
    layer = pl.program_id(0)
    bb = pl.program_id(1)

    @pl.when(jnp.logical_and(layer == 0, bb == 0))
    def _load_x():
        xcur_ref[...] = x_ref[...]

    @pl.when(bb == 0)
    def _project():
        xb = xcur_ref[...].astype(BF16)
        u_ref[...] = _bdot_nt(xb, win_ref[...])
        ab = _bdot_nt(xb, wab_ref[...])
        eg_ref[...] = jnp.exp(-jnp.exp(alog_ref[...]) * jax.nn.softplus(ab + dtb_ref[...]))
        beta_ref[...] = jax.nn.sigmoid(ab)

        cur = u_ref[:, C_RGX:C_RGX + W]
        xc = rcb_ref[...] + rcw_ref[CONV_W - 1:CONV_W, :] * cur
        for k in range(CONV_W - 1):
            xc = xc + rcw_ref[k:k + 1, :] * rgb0_ref[k]
        rgb_ref[0] = rgb0_ref[1]
        rgb_ref[1] = rgb0_ref[2]
        rgb_ref[2] = cur
        gates = jnp.dot(xc.astype(BF16), wg_ref[...], preferred_element_type=F32) + bg_ref[...]
        r = jax.nn.sigmoid(gates[:, :W])
        i = jax.nn.sigmoid(gates[:, W:])
        log_a = (-RG_C) * r * jax.nn.softplus(-lam_ref[...])
        a = jnp.exp(log_a)
        h = a * h0_ref[...] + jnp.sqrt(1.0 - a * a) * (i * xc)
        h_ref[...] = h
        mix_ref[:, 0:W] = h * _silu(u_ref[:, C_RGZ:C_RGZ + W])

        cosf = cos_ref[...]
        sinf = sin_ref[...]
        for hd in range(HEADS):
            qs = slice(C_RQ + hd * DK, C_RQ + (hd + 1) * DK)
            ks = slice(C_RK + hd * DK, C_RK + (hd + 1) * DK)
            q = u_ref[:, qs]
            k = u_ref[:, ks]
            u_ref[:, qs] = q * cosf + pltpu.roll(q, DK // 2, axis=1) * sinf
            u_ref[:, ks] = (k * cosf + pltpu.roll(k, DK // 2, axis=1) * sinf) * (DK ** -0.5)

        for j in range(3 * W // LANES):
            cs = slice(C_GQ + j * LANES, C_GQ + (j + 1) * LANES)
            ws = slice(j * LANES, (j + 1) * LANES)
            cur = u_ref[:, cs]
            acc = gcw_ref[CONV_W - 1:CONV_W, ws] * cur
            for k in range(CONV_W - 1):
                acc = acc + gcw_ref[k:k + 1, ws] * gb0_ref[k, :, ws]
            gb_ref[0, :, ws] = gb0_ref[1, :, ws]
            gb_ref[1, :, ws] = gb0_ref[2, :, ws]
            gb_ref[2, :, ws] = cur
            y = _silu(acc)
            if j < 2 * HEADS:
                y = y * lax.rsqrt(jnp.sum(y * y, axis=-1, keepdims=True) + EPS)
                if j < HEADS:
                    y = y * (DK ** -0.5)
            u_ref[:, cs] = y

    r0 = pl.multiple_of(bb * bb_rows, SUBLANES)
    rows = pl.ds(r0, bb_rows)
    egb = eg_ref[rows, :]
    btb = beta_ref[rows, :]
    rid = lax.broadcasted_iota(jnp.int32, (bb_rows, DK), 0)
    rid2 = lax.broadcasted_iota(jnp.int32, (2 * bb_rows, DK), 0)

    def own_rows(x):
        return jnp.concatenate([jnp.where(rid == i, x, 0.0) for i in range(bb_rows)], axis=1)

    for hd in range(HEADS):
        q = u_ref[rows, C_RQ + hd * DK:C_RQ + (hd + 1) * DK]
        k = u_ref[rows, C_RK + hd * DK:C_RK + (hd + 1) * DK]
        v = u_ref[rows, C_RV + hd * DK:C_RV + (hd + 1) * DK]
        z = u_ref[rows, C_RZ + hd * DK:C_RZ + (hd + 1) * DK]
        qk = jnp.sum(q * k, axis=-1, keepdims=True)
        gam = gam_ref[hd]
        qs_ = jnp.zeros((bb_rows, DK), F32)
        for i in range(bb_rows):
            qs_ = jnp.where(rid == i, _bdot(q, sret0_ref[i, hd]), qs_)
        o = qk * v + gam * qs_
        kv = _bdot_tn(k, own_rows(v))
        for i in range(bb_rows):
            sret_ref[i, hd] = gam * sret0_ref[i, hd] + kv[:, i * DK:(i + 1) * DK]
        mu = jnp.mean(o, axis=-1, keepdims=True)
        d = o - mu
        on = d * lax.rsqrt(jnp.mean(d * d, axis=-1, keepdims=True) + EPS)
        gs = slice(hd * DK, (hd + 1) * DK)
        mix_ref[rows, W + hd * DK:W + (hd + 1) * DK] = (on * gnw_ref[:, gs] + gnb_ref[:, gs]) * _silu(z)

        q = u_ref[rows, C_GQ + hd * DK:C_GQ + (hd + 1) * DK]
        k = u_ref[rows, C_GK + hd * DK:C_GK + (hd + 1) * DK]
        v = u_ref[rows, C_GV + hd * DK:C_GV + (hd + 1) * DK]
        z = u_ref[rows, C_GZ + hd * DK:C_GZ + (hd + 1) * DK]
        qk = jnp.sum(q * k, axis=-1, keepdims=True)
        eg = jnp.broadcast_to(egb[:, hd:hd + 1], (bb_rows, DK))
        beta = jnp.broadcast_to(btb[:, HEADS + hd:HEADS + hd + 1], (bb_rows, DK))
        kq = jnp.concatenate([k, q], axis=0)
        kqs = jnp.zeros((2 * bb_rows, DK), F32)
        for i in range(bb_rows):
            kqs = jnp.where(rid2 % bb_rows == i, _bdot(kq, sgdn0_ref[i, hd]), kqs)
        v_new = beta * (v - eg * kqs[:bb_rows])
        o = eg * kqs[bb_rows:] + qk * v_new
        kv = _bdot_tn(k, own_rows(v_new))
        for i in range(bb_rows):
            sgdn_ref[i, hd] = sgdn0_ref[i, hd] * eg[i:i + 1, :] + kv[:, i * DK:(i + 1) * DK]
        on = o * lax.rsqrt(jnp.mean(o * o, axis=-1, keepdims=True) + EPS) * gnorm_ref[...]
        mix_ref[rows, 2 * W + hd * DK:2 * W + (hd + 1) * DK] = on * _silu(z)

    @pl.when(bb == n_bb - 1)
    def _finish():
        out = jnp.dot(mix_ref[...].astype(BF16), wout_ref[...], preferred_element_type=F32)
        r = ALPHA * xcur_ref[...] + out
        mu = jnp.mean(r, axis=-1, keepdims=True)
        d = r - mu
        var = jnp.mean(d * d, axis=-1, keepdims=True)
        y = d * lax.rsqrt(var + EPS) * lnw_ref[...] + lnb_ref[...]
        xcur_ref[...] = y
        y_ref[...] = y


def _sample_path(x, h0, rgb0, sret0, gb0, sgdn0, wts, *, bb_rows):
    nb = x.shape[0]
    n_bb = nb // bb_rows
    pos = jnp.arange(1, dtype=F32) + float(PAST_LEN)
    cos2, sin2 = _rope_tables(pos)
    lg = jnp.log1p(-jnp.exp2(-5.0 - jnp.arange(HEADS, dtype=F32)))
    gam = jnp.broadcast_to(jnp.exp(lg)[:, None, None], (HEADS, 1, DK))

    def const(a):
        nd = a.ndim
        return pl.BlockSpec(a.shape, lambda l, b, _n=nd: (0,) * _n)

    def per_layer(a, prefetch=False):
        nd = a.ndim - 1
        return pl.BlockSpec((None,) + a.shape[1:], lambda l, b, _n=nd: (l,) + (0,) * _n,
                            pipeline_mode=pl.Buffered(2 if prefetch else 1))

    def per_block(a):
        return pl.BlockSpec((None, bb_rows) + a.shape[2:], lambda l, b: (l, b, 0, 0, 0))

    ins = (x, cos2, sin2, gam, h0, rgb0, sret0, gb0, sgdn0) + tuple(wts)
    in_specs = ([const(x), const(cos2), const(sin2), const(gam),
                 per_layer(h0), per_layer(rgb0), per_block(sret0), per_layer(gb0), per_block(sgdn0)]
                + [per_layer(a, prefetch=(i == 0)) for i, a in enumerate(wts)])
    out_shape = (jax.ShapeDtypeStruct(x.shape, F32),
                 jax.ShapeDtypeStruct(h0.shape, F32), jax.ShapeDtypeStruct(rgb0.shape, F32),
                 jax.ShapeDtypeStruct(sret0.shape, F32), jax.ShapeDtypeStruct(gb0.shape, F32),
                 jax.ShapeDtypeStruct(sgdn0.shape, F32))
    out_specs = (const(x), per_layer(h0), per_layer(rgb0), per_block(sret0), per_layer(gb0), per_block(sgdn0))
    scratch = [pltpu.VMEM((nb, D_MODEL), F32),
               pltpu.VMEM((nb, D_MAIN), F32),
               pltpu.VMEM((nb, LANES), F32),
               pltpu.VMEM((nb, LANES), F32),
               pltpu.VMEM((nb, 3 * W), F32)]
    kern = functools.partial(_sample_kernel, bb_rows=bb_rows, n_bb=n_bb)
    return pl.pallas_call(
        kern, grid=(DEPTH, n_bb), in_specs=in_specs, out_specs=out_specs, out_shape=out_shape,
        scratch_shapes=scratch,
        compiler_params=pltpu.CompilerParams(dimension_semantics=("arbitrary", "arbitrary"),
                                             vmem_limit_bytes=58 * 1024 * 1024),
        name="sample_path",
    )(*ins)


def kernel(x_prompt, x_sample, state_rglru_h, state_rglru_conv, state_ret, state_gdn_conv, state_gdn,
           meta_tokens, w_in, rg_conv_w, rg_conv_b, rg_w_a, rg_b_a, rg_w_x, rg_b_x, rg_lambda,
           ret_gn_w, ret_gn_b, gdn_conv_w, gdn_a_log, gdn_dt_bias, gdn_norm_w, w_out, ln_w, ln_b):
    bp, seq, _ = x_prompt.shape
    nb = x_sample.shape[0]
    wts = _prep_weights(w_in, rg_conv_w, rg_conv_b, rg_w_a, rg_b_a, rg_w_x, rg_b_x, rg_lambda,
                        ret_gn_w, ret_gn_b, gdn_conv_w, gdn_a_log, gdn_dt_bias, gdn_norm_w, w_out, ln_w, ln_b)

    pos = jnp.arange(N_META + seq, dtype=F32)
    zeros = lambda *s: jnp.zeros(s, F32)
    init0 = (zeros(1, 1, W), zeros(1, CONV_W - 1, W), zeros(1, HEADS, DK, DK),
             zeros(1, CONV_W - 1, 3 * W), zeros(1, HEADS, DK, DK))
    st_meta = _meta_prefix(meta_tokens.astype(x_prompt.dtype)[None], pos[:N_META], init0, wts)
    xp = x_prompt
    new_p = [[] for _ in range(5)]
    for l in range(DEPTH):
        st_m = tuple(a[l:l + 1] for a in st_meta)
        xp, *st_p = _prompt_layer(l, xp, pos[N_META:], st_m, wts, tb=512, c_ret=256, c_gdn=64)
        for j in range(5):
            new_p[j].append(st_p[j])
    sp = [jnp.stack(a) for a in new_p]
    sp[0] = sp[0].reshape(DEPTH, bp, W)

    tap_major = lambda a: jnp.transpose(a, (0, 2, 1, 3))
    ys, sh, srgb, sret, sgb, sgdn = _sample_path(
        x_sample.reshape(nb, D_MODEL), state_rglru_h, tap_major(state_rglru_conv), state_ret,
        tap_major(state_gdn_conv), state_gdn, wts, bb_rows=8)
    return (xp, ys.reshape(x_sample.shape), sp[0], sp[1], sp[2], sp[3], sp[4],
            sh, tap_major(srgb), sret, tap_major(sgb), sgdn)
```

```python
import functools
import math

import jax
import jax.numpy as jnp
from jax import lax
from jax.experimental import pallas as pl
from jax.experimental.pallas import tpu as pltpu

F32 = jnp.float32
BF16 = jnp.bfloat16

D_MODEL = 1024
DEPTH = 4
N_META = 16
PAST_LEN = 16384
W = 512
CONV_W = 4
RG_BLOCKS = 8
RG_C = 8.0
HEADS = 4
DK = W // HEADS
ROPE_BASE = 10000.0
EPS = 1e-6
ALPHA = (2.0 * DEPTH) ** 0.25
D_MAIN = 10 * W
LANES = 128
SUBLANES = 8
N_CONV = 4 * W

PROMPT_BLOCK = 512
RET_CHUNK = 256
GDN_CHUNK = 64
SAMPLE_BLOCK = SUBLANES

C_RGX, C_RGZ, C_RQ, C_RK, C_RV, C_RZ, C_GQ, C_GK, C_GV, C_GZ = (i * W for i in range(10))


def _bdot(a, b):
    return jnp.dot(a.astype(BF16), b.astype(BF16), preferred_element_type=F32)


def _bdot_nt(a, b):
    return lax.dot_general(a.astype(BF16), b.astype(BF16), (((1,), (1,)), ((), ())), preferred_element_type=F32)


def _bdot_tn(a, b):
    return lax.dot_general(a.astype(BF16), b.astype(BF16), (((0,), (0,)), ((), ())), preferred_element_type=F32)


def _bmm(a, b):
    return jnp.einsum('nij,njk->nik', a.astype(BF16), b.astype(BF16), preferred_element_type=F32)


def _bmm_nt(a, b):
    return jnp.einsum('nik,njk->nij', a.astype(BF16), b.astype(BF16), preferred_element_type=F32)


def _silu(x):
    return x * jax.nn.sigmoid(x)


def _sqrt01(y):
    return y * lax.rsqrt(jnp.maximum(y, 1e-30))


def _scan_rows(a, b, h0):
    n, lanes = a.shape
    g = n // SUBLANES
    a3 = a.reshape(g, SUBLANES, lanes)
    b3 = b.reshape(g, SUBLANES, lanes)
    sub = lax.broadcasted_iota(jnp.int32, a3.shape, 1)
    s = 1
    while s < SUBLANES:
        m = sub >= s
        a_s = pltpu.roll(a3, s, axis=1)
        b_s = pltpu.roll(b3, s, axis=1)
        b3 = jnp.where(m, a3 * b_s + b3, b3)
        a3 = jnp.where(m, a3 * a_s, a3)
        s *= 2
    carry = h0
    hs = []
    for i in range(g):
        hi = a3[i] * carry + b3[i]
        carry = hi[SUBLANES - 1:SUBLANES, :]
        hs.append(hi)
    return jnp.concatenate(hs, axis=0), carry


def _causal_conv_strip(x, hist, taps):
    assert CONV_W == 4
    r8 = lax.broadcasted_iota(jnp.int32, hist.shape, 0)

    def shift(v, prev, d):
        vs = pltpu.roll(v, d, axis=0)
        top = jnp.where(r8 < d, pltpu.roll(prev, d, axis=0), vs[0:SUBLANES])
        return jnp.concatenate([top, vs[SUBLANES:]], axis=0)

    w0, w1, w2, w3 = taps
    x1 = shift(x, hist, 1)
    z = w1 * x + w0 * x1
    z_hist = w1 * hist + w0 * pltpu.roll(hist, 1, axis=0)
    return w3 * x + w2 * x1 + shift(z, z_hist, 2)


def _prompt_layer_kernel(
        x_ref, cos_ref, sin_ref,
        h0_ref, rgb0_ref, sret0_ref, gb0_ref, sgdn0_ref,
        win_ref, wab_ref, wg_ref, bg_ref, lam_ref, rcw_ref, rcb_ref, gnw_ref, gnb_ref,
        gcw_ref, alog_ref, dtb_ref, gnorm_ref, wout_ref, lnw_ref, lnb_ref,
        rdecay_ref, rqdec_ref, rkdec_ref, rsdec_ref,
        y_ref, h_ref, rgb_ref, sret_ref, gb_ref, sgdn_ref,
        u_ref, hist_ref, xc_ref, mix_ref,
        *, tb, c_ret, c_gdn):
    t = pl.program_id(1)

    @pl.when(t == 0)
    def _init():
        h_ref[...] = h0_ref[...]
        sret_ref[...] = sret0_ref[...]
        sgdn_ref[...] = sgdn0_ref[...]
        hist_ref[SUBLANES - 3:SUBLANES, 0:W] = rgb0_ref[...]
        hist_ref[SUBLANES - 3:SUBLANES, W:N_CONV] = gb0_ref[...]

    x = x_ref[...]
    xb = x.astype(BF16)

    def project(c0, c1):
        u_ref[:, c0:c1] = _bdot_nt(xb, win_ref[c0:c1, :])

    project(C_RGX, C_RQ)
    ab = _bdot_nt(xb, wab_ref[...])
    project(C_GQ, D_MAIN)

    for j in range(W // LANES):
        cs = slice(C_RGX + j * LANES, C_RGX + (j + 1) * LANES)
        ws = slice(j * LANES, (j + 1) * LANES)
        xj = u_ref[:, cs]
        taps = [rcw_ref[k:k + 1, ws] for k in range(CONV_W)]
        xc_ref[:, ws] = _causal_conv_strip(xj, hist_ref[:, ws], taps) + rcb_ref[:, ws]
        hist_ref[:, ws] = xj[tb - SUBLANES:tb, :]
        rgb_ref[:, ws] = xj[tb - 3:tb, :]
    gates = jnp.dot(xc_ref[...].astype(BF16), wg_ref[...], preferred_element_type=F32) + bg_ref[...]
    for j in range(W // LANES):
        zs = slice(C_RGZ + j * LANES, C_RGZ + (j + 1) * LANES)
        ws = slice(j * LANES, (j + 1) * LANES)
        r = jax.nn.sigmoid(gates[:, j * LANES:(j + 1) * LANES])
        i = jax.nn.sigmoid(gates[:, W + j * LANES:W + (j + 1) * LANES])
        log_a = (-RG_C) * r * jax.nn.softplus(-lam_ref[:, ws])
        a = jnp.exp(log_a)
        b = _sqrt01(1.0 - a * a) * (i * xc_ref[:, ws])
        h, h_last = _scan_rows(a, b, h_ref[:, ws])
        h_ref[:, ws] = h_last
        mix_ref[:, ws] = h * _silu(u_ref[:, zs])

    project(C_RQ, C_GQ)
    cosf = cos_ref[...]
    sinf = sin_ref[...]
    for hd in range(HEADS):
        q = u_ref[:, C_RQ + hd * DK:C_RQ + (hd + 1) * DK]
        k = u_ref[:, C_RK + hd * DK:C_RK + (hd + 1) * DK]
        v = u_ref[:, C_RV + hd * DK:C_RV + (hd + 1) * DK]
        z = u_ref[:, C_RZ + hd * DK:C_RZ + (hd + 1) * DK]
        q = q * cosf + pltpu.roll(q, DK // 2, axis=1) * sinf
        k = (k * cosf + pltpu.roll(k, DK // 2, axis=1) * sinf) * (DK ** -0.5)
        s = sret_ref[hd]
        outs = []
        for c in range(tb // c_ret):
            rs = slice(c * c_ret, (c + 1) * c_ret)
            qc, kc, vc = q[rs], k[rs], v[rs]
            sc = _bdot_nt(qc, kc) * rdecay_ref[hd]
            outs.append(_bdot(sc, vc) + _bdot(qc * rqdec_ref[hd], s))
            s = s * rsdec_ref[hd] + _bdot_tn(kc * rkdec_ref[hd], vc)
        sret_ref[hd] = s
        o = outs[0] if len(outs) == 1 else jnp.concatenate(outs, axis=0)
        mu = jnp.mean(o, axis=-1, keepdims=True)
        d = o - mu
        on = d * lax.rsqrt(jnp.mean(d * d, axis=-1, keepdims=True) + EPS)
        gs = slice(hd * DK, (hd + 1) * DK)
        mix_ref[:, W + hd * DK:W + (hd + 1) * DK] = (on * gnw_ref[:, gs] + gnb_ref[:, gs]) * _silu(z)

    for j in range(3 * W // LANES):
        cs = slice(C_GQ + j * LANES, C_GQ + (j + 1) * LANES)
        ws = slice(j * LANES, (j + 1) * LANES)
        hs = slice(W + j * LANES, W + (j + 1) * LANES)
        xj = u_ref[:, cs]
        taps = [gcw_ref[k:k + 1, ws] for k in range(CONV_W)]
        u_ref[:, cs] = _silu(_causal_conv_strip(xj, hist_ref[:, hs], taps))
        hist_ref[:, hs] = xj[tb - SUBLANES:tb, :]
        gb_ref[:, ws] = xj[tb - 3:tb, :]

    nc = tb // c_gdn
    glog = -jnp.exp(alog_ref[...]) * jax.nn.softplus(ab + dtb_ref[...])
    beta_all = jax.nn.sigmoid(ab)
    in_chunk = lax.broadcasted_iota(jnp.int32, (tb, LANES), 0) % c_gdn
    gcs = glog
    sh = 1
    while sh < c_gdn:
        gcs = gcs + jnp.where(in_chunk >= sh, pltpu.roll(gcs, sh, axis=0), 0.0)
        sh *= 2

    ri = lax.broadcasted_iota(jnp.int32, (c_gdn, c_gdn), 0)
    ci = lax.broadcasted_iota(jnp.int32, (c_gdn, c_gdn), 1)
    tril = ri >= ci
    strict = ri > ci
    eye_f = jnp.where(ri == ci, 1.0, 0.0).astype(F32)
    n_lv = int(math.ceil(math.log2(c_gdn)))

    names = ("q", "k", "kb", "vb", "kbe", "qe", "kd", "gl", "dec")
    parts = {n: [[None] * HEADS for _ in range(nc)] for n in names}
    for hd in range(HEADS):
        q = u_ref[:, C_GQ + hd * DK:C_GQ + (hd + 1) * DK]
        k = u_ref[:, C_GK + hd * DK:C_GK + (hd + 1) * DK]
        v = u_ref[:, C_GV + hd * DK:C_GV + (hd + 1) * DK]
        q = q * lax.rsqrt(jnp.sum(q * q, axis=-1, keepdims=True) + EPS) * (DK ** -0.5)
        k = k * lax.rsqrt(jnp.sum(k * k, axis=-1, keepdims=True) + EPS)
        gc = jnp.broadcast_to(gcs[:, hd:hd + 1], (tb, LANES))
        beta = jnp.broadcast_to(beta_all[:, HEADS + hd:HEADS + hd + 1], (tb, LANES))
        egc = jnp.exp(gc)
        kb = k * beta
        vb = v * beta
        kbe = kb * egc
        qe = q * egc
        for c in range(nc):
            rs = slice(c * c_gdn, (c + 1) * c_gdn)
            gcc = gc[rs]
            gl = gcc[c_gdn - 1:c_gdn, :]
            diff = gcc[:, :c_gdn] - gcc.T[:c_gdn, :]
            parts["dec"][c][hd] = jnp.where(tril, jnp.exp(jnp.where(tril, diff, 0.0)), 0.0)
            parts["gl"][c][hd] = gl
            parts["kd"][c][hd] = k[rs] * jnp.exp(gl - gcc)
            for n, val in (("q", q), ("k", k), ("kb", kb), ("vb", vb), ("kbe", kbe), ("qe", qe)):
                parts[n][c][hd] = val[rs]
    st = {n: jnp.stack([parts[n][c][hd] for c in range(nc) for hd in range(HEADS)]) for n in names}

    a_low = jnp.where(strict, _bmm_nt(st["kb"], st["k"]) * st["dec"], 0.0)
    att = _bmm_nt(st["q"], st["k"]) * st["dec"]
    pt = jnp.concatenate([-a_low, jnp.broadcast_to(eye_f, a_low.shape)], axis=-1)
    right = lax.broadcasted_iota(jnp.int32, (c_gdn, 2 * c_gdn), 1) >= c_gdn
    for _ in range(n_lv):
        ptb = pt.astype(BF16)
        r = jnp.einsum('nij,njk->nik', ptb[..., :c_gdn], ptb, preferred_element_type=F32)
        pt = r + jnp.where(right, pt, 0.0)
    tinv = pt[..., c_gdn:]
    sol = _bmm(tinv, jnp.concatenate([st["vb"], st["kbe"]], axis=-1))
    uu, ww = sol[..., :DK], sol[..., DK:]

    s = sgdn_ref[...]
    outs = []
    for c in range(nc):
        hs4 = slice(c * HEADS, (c + 1) * HEADS)
        ws_ = _bmm(jnp.concatenate([ww[hs4], st["qe"][hs4]], axis=1), s)
        v_new = uu[hs4] - ws_[:, :c_gdn]
        outs.append(ws_[:, c_gdn:] + _bmm(att[hs4], v_new))
        kv = jnp.einsum('hik,hiv->hkv', st["kd"][hs4].astype(BF16), v_new.astype(BF16),
                        preferred_element_type=F32)
        s = s * jnp.exp(st["gl"][hs4]) + kv
    sgdn_ref[...] = s
    for hd in range(HEADS):
        z = u_ref[:, C_GZ + hd * DK:C_GZ + (hd + 1) * DK]
        o = outs[0][hd] if nc == 1 else jnp.concatenate([outs[c][hd] for c in range(nc)], axis=0)
        on = o * lax.rsqrt(jnp.mean(o * o, axis=-1, keepdims=True) + EPS) * gnorm_ref[...]
        mix_ref[:, 2 * W + hd * DK:2 * W + (hd + 1) * DK] = on * _silu(z)

    out = jnp.dot(mix_ref[...].astype(BF16), wout_ref[...], preferred_element_type=F32)
    r = ALPHA * x + out
    mu = jnp.mean(r, axis=-1, keepdims=True)
    d = r - mu
    var = jnp.mean(d * d, axis=-1, keepdims=True)
    y_ref[...] = d * lax.rsqrt(var + EPS) * lnw_ref[...] + lnb_ref[...]


def _ret_tables(c):
    lg = jnp.log1p(-jnp.exp2(-5.0 - jnp.arange(HEADS, dtype=F32)))[:, None, None]
    idx = jnp.arange(c, dtype=F32)
    diff = idx[:, None] - idx[None, :]
    decay = jnp.where(diff >= 0, jnp.exp(lg * jnp.maximum(diff, 0.0)), 0.0)
    q_dec = jnp.broadcast_to(jnp.exp(lg[:, 0] * (idx + 1.0))[:, :, None], (HEADS, c, DK))
    k_dec = jnp.broadcast_to(jnp.exp(lg[:, 0] * (c - 1.0 - idx))[:, :, None], (HEADS, c, DK))
    s_dec = jnp.broadcast_to(jnp.exp(lg * c), (HEADS, 1, DK))
    return decay, q_dec, k_dec, s_dec


def _rope_tables(pos):
    half = DK // 2
    inv = ROPE_BASE ** (-jnp.arange(half, dtype=F32) / half)
    ang = pos[:, None] * inv[None, :]
    cos, sin = jnp.cos(ang), jnp.sin(ang)
    return jnp.concatenate([cos, cos], -1), jnp.concatenate([-sin, sin], -1)


def _prompt_layer(layer, x, pos, init, wts, *, tb, c_ret, c_gdn):
    bsz, tlen, _ = x.shape
    assert tlen % tb == 0 and tb % c_ret == 0 and tb % c_gdn == 0 and tb % SUBLANES == 0
    nt = tlen // tb
    cos2, sin2 = _rope_tables(pos)
    rtabs = _ret_tables(c_ret)

    def wspec(a):
        nd = a.ndim - 1
        return pl.BlockSpec((None,) + a.shape[1:], lambda b, t, _n=nd: (layer,) + (0,) * _n,
                            pipeline_mode=pl.Buffered(1))

    def cspec(a):
        nd = a.ndim
        return pl.BlockSpec(a.shape, lambda b, t, _n=nd: (0,) * _n, pipeline_mode=pl.Buffered(1))

    def ispec(a):
        nd = a.ndim - 1
        return pl.BlockSpec((None,) + a.shape[1:], lambda b, t, _n=nd: (0,) * (_n + 1),
                            pipeline_mode=pl.Buffered(1))

    def ospec(shape):
        nd = len(shape)
        return pl.BlockSpec((None,) + shape, lambda b, t, _n=nd: (b,) + (0,) * _n)

    in_specs = ([pl.BlockSpec((None, tb, D_MODEL), lambda b, t: (b, t, 0)),
                 pl.BlockSpec((tb, DK), lambda b, t: (t, 0)),
                 pl.BlockSpec((tb, DK), lambda b, t: (t, 0))]
                + [ispec(a) for a in init] + [wspec(a) for a in wts] + [cspec(a) for a in rtabs])
    out_shape = (jax.ShapeDtypeStruct((bsz, tlen, D_MODEL), F32),
                 jax.ShapeDtypeStruct((bsz, 1, W), F32),
                 jax.ShapeDtypeStruct((bsz, CONV_W - 1, W), F32),
                 jax.ShapeDtypeStruct((bsz, HEADS, DK, DK), F32),
                 jax.ShapeDtypeStruct((bsz, CONV_W - 1, 3 * W), F32),
                 jax.ShapeDtypeStruct((bsz, HEADS, DK, DK), F32))
    out_specs = (pl.BlockSpec((None, tb, D_MODEL), lambda b, t: (b, t, 0)),
                 ospec((1, W)), ospec((CONV_W - 1, W)), ospec((HEADS, DK, DK)),
                 ospec((CONV_W - 1, 3 * W)), ospec((HEADS, DK, DK)))
    scratch = [pltpu.VMEM((tb, D_MAIN), F32),
               pltpu.VMEM((SUBLANES, N_CONV), F32),
               pltpu.VMEM((tb, W), F32),
               pltpu.VMEM((tb, 3 * W), F32)]
    kern = functools.partial(_prompt_layer_kernel, tb=tb, c_ret=c_ret, c_gdn=c_gdn)
    return pl.pallas_call(
        kern, grid=(bsz, nt), in_specs=in_specs, out_specs=out_specs, out_shape=out_shape,
        scratch_shapes=scratch,
        compiler_params=pltpu.CompilerParams(dimension_semantics=("arbitrary", "arbitrary"),
                                             vmem_limit_bytes=56 * 1024 * 1024),
        name=f"prompt_layer{layer}_t{tlen}",
    )(x, cos2, sin2, *init, *wts, *rtabs)


def _meta_kernel(*refs, n_tok):
    x_ref, body, y_ref, xcarry_ref = refs[0], refs[1:-1], refs[28], refs[-1]

    @pl.when(pl.program_id(0) == 0)
    def _first_layer():
        xcarry_ref[...] = x_ref[...]

    _prompt_layer_kernel(xcarry_ref, *body, tb=n_tok, c_ret=n_tok, c_gdn=n_tok)
    xcarry_ref[...] = y_ref[...]


def _meta_prefix(x, pos, init, wts):
    _, n_tok, _ = x.shape
    depth = wts[0].shape[0]
    cos2, sin2 = _rope_tables(pos)
    rtabs = _ret_tables(n_tok)
    const = lambda a: pl.BlockSpec(a.shape, lambda l, t, _n=a.ndim: (0,) * _n)
    squeeze0 = lambda a: pl.BlockSpec((None,) + a.shape[1:], lambda l, t, _n=a.ndim - 1: (0,) * (_n + 1))
    per_layer = lambda shape: pl.BlockSpec((None,) + shape, lambda l, t, _n=len(shape): (l,) + (0,) * _n)
    in_specs = ([squeeze0(x), const(cos2), const(sin2)] + [squeeze0(a) for a in init]
                + [per_layer(a.shape[1:]) for a in wts] + [const(a) for a in rtabs])
    st_shapes = ((1, W), (CONV_W - 1, W), (HEADS, DK, DK), (CONV_W - 1, 3 * W), (HEADS, DK, DK))
    out_shape = tuple(jax.ShapeDtypeStruct((depth,) + s, F32) for s in ((n_tok, D_MODEL),) + st_shapes)
    out_specs = tuple(per_layer(s) for s in ((n_tok, D_MODEL),) + st_shapes)
    scratch = [pltpu.VMEM((n_tok, D_MAIN), F32), pltpu.VMEM((SUBLANES, N_CONV), F32),
               pltpu.VMEM((n_tok, W), F32), pltpu.VMEM((n_tok, 3 * W), F32),
               pltpu.VMEM((n_tok, D_MODEL), F32)]
    outs = pl.pallas_call(
        functools.partial(_meta_kernel, n_tok=n_tok), grid=(depth, 1),
        in_specs=in_specs, out_specs=out_specs, out_shape=out_shape, scratch_shapes=scratch,
        compiler_params=pltpu.CompilerParams(dimension_semantics=("arbitrary", "arbitrary"),
                                             vmem_limit_bytes=48 * 1024 * 1024),
        name="meta_prefix",
    )(x, cos2, sin2, *init, *wts, *rtabs)
    return outs[1:]


def _cast_kernel(x_ref, o_ref):
    o_ref[...] = x_ref[...].astype(o_ref.dtype)


def _cast_main_rows(w_t):
    depth, _, d_model = w_t.shape
    rows = D_MAIN // 4
    return pl.pallas_call(
        _cast_kernel, grid=(depth, D_MAIN // rows),
        in_specs=[pl.BlockSpec((None, rows, d_model), lambda l, j: (l, j, 0))],
        out_specs=pl.BlockSpec((None, rows, d_model), lambda l, j: (l, j, 0)),
        out_shape=jax.ShapeDtypeStruct((depth, D_MAIN, d_model), BF16),
        compiler_params=pltpu.CompilerParams(vmem_limit_bytes=32 * 1024 * 1024),
        name="cast_w_in",
    )(w_t)


def _prep_weights(w_in, rg_conv_w, rg_conv_b, rg_w_a, rg_b_a, rg_w_x, rg_b_x, rg_lambda,
                  ret_gn_w, ret_gn_b, gdn_conv_w, gdn_a_log, gdn_dt_bias, gdn_norm_w, w_out, ln_w, ln_b):
    eye = jnp.eye(RG_BLOCKS, dtype=F32)

    def bdiag(w):
        l, n, c, d = w.shape
        return jnp.einsum('lncd,nm->lncmd', w.astype(F32), eye).reshape(l, n * c, n * d)

    pad = LANES - 2 * HEADS
    w_t = jnp.transpose(w_in, (0, 2, 1))
    w_main = _cast_main_rows(w_t)
    w_ab = jnp.pad(w_t[:, D_MAIN:, :], ((0, 0), (0, pad), (0, 0))).astype(BF16)
    wg = jnp.concatenate([bdiag(rg_w_a), bdiag(rg_w_x)], axis=-1).astype(BF16)
    bg = jnp.concatenate([rg_b_a, rg_b_x], axis=-1)[:, None, :].astype(F32)
    row = lambda a: a[:, None, :].astype(F32)
    padh = lambda a: jnp.pad(a.astype(F32), ((0, 0), (0, LANES - HEADS)))[:, None, :]
    return (w_main, w_ab, wg, bg, row(rg_lambda), rg_conv_w.astype(F32), row(rg_conv_b),
            row(ret_gn_w), row(ret_gn_b), gdn_conv_w.astype(F32), padh(gdn_a_log), padh(gdn_dt_bias),
            row(gdn_norm_w), w_out.astype(BF16), row(ln_w), row(ln_b))


def _sample_kernel(
        x_ref, cos_ref, sin_ref, gam_ref,
        h0_ref, rgb0_ref, sret0_ref, gb0_ref, sgdn0_ref,
        win_ref, wab_ref, wg_ref, bg_ref, lam_ref, rcw_ref, rcb_ref, gnw_ref, gnb_ref,
        gcw_ref, alog_ref, dtb_ref, gnorm_ref, wout_ref, lnw_ref, lnb_ref,
        y_ref, h_ref, rgb_ref, sret_ref, gb_ref, sgdn_ref,
        xcur_ref, u_ref, eg_ref, beta_ref, mix_ref,
        *, bb_rows, n_bb):
    layer = pl.program_id(0)
    bb = pl.program_id(1)

    @pl.when(jnp.logical_and(layer == 0, bb == 0))
    def _load_x():
        xcur_ref[...] = x_ref[...]

    @pl.when(bb == 0)
    def _project():
        xb = xcur_ref[...].astype(BF16)
        u_ref[...] = _bdot_nt(xb, win_ref[...])
        ab = _bdot_nt(xb, wab_ref[...])
        eg_ref[...] = jnp.exp(-jnp.exp(alog_ref[...]) * jax.nn.softplus(ab + dtb_ref[...]))
        beta_ref[...] = jax.nn.sigmoid(ab)

        cur = u_ref[:, C_RGX:C_RGX + W]
        xc = rcb_ref[...] + rcw_ref[CONV_W - 1:CONV_W, :] * cur
        for k in range(CONV_W - 1):
            xc = xc + rcw_ref[k:k + 1, :] * rgb0_ref[k]
        rgb_ref[0] = rgb0_ref[1]
        rgb_ref[1] = rgb0_ref[2]
        rgb_ref[2] = cur
        gates = jnp.dot(xc.astype(BF16), wg_ref[...], preferred_element_type=F32) + bg_ref[...]
        r = jax.nn.sigmoid(gates[:, :W])
        i = jax.nn.sigmoid(gates[:, W:])
        log_a = (-RG_C) * r * jax.nn.softplus(-lam_ref[...])
        a = jnp.exp(log_a)
        h = a * h0_ref[...] + _sqrt01(1.0 - a * a) * (i * xc)
        h_ref[...] = h
        mix_ref[:, 0:W] = h * _silu(u_ref[:, C_RGZ:C_RGZ + W])

        cosf = cos_ref[...]
        sinf = sin_ref[...]
        for hd in range(HEADS):
            qs = slice(C_RQ + hd * DK, C_RQ + (hd + 1) * DK)
            ks = slice(C_RK + hd * DK, C_RK + (hd + 1) * DK)
            q = u_ref[:, qs]
            k = u_ref[:, ks]
            u_ref[:, qs] = q * cosf + pltpu.roll(q, DK // 2, axis=1) * sinf
            u_ref[:, ks] = (k * cosf + pltpu.roll(k, DK // 2, axis=1) * sinf) * (DK ** -0.5)

        for j in range(3 * W // LANES):
            cs = slice(C_GQ + j * LANES, C_GQ + (j + 1) * LANES)
            ws = slice(j * LANES, (j + 1) * LANES)
            cur = u_ref[:, cs]
            acc = gcw_ref[CONV_W - 1:CONV_W, ws] * cur
            for k in range(CONV_W - 1):
                acc = acc + gcw_ref[k:k + 1, ws] * gb0_ref[k, :, ws]
            gb_ref[0, :, ws] = gb0_ref[1, :, ws]
            gb_ref[1, :, ws] = gb0_ref[2, :, ws]
            gb_ref[2, :, ws] = cur
            y = _silu(acc)
            if j < 2 * HEADS:
                y = y * lax.rsqrt(jnp.sum(y * y, axis=-1, keepdims=True) + EPS)
                if j < HEADS:
                    y = y * (DK ** -0.5)
            u_ref[:, cs] = y

    r0 = pl.multiple_of(bb * bb_rows, SUBLANES)
    rows = pl.ds(r0, bb_rows)
    egb = eg_ref[rows, :]
    btb = beta_ref[rows, :]
    rid = lax.broadcasted_iota(jnp.int32, (bb_rows, DK), 0)
    rid2 = lax.broadcasted_iota(jnp.int32, (2 * bb_rows, DK), 0)

    def own_rows(x):
        return jnp.concatenate([jnp.where(rid == i, x, 0.0) for i in range(bb_rows)], axis=1)

    for hd in range(HEADS):
        q = u_ref[rows, C_RQ + hd * DK:C_RQ + (hd + 1) * DK]
        k = u_ref[rows, C_RK + hd * DK:C_RK + (hd + 1) * DK]
        v = u_ref[rows, C_RV + hd * DK:C_RV + (hd + 1) * DK]
        z = u_ref[rows, C_RZ + hd * DK:C_RZ + (hd + 1) * DK]
        qk = jnp.sum(q * k, axis=-1, keepdims=True)
        gam = gam_ref[hd]
        qs_ = jnp.zeros((bb_rows, DK), F32)
        for i in range(bb_rows):
            qs_ = jnp.where(rid == i, _bdot(q, sret0_ref[i, hd]), qs_)
        o = qk * v + gam * qs_
        kv = _bdot_tn(k, own_rows(v))
        for i in range(bb_rows):
            sret_ref[i, hd] = gam * sret0_ref[i, hd] + kv[:, i * DK:(i + 1) * DK]
        mu = jnp.mean(o, axis=-1, keepdims=True)
        d = o - mu
        on = d * lax.rsqrt(jnp.mean(d * d, axis=-1, keepdims=True) + EPS)
        gs = slice(hd * DK, (hd + 1) * DK)
        mix_ref[rows, W + hd * DK:W + (hd + 1) * DK] = (on * gnw_ref[:, gs] + gnb_ref[:, gs]) * _silu(z)

        q = u_ref[rows, C_GQ + hd * DK:C_GQ + (hd + 1) * DK]
        k = u_ref[rows, C_GK + hd * DK:C_GK + (hd + 1) * DK]
        v = u_ref[rows, C_GV + hd * DK:C_GV + (hd + 1) * DK]
        z = u_ref[rows, C_GZ + hd * DK:C_GZ + (hd + 1) * DK]
        qk = jnp.sum(q * k, axis=-1, keepdims=True)
        eg = jnp.broadcast_to(egb[:, hd:hd + 1], (bb_rows, DK))
        beta = jnp.broadcast_to(btb[:, HEADS + hd:HEADS + hd + 1], (bb_rows, DK))
        kq = jnp.concatenate([k, q], axis=0)
        kqs = jnp.zeros((2 * bb_rows, DK), F32)
        for i in range(bb_rows):
            kqs = jnp.where(rid2 % bb_rows == i, _bdot(kq, sgdn0_ref[i, hd]), kqs)
        v_new = beta * (v - eg * kqs[:bb_rows])
        o = eg * kqs[bb_rows:] + qk * v_new
        kv = _bdot_tn(k, own_rows(v_new))
        for i in range(bb_rows):
            sgdn_ref[i, hd] = sgdn0_ref[i, hd] * eg[i:i + 1, :] + kv[:, i * DK:(i + 1) * DK]
        on = o * lax.rsqrt(jnp.mean(o * o, axis=-1, keepdims=True) + EPS) * gnorm_ref[...]
        mix_ref[rows, 2 * W + hd * DK:2 * W + (hd + 1) * DK] = on * _silu(z)

    @pl.when(bb == n_bb - 1)
    def _finish():
        out = jnp.dot(mix_ref[...].astype(BF16), wout_ref[...], preferred_element_type=F32)
        r = ALPHA * xcur_ref[...] + out
        mu = jnp.mean(r, axis=-1, keepdims=True)
        d = r - mu
        var = jnp.mean(d * d, axis=-1, keepdims=True)
        y = d * lax.rsqrt(var + EPS) * lnw_ref[...] + lnb_ref[...]
        xcur_ref[...] = y
        y_ref[...] = y


def _sample_path(x, h0, rgb0, sret0, gb0, sgdn0, wts, *, bb_rows):
    nb = x.shape[0]
    n_bb = nb // bb_rows
    pos = jnp.arange(1, dtype=F32) + float(PAST_LEN)
    cos2, sin2 = _rope_tables(pos)
    lg = jnp.log1p(-jnp.exp2(-5.0 - jnp.arange(HEADS, dtype=F32)))
    gam = jnp.broadcast_to(jnp.exp(lg)[:, None, None], (HEADS, 1, DK))

    def const(a):
        nd = a.ndim
        return pl.BlockSpec(a.shape, lambda l, b, _n=nd: (0,) * _n)

    def per_layer(a, prefetch=False):
        nd = a.ndim - 1
        return pl.BlockSpec((None,) + a.shape[1:], lambda l, b, _n=nd: (l,) + (0,) * _n,
                            pipeline_mode=pl.Buffered(2 if prefetch else 1))

    def per_block(a):
        return pl.BlockSpec((None, bb_rows) + a.shape[2:], lambda l, b: (l, b, 0, 0, 0))

    ins = (x, cos2, sin2, gam, h0, rgb0, sret0, gb0, sgdn0) + tuple(wts)
    in_specs = ([const(x), const(cos2), const(sin2), const(gam),
                 per_layer(h0), per_layer(rgb0), per_block(sret0), per_layer(gb0), per_block(sgdn0)]
                + [per_layer(a, prefetch=(i == 0)) for i, a in enumerate(wts)])
    out_shape = (jax.ShapeDtypeStruct(x.shape, F32),
                 jax.ShapeDtypeStruct(h0.shape, F32), jax.ShapeDtypeStruct(rgb0.shape, F32),
                 jax.ShapeDtypeStruct(sret0.shape, F32), jax.ShapeDtypeStruct(gb0.shape, F32),
                 jax.ShapeDtypeStruct(sgdn0.shape, F32))
    out_specs = (const(x), per_layer(h0), per_layer(rgb0), per_block(sret0), per_layer(gb0), per_block(sgdn0))
    scratch = [pltpu.VMEM((nb, D_MODEL), F32),
               pltpu.VMEM((nb, D_MAIN), F32),
               pltpu.VMEM((nb, LANES), F32),
               pltpu.VMEM((nb, LANES), F32),
               pltpu.VMEM((nb, 3 * W), F32)]
    kern = functools.partial(_sample_kernel, bb_rows=bb_rows, n_bb=n_bb)
    return pl.pallas_call(
        kern, grid=(DEPTH, n_bb), in_specs=in_specs, out_specs=out_specs, out_shape=out_shape,
        scratch_shapes=scratch,
        compiler_params=pltpu.CompilerParams(dimension_semantics=("arbitrary", "arbitrary"),
                                             vmem_limit_bytes=58 * 1024 * 1024),
        name="sample_path",
    )(*ins)


def kernel(x_prompt, x_sample, state_rglru_h, state_rglru_conv, state_ret, state_gdn_conv, state_gdn,
           meta_tokens, w_in, rg_conv_w, rg_conv_b, rg_w_a, rg_b_a, rg_w_x, rg_b_x, rg_lambda,
           ret_gn_w, ret_gn_b, gdn_conv_w, gdn_a_log, gdn_dt_bias, gdn_norm_w, w_out, ln_w, ln_b):
    bp, seq, _ = x_prompt.shape
    nb = x_sample.shape[0]
    wts = _prep_weights(w_in, rg_conv_w, rg_conv_b, rg_w_a, rg_b_a, rg_w_x, rg_b_x, rg_lambda,
                        ret_gn_w, ret_gn_b, gdn_conv_w, gdn_a_log, gdn_dt_bias, gdn_norm_w, w_out, ln_w, ln_b)

    pos = jnp.arange(N_META + seq, dtype=F32)
    zeros = lambda *s: jnp.zeros(s, F32)
    init0 = (zeros(1, 1, W), zeros(1, CONV_W - 1, W), zeros(1, HEADS, DK, DK),
             zeros(1, CONV_W - 1, 3 * W), zeros(1, HEADS, DK, DK))
    st_meta = _meta_prefix(meta_tokens.astype(x_prompt.dtype)[None], pos[:N_META], init0, wts)
    xp = x_prompt
    new_p = [[] for _ in range(5)]
    for l in range(DEPTH):
        st_m = tuple(a[l:l + 1] for a in st_meta)
        xp, *st_p = _prompt_layer(l, xp, pos[N_META:], st_m, wts,
                                  tb=PROMPT_BLOCK, c_ret=RET_CHUNK, c_gdn=GDN_CHUNK)
        for j in range(5):
            new_p[j].append(st_p[j])
    sp = [jnp.stack(a) for a in new_p]
    sp[0] = sp[0].reshape(DEPTH, bp, W)

    tap_major = lambda a: jnp.transpose(a, (0, 2, 1, 3))
    ys, sh, srgb, sret, sgb, sgdn = _sample_path(
        x_sample.reshape(nb, D_MODEL), state_rglru_h, tap_major(state_rglru_conv), state_ret,
        tap_major(state_gdn_conv), state_gdn, wts, bb_rows=SAMPLE_BLOCK)
    return (xp, ys.reshape(x_sample.shape), sp[0], sp[1], sp[2], sp[3], sp[4],
            sh, tap_major(srgb), sret, tap_major(sgb), sgdn)
```

```python
import functools
import math

import jax
import jax.numpy as jnp
from jax import lax
from jax.experimental import pallas as pl
from jax.experimental.pallas import tpu as pltpu

F32 = jnp.float32
BF16 = jnp.bfloat16

D_MODEL = 1024
DEPTH = 4
N_META = 16
PAST_LEN = 16384
W = 512
CONV_W = 4
RG_BLOCKS = 8
RG_C = 8.0
HEADS = 4
DK = W // HEADS
ROPE_BASE = 10000.0
EPS = 1e-6
ALPHA = (2.0 * DEPTH) ** 0.25
D_MAIN = 10 * W
LANES = 128
SUBLANES = 8
N_CONV = 4 * W

PROMPT_BLOCK = 512
RET_CHUNK = 256
GDN_CHUNK = 64
SAMPLE_BLOCK = SUBLANES
MIB = 1024 * 1024
VMEM_LIMIT_PROMPT = 56 * MIB
VMEM_LIMIT_SAMPLE = 58 * MIB
VMEM_LIMIT_META = 48 * MIB
VMEM_LIMIT_CAST = 32 * MIB

C_RGX, C_RGZ, C_RQ, C_RK, C_RV, C_RZ, C_GQ, C_GK, C_GV, C_GZ = (i * W for i in range(10))


def _bdot(a, b):
    return jnp.dot(a.astype(BF16), b.astype(BF16), preferred_element_type=F32)


def _bdot_nt(a, b):
    return lax.dot_general(a.astype(BF16), b.astype(BF16), (((1,), (1,)), ((), ())), preferred_element_type=F32)


def _bdot_tn(a, b):
    return lax.dot_general(a.astype(BF16), b.astype(BF16), (((0,), (0,)), ((), ())), preferred_element_type=F32)


def _bmm(a, b):
    return jnp.einsum('nij,njk->nik', a.astype(BF16), b.astype(BF16), preferred_element_type=F32)


def _bmm_nt(a, b):
    return jnp.einsum('nik,njk->nij', a.astype(BF16), b.astype(BF16), preferred_element_type=F32)


def _silu(x):
    return x * jax.nn.sigmoid(x)


def _sqrt01(y):
    return y * lax.rsqrt(jnp.maximum(y, 1e-30))


def _scan_rows(a, b, h0):
    n, lanes = a.shape
    g = n // SUBLANES
    a3 = a.reshape(g, SUBLANES, lanes)
    b3 = b.reshape(g, SUBLANES, lanes)
    sub = lax.broadcasted_iota(jnp.int32, a3.shape, 1)
    s = 1
    while s < SUBLANES:
        m = sub >= s
        a_s = pltpu.roll(a3, s, axis=1)
        b_s = pltpu.roll(b3, s, axis=1)
        b3 = jnp.where(m, a3 * b_s + b3, b3)
        a3 = jnp.where(m, a3 * a_s, a3)
        s *= 2
    carry = h0
    hs = []
    for i in range(g):
        hi = a3[i] * carry + b3[i]
        carry = hi[SUBLANES - 1:SUBLANES, :]
        hs.append(hi)
    return jnp.concatenate(hs, axis=0), carry


def _causal_conv_strip(x, hist, taps):
    assert CONV_W == 4
    r8 = lax.broadcasted_iota(jnp.int32, hist.shape, 0)

    def shift(v, prev, d):
        vs = pltpu.roll(v, d, axis=0)
        top = jnp.where(r8 < d, pltpu.roll(prev, d, axis=0), vs[0:SUBLANES])
        return jnp.concatenate([top, vs[SUBLANES:]], axis=0)

    w0, w1, w2, w3 = taps
    x1 = shift(x, hist, 1)
    z = w1 * x + w0 * x1
    z_hist = w1 * hist + w0 * pltpu.roll(hist, 1, axis=0)
    return w3 * x + w2 * x1 + shift(z, z_hist, 2)


def _prompt_layer_kernel(
        x_ref, cos_ref, sin_ref,
        h0_ref, rgb0_ref, sret0_ref, gb0_ref, sgdn0_ref,
        win_ref, wab_ref, wg_ref, bg_ref, lam_ref, rcw_ref, rcb_ref, gnw_ref, gnb_ref,
        gcw_ref, alog_ref, dtb_ref, gnorm_ref, wout_ref, lnw_ref, lnb_ref,
        rdecay_ref, rqdec_ref, rkdec_ref, rsdec_ref,
        y_ref, h_ref, rgb_ref, sret_ref, gb_ref, sgdn_ref,
        u_ref, hist_ref, xc_ref, mix_ref,
        *, tb, c_ret, c_gdn):
    t = pl.program_id(1)

    @pl.when(t == 0)
    def _init():
        h_ref[...] = h0_ref[...]
        sret_ref[...] = sret0_ref[...]
        sgdn_ref[...] = sgdn0_ref[...]
        hist_ref[SUBLANES - 3:SUBLANES, 0:W] = rgb0_ref[...]
        hist_ref[SUBLANES - 3:SUBLANES, W:N_CONV] = gb0_ref[...]

    x = x_ref[...]
    xb = x.astype(BF16)

    def project(c0, c1):
        u_ref[:, c0:c1] = _bdot_nt(xb, win_ref[c0:c1, :])

    project(C_RGX, C_RQ)
    ab = _bdot_nt(xb, wab_ref[...])
    project(C_GQ, D_MAIN)

    for j in range(W // LANES):
        cs = slice(C_RGX + j * LANES, C_RGX + (j + 1) * LANES)
        ws = slice(j * LANES, (j + 1) * LANES)
        xj = u_ref[:, cs]
        taps = [rcw_ref[k:k + 1, ws] for k in range(CONV_W)]
        xc_ref[:, ws] = _causal_conv_strip(xj, hist_ref[:, ws], taps) + rcb_ref[:, ws]
        hist_ref[:, ws] = xj[tb - SUBLANES:tb, :]
        rgb_ref[:, ws] = xj[tb - 3:tb, :]
    gates = jnp.dot(xc_ref[...].astype(BF16), wg_ref[...], preferred_element_type=F32) + bg_ref[...]
    for j in range(W // LANES):
        zs = slice(C_RGZ + j * LANES, C_RGZ + (j + 1) * LANES)
        ws = slice(j * LANES, (j + 1) * LANES)
        r = jax.nn.sigmoid(gates[:, j * LANES:(j + 1) * LANES])
        i = jax.nn.sigmoid(gates[:, W + j * LANES:W + (j + 1) * LANES])
        log_a = (-RG_C) * r * jax.nn.softplus(-lam_ref[:, ws])
        a = jnp.exp(log_a)
        b = _sqrt01(1.0 - a * a) * (i * xc_ref[:, ws])
        h, h_last = _scan_rows(a, b, h_ref[:, ws])
        h_ref[:, ws] = h_last
        mix_ref[:, ws] = h * _silu(u_ref[:, zs])

    project(C_RQ, C_GQ)
    cosf = cos_ref[...]
    sinf = sin_ref[...]
    for hd in range(HEADS):
        q = u_ref[:, C_RQ + hd * DK:C_RQ + (hd + 1) * DK]
        k = u_ref[:, C_RK + hd * DK:C_RK + (hd + 1) * DK]
        v = u_ref[:, C_RV + hd * DK:C_RV + (hd + 1) * DK]
        z = u_ref[:, C_RZ + hd * DK:C_RZ + (hd + 1) * DK]
        q = q * cosf + pltpu.roll(q, DK // 2, axis=1) * sinf
        k = (k * cosf + pltpu.roll(k, DK // 2, axis=1) * sinf) * (DK ** -0.5)
        s = sret_ref[hd]
        outs = []
        for c in range(tb // c_ret):
            rs = slice(c * c_ret, (c + 1) * c_ret)
            qc, kc, vc = q[rs], k[rs], v[rs]
            sc = _bdot_nt(qc, kc) * rdecay_ref[hd]
            outs.append(_bdot(sc, vc) + _bdot(qc * rqdec_ref[hd], s))
            s = s * rsdec_ref[hd] + _bdot_tn(kc * rkdec_ref[hd], vc)
        sret_ref[hd] = s
        o = outs[0] if len(outs) == 1 else jnp.concatenate(outs, axis=0)
        mu = jnp.mean(o, axis=-1, keepdims=True)
        d = o - mu
        on = d * lax.rsqrt(jnp.mean(d * d, axis=-1, keepdims=True) + EPS)
        gs = slice(hd * DK, (hd + 1) * DK)
        mix_ref[:, W + hd * DK:W + (hd + 1) * DK] = (on * gnw_ref[:, gs] + gnb_ref[:, gs]) * _silu(z)

    for j in range(3 * W // LANES):
        cs = slice(C_GQ + j * LANES, C_GQ + (j + 1) * LANES)
        ws = slice(j * LANES, (j + 1) * LANES)
        hs = slice(W + j * LANES, W + (j + 1) * LANES)
        xj = u_ref[:, cs]
        taps = [gcw_ref[k:k + 1, ws] for k in range(CONV_W)]
        u_ref[:, cs] = _silu(_causal_conv_strip(xj, hist_ref[:, hs], taps))
        hist_ref[:, hs] = xj[tb - SUBLANES:tb, :]
        gb_ref[:, ws] = xj[tb - 3:tb, :]

    nc = tb // c_gdn
    glog = -jnp.exp(alog_ref[...]) * jax.nn.softplus(ab + dtb_ref[...])
    beta_all = jax.nn.sigmoid(ab)
    in_chunk = lax.broadcasted_iota(jnp.int32, (tb, LANES), 0) % c_gdn
    gcs = glog
    sh = 1
    while sh < c_gdn:
        gcs = gcs + jnp.where(in_chunk >= sh, pltpu.roll(gcs, sh, axis=0), 0.0)
        sh *= 2

    ri = lax.broadcasted_iota(jnp.int32, (c_gdn, c_gdn), 0)
    ci = lax.broadcasted_iota(jnp.int32, (c_gdn, c_gdn), 1)
    tril = ri >= ci
    strict = ri > ci
    eye_f = jnp.where(ri == ci, 1.0, 0.0).astype(F32)
    n_lv = int(math.ceil(math.log2(c_gdn)))

    names = ("q", "k", "kb", "vb", "kbe", "qe", "kd", "gl", "dec")
    parts = {n: [[None] * HEADS for _ in range(nc)] for n in names}
    for hd in range(HEADS):
        q = u_ref[:, C_GQ + hd * DK:C_GQ + (hd + 1) * DK]
        k = u_ref[:, C_GK + hd * DK:C_GK + (hd + 1) * DK]
        v = u_ref[:, C_GV + hd * DK:C_GV + (hd + 1) * DK]
        q = q * lax.rsqrt(jnp.sum(q * q, axis=-1, keepdims=True) + EPS) * (DK ** -0.5)
        k = k * lax.rsqrt(jnp.sum(k * k, axis=-1, keepdims=True) + EPS)
        gc = jnp.broadcast_to(gcs[:, hd:hd + 1], (tb, LANES))
        beta = jnp.broadcast_to(beta_all[:, HEADS + hd:HEADS + hd + 1], (tb, LANES))
        egc = jnp.exp(gc)
        kb = k * beta
        vb = v * beta
        kbe = kb * egc
        qe = q * egc
        for c in range(nc):
            rs = slice(c * c_gdn, (c + 1) * c_gdn)
            gcc = gc[rs]
            gl = gcc[c_gdn - 1:c_gdn, :]
            diff = gcc[:, :c_gdn] - gcc.T[:c_gdn, :]
            parts["dec"][c][hd] = jnp.where(tril, jnp.exp(jnp.where(tril, diff, 0.0)), 0.0)
            parts["gl"][c][hd] = gl
            parts["kd"][c][hd] = k[rs] * jnp.exp(gl - gcc)
            for n, val in (("q", q), ("k", k), ("kb", kb), ("vb", vb), ("kbe", kbe), ("qe", qe)):
                parts[n][c][hd] = val[rs]
    st = {n: jnp.stack([parts[n][c][hd] for c in range(nc) for hd in range(HEADS)]) for n in names}

    a_low = jnp.where(strict, _bmm_nt(st["kb"], st["k"]) * st["dec"], 0.0)
    att = _bmm_nt(st["q"], st["k"]) * st["dec"]
    pt = jnp.concatenate([-a_low, jnp.broadcast_to(eye_f, a_low.shape)], axis=-1)
    right = lax.broadcasted_iota(jnp.int32, (c_gdn, 2 * c_gdn), 1) >= c_gdn
    for _ in range(n_lv):
        ptb = pt.astype(BF16)
        r = jnp.einsum('nij,njk->nik', ptb[..., :c_gdn], ptb, preferred_element_type=F32)
        pt = r + jnp.where(right, pt, 0.0)
    tinv = pt[..., c_gdn:]
    sol = _bmm(tinv, jnp.concatenate([st["vb"], st["kbe"]], axis=-1))
    uu, ww = sol[..., :DK], sol[..., DK:]

    s = sgdn_ref[...]
    outs = []
    for c in range(nc):
        hs4 = slice(c * HEADS, (c + 1) * HEADS)
        ws_ = _bmm(jnp.concatenate([ww[hs4], st["qe"][hs4]], axis=1), s)
        v_new = uu[hs4] - ws_[:, :c_gdn]
        outs.append(ws_[:, c_gdn:] + _bmm(att[hs4], v_new))
        kv = jnp.einsum('hik,hiv->hkv', st["kd"][hs4].astype(BF16), v_new.astype(BF16),
                        preferred_element_type=F32)
        s = s * jnp.exp(st["gl"][hs4]) + kv
    sgdn_ref[...] = s
    for hd in range(HEADS):
        z = u_ref[:, C_GZ + hd * DK:C_GZ + (hd + 1) * DK]
        o = outs[0][hd] if nc == 1 else jnp.concatenate([outs[c][hd] for c in range(nc)], axis=0)
        on = o * lax.rsqrt(jnp.mean(o * o, axis=-1, keepdims=True) + EPS) * gnorm_ref[...]
        mix_ref[:, 2 * W + hd * DK:2 * W + (hd + 1) * DK] = on * _silu(z)

    out = jnp.dot(mix_ref[...].astype(BF16), wout_ref[...], preferred_element_type=F32)
    r = ALPHA * x + out
    mu = jnp.mean(r, axis=-1, keepdims=True)
    d = r - mu
    var = jnp.mean(d * d, axis=-1, keepdims=True)
    y_ref[...] = d * lax.rsqrt(var + EPS) * lnw_ref[...] + lnb_ref[...]


def _ret_tables(c):
    lg = jnp.log1p(-jnp.exp2(-5.0 - jnp.arange(HEADS, dtype=F32)))[:, None, None]
    idx = jnp.arange(c, dtype=F32)
    diff = idx[:, None] - idx[None, :]
    decay = jnp.where(diff >= 0, jnp.exp(lg * jnp.maximum(diff, 0.0)), 0.0)
    q_dec = jnp.broadcast_to(jnp.exp(lg[:, 0] * (idx + 1.0))[:, :, None], (HEADS, c, DK))
    k_dec = jnp.broadcast_to(jnp.exp(lg[:, 0] * (c - 1.0 - idx))[:, :, None], (HEADS, c, DK))
    s_dec = jnp.broadcast_to(jnp.exp(lg * c), (HEADS, 1, DK))
    return decay, q_dec, k_dec, s_dec


def _rope_tables(pos):
    half = DK // 2
    inv = ROPE_BASE ** (-jnp.arange(half, dtype=F32) / half)
    ang = pos[:, None] * inv[None, :]
    cos, sin = jnp.cos(ang), jnp.sin(ang)
    return jnp.concatenate([cos, cos], -1), jnp.concatenate([-sin, sin], -1)


def _prompt_layer(layer, x, pos, init, wts, *, tb, c_ret, c_gdn):
    bsz, tlen, _ = x.shape
    assert tlen % tb == 0 and tb % c_ret == 0 and tb % c_gdn == 0 and tb % SUBLANES == 0
    nt = tlen // tb
    cos2, sin2 = _rope_tables(pos)
    rtabs = _ret_tables(c_ret)

    def wspec(a):
        nd = a.ndim - 1
        return pl.BlockSpec((None,) + a.shape[1:], lambda b, t, _n=nd: (layer,) + (0,) * _n,
                            pipeline_mode=pl.Buffered(1))

    def cspec(a):
        nd = a.ndim
        return pl.BlockSpec(a.shape, lambda b, t, _n=nd: (0,) * _n, pipeline_mode=pl.Buffered(1))

    def ispec(a):
        nd = a.ndim - 1
        return pl.BlockSpec((None,) + a.shape[1:], lambda b, t, _n=nd: (0,) * (_n + 1),
                            pipeline_mode=pl.Buffered(1))

    def ospec(shape):
        nd = len(shape)
        return pl.BlockSpec((None,) + shape, lambda b, t, _n=nd: (b,) + (0,) * _n)

    in_specs = ([pl.BlockSpec((None, tb, D_MODEL), lambda b, t: (b, t, 0)),
                 pl.BlockSpec((tb, DK), lambda b, t: (t, 0)),
                 pl.BlockSpec((tb, DK), lambda b, t: (t, 0))]
                + [ispec(a) for a in init] + [wspec(a) for a in wts] + [cspec(a) for a in rtabs])
    out_shape = (jax.ShapeDtypeStruct((bsz, tlen, D_MODEL), F32),
                 jax.ShapeDtypeStruct((bsz, 1, W), F32),
                 jax.ShapeDtypeStruct((bsz, CONV_W - 1, W), F32),
                 jax.ShapeDtypeStruct((bsz, HEADS, DK, DK), F32),
                 jax.ShapeDtypeStruct((bsz, CONV_W - 1, 3 * W), F32),
                 jax.ShapeDtypeStruct((bsz, HEADS, DK, DK), F32))
    out_specs = (pl.BlockSpec((None, tb, D_MODEL), lambda b, t: (b, t, 0)),
                 ospec((1, W)), ospec((CONV_W - 1, W)), ospec((HEADS, DK, DK)),
                 ospec((CONV_W - 1, 3 * W)), ospec((HEADS, DK, DK)))
    scratch = [pltpu.VMEM((tb, D_MAIN), F32),
               pltpu.VMEM((SUBLANES, N_CONV), F32),
               pltpu.VMEM((tb, W), F32),
               pltpu.VMEM((tb, 3 * W), F32)]
    kern = functools.partial(_prompt_layer_kernel, tb=tb, c_ret=c_ret, c_gdn=c_gdn)
    return pl.pallas_call(
        kern, grid=(bsz, nt), in_specs=in_specs, out_specs=out_specs, out_shape=out_shape,
        scratch_shapes=scratch,
        compiler_params=pltpu.CompilerParams(dimension_semantics=("arbitrary", "arbitrary"),
                                             vmem_limit_bytes=VMEM_LIMIT_PROMPT),
        name=f"prompt_layer{layer}_t{tlen}",
    )(x, cos2, sin2, *init, *wts, *rtabs)


def _meta_kernel(*refs, n_tok):
    x_ref, body, y_ref, xcarry_ref = refs[0], refs[1:-1], refs[28], refs[-1]

    @pl.when(pl.program_id(0) == 0)
    def _first_layer():
        xcarry_ref[...] = x_ref[...]

    _prompt_layer_kernel(xcarry_ref, *body, tb=n_tok, c_ret=n_tok, c_gdn=n_tok)
    xcarry_ref[...] = y_ref[...]


def _meta_prefix(x, pos, init, wts):
    _, n_tok, _ = x.shape
    depth = wts[0].shape[0]
    cos2, sin2 = _rope_tables(pos)
    rtabs = _ret_tables(n_tok)
    const = lambda a: pl.BlockSpec(a.shape, lambda l, t, _n=a.ndim: (0,) * _n)
    squeeze0 = lambda a: pl.BlockSpec((None,) + a.shape[1:], lambda l, t, _n=a.ndim - 1: (0,) * (_n + 1))
    per_layer = lambda shape: pl.BlockSpec((None,) + shape, lambda l, t, _n=len(shape): (l,) + (0,) * _n)
    in_specs = ([squeeze0(x), const(cos2), const(sin2)] + [squeeze0(a) for a in init]
                + [per_layer(a.shape[1:]) for a in wts] + [const(a) for a in rtabs])
    st_shapes = ((1, W), (CONV_W - 1, W), (HEADS, DK, DK), (CONV_W - 1, 3 * W), (HEADS, DK, DK))
    out_shape = tuple(jax.ShapeDtypeStruct((depth,) + s, F32) for s in ((n_tok, D_MODEL),) + st_shapes)
    out_specs = tuple(per_layer(s) for s in ((n_tok, D_MODEL),) + st_shapes)
    scratch = [pltpu.VMEM((n_tok, D_MAIN), F32), pltpu.VMEM((SUBLANES, N_CONV), F32),
               pltpu.VMEM((n_tok, W), F32), pltpu.VMEM((n_tok, 3 * W), F32),
               pltpu.VMEM((n_tok, D_MODEL), F32)]
    outs = pl.pallas_call(
        functools.partial(_meta_kernel, n_tok=n_tok), grid=(depth, 1),
        in_specs=in_specs, out_specs=out_specs, out_shape=out_shape, scratch_shapes=scratch,
        compiler_params=pltpu.CompilerParams(dimension_semantics=("arbitrary", "arbitrary"),
                                             vmem_limit_bytes=VMEM_LIMIT_META),
        name="meta_prefix",
    )(x, cos2, sin2, *init, *wts, *rtabs)
    return outs[1:]


def _cast_kernel(x_ref, o_ref):
    o_ref[...] = x_ref[...].astype(o_ref.dtype)


def _cast_main_rows(w_t):
    depth, _, d_model = w_t.shape
    rows = D_MAIN // 4
    return pl.pallas_call(
        _cast_kernel, grid=(depth, D_MAIN // rows),
        in_specs=[pl.BlockSpec((None, rows, d_model), lambda l, j: (l, j, 0))],
        out_specs=pl.BlockSpec((None, rows, d_model), lambda l, j: (l, j, 0)),
        out_shape=jax.ShapeDtypeStruct((depth, D_MAIN, d_model), BF16),
        compiler_params=pltpu.CompilerParams(vmem_limit_bytes=VMEM_LIMIT_CAST),
        name="cast_w_in",
    )(w_t)


def _prep_weights(w_in, rg_conv_w, rg_conv_b, rg_w_a, rg_b_a, rg_w_x, rg_b_x, rg_lambda,
                  ret_gn_w, ret_gn_b, gdn_conv_w, gdn_a_log, gdn_dt_bias, gdn_norm_w, w_out, ln_w, ln_b):
    eye = jnp.eye(RG_BLOCKS, dtype=F32)

    def bdiag(w):
        l, n, c, d = w.shape
        return jnp.einsum('lncd,nm->lncmd', w.astype(F32), eye).reshape(l, n * c, n * d)

    pad = LANES - 2 * HEADS
    w_t = jnp.transpose(w_in, (0, 2, 1))
    w_main = _cast_main_rows(w_t)
    w_ab = jnp.pad(w_t[:, D_MAIN:, :], ((0, 0), (0, pad), (0, 0))).astype(BF16)
    wg = jnp.concatenate([bdiag(rg_w_a), bdiag(rg_w_x)], axis=-1).astype(BF16)
    bg = jnp.concatenate([rg_b_a, rg_b_x], axis=-1)[:, None, :].astype(F32)
    row = lambda a: a[:, None, :].astype(F32)
    padh = lambda a: jnp.pad(a.astype(F32), ((0, 0), (0, LANES - HEADS)))[:, None, :]
    return (w_main, w_ab, wg, bg, row(rg_lambda), rg_conv_w.astype(F32), row(rg_conv_b),
            row(ret_gn_w), row(ret_gn_b), gdn_conv_w.astype(F32), padh(gdn_a_log), padh(gdn_dt_bias),
            row(gdn_norm_w), w_out.astype(BF16), row(ln_w), row(ln_b))


def _sample_kernel(
        x_ref, cos_ref, sin_ref, gam_ref,
        h0_ref, rgb0_ref, sret0_ref, gb0_ref, sgdn0_ref,
        win_ref, wab_ref, wg_ref, bg_ref, lam_ref, rcw_ref, rcb_ref, gnw_ref, gnb_ref,
        gcw_ref, alog_ref, dtb_ref, gnorm_ref, wout_ref, lnw_ref, lnb_ref,
        y_ref, h_ref, rgb_ref, sret_ref, gb_ref, sgdn_ref,
        xcur_ref, u_ref, eg_ref, beta_ref, mix_ref,
        *, bb_rows, n_bb):
    layer = pl.program_id(0)
    bb = pl.program_id(1)

    @pl.when(jnp.logical_and(layer == 0, bb == 0))
    def _load_x():
        xcur_ref[...] = x_ref[...]

    @pl.when(bb == 0)
    def _project():
        xb = xcur_ref[...].astype(BF16)
        u_ref[...] = _bdot_nt(xb, win_ref[...])
        ab = _bdot_nt(xb, wab_ref[...])
        eg_ref[...] = jnp.exp(-jnp.exp(alog_ref[...]) * jax.nn.softplus(ab + dtb_ref[...]))
        beta_ref[...] = jax.nn.sigmoid(ab)

        cur = u_ref[:, C_RGX:C_RGX + W]
        xc = rcb_ref[...] + rcw_ref[CONV_W - 1:CONV_W, :] * cur
        for k in range(CONV_W - 1):
            xc = xc + rcw_ref[k:k + 1, :] * rgb0_ref[k]
        rgb_ref[0] = rgb0_ref[1]
        rgb_ref[1] = rgb0_ref[2]
        rgb_ref[2] = cur
        gates = jnp.dot(xc.astype(BF16), wg_ref[...], preferred_element_type=F32) + bg_ref[...]
        r = jax.nn.sigmoid(gates[:, :W])
        i = jax.nn.sigmoid(gates[:, W:])
        log_a = (-RG_C) * r * jax.nn.softplus(-lam_ref[...])
        a = jnp.exp(log_a)
        h = a * h0_ref[...] + _sqrt01(1.0 - a * a) * (i * xc)
        h_ref[...] = h
        mix_ref[:, 0:W] = h * _silu(u_ref[:, C_RGZ:C_RGZ + W])

        cosf = cos_ref[...]
        sinf = sin_ref[...]
        for hd in range(HEADS):
            qs = slice(C_RQ + hd * DK, C_RQ + (hd + 1) * DK)
            ks = slice(C_RK + hd * DK, C_RK + (hd + 1) * DK)
            q = u_ref[:, qs]
            k = u_ref[:, ks]
            u_ref[:, qs] = q * cosf + pltpu.roll(q, DK // 2, axis=1) * sinf
            u_ref[:, ks] = (k * cosf + pltpu.roll(k, DK // 2, axis=1) * sinf) * (DK ** -0.5)

        for j in range(3 * W // LANES):
            cs = slice(C_GQ + j * LANES, C_GQ + (j + 1) * LANES)
            ws = slice(j * LANES, (j + 1) * LANES)
            cur = u_ref[:, cs]
            acc = gcw_ref[CONV_W - 1:CONV_W, ws] * cur
            for k in range(CONV_W - 1):
                acc = acc + gcw_ref[k:k + 1, ws] * gb0_ref[k, :, ws]
            gb_ref[0, :, ws] = gb0_ref[1, :, ws]
            gb_ref[1, :, ws] = gb0_ref[2, :, ws]
            gb_ref[2, :, ws] = cur
            y = _silu(acc)
            if j < 2 * HEADS:
                y = y * lax.rsqrt(jnp.sum(y * y, axis=-1, keepdims=True) + EPS)
                if j < HEADS:
                    y = y * (DK ** -0.5)
            u_ref[:, cs] = y

    r0 = pl.multiple_of(bb * bb_rows, SUBLANES)
    rows = pl.ds(r0, bb_rows)
    egb = eg_ref[rows, :]
    btb = beta_ref[rows, :]
    rid = lax.broadcasted_iota(jnp.int32, (bb_rows, DK), 0)
    rid2 = lax.broadcasted_iota(jnp.int32, (2 * bb_rows, DK), 0)

    def own_rows(x):
        return jnp.concatenate([jnp.where(rid == i, x, 0.0) for i in range(bb_rows)], axis=1)

    for hd in range(HEADS):
        q = u_ref[rows, C_RQ + hd * DK:C_RQ + (hd + 1) * DK]
        k = u_ref[rows, C_RK + hd * DK:C_RK + (hd + 1) * DK]
        v = u_ref[rows, C_RV + hd * DK:C_RV + (hd + 1) * DK]
        z = u_ref[rows, C_RZ + hd * DK:C_RZ + (hd + 1) * DK]
        qk = jnp.sum(q * k, axis=-1, keepdims=True)
        gam = gam_ref[hd]
        qs_ = jnp.zeros((bb_rows, DK), F32)
        for i in range(bb_rows):
            qs_ = jnp.where(rid == i, _bdot(q, sret0_ref[i, hd]), qs_)
        o = qk * v + gam * qs_
        kv = _bdot_tn(k, own_rows(v))
        for i in range(bb_rows):
            sret_ref[i, hd] = gam * sret0_ref[i, hd] + kv[:, i * DK:(i + 1) * DK]
        mu = jnp.mean(o, axis=-1, keepdims=True)
        d = o - mu
        on = d * lax.rsqrt(jnp.mean(d * d, axis=-1, keepdims=True) + EPS)
        gs = slice(hd * DK, (hd + 1) * DK)
        mix_ref[rows, W + hd * DK:W + (hd + 1) * DK] = (on * gnw_ref[:, gs] + gnb_ref[:, gs]) * _silu(z)

        q = u_ref[rows, C_GQ + hd * DK:C_GQ + (hd + 1) * DK]
        k = u_ref[rows, C_GK + hd * DK:C_GK + (hd + 1) * DK]
        v = u_ref[rows, C_GV + hd * DK:C_GV + (hd + 1) * DK]
        z = u_ref[rows, C_GZ + hd * DK:C_GZ + (hd + 1) * DK]
        qk = jnp.sum(q * k, axis=-1, keepdims=True)
        eg = jnp.broadcast_to(egb[:, hd:hd + 1], (bb_rows, DK))
        beta = jnp.broadcast_to(btb[:, HEADS + hd:HEADS + hd + 1], (bb_rows, DK))
        kq = jnp.concatenate([k, q], axis=0)
        kqs = jnp.zeros((2 * bb_rows, DK), F32)
        for i in range(bb_rows):
            kqs = jnp.where(rid2 % bb_rows == i, _bdot(kq, sgdn0_ref[i, hd]), kqs)
        v_new = beta * (v - eg * kqs[:bb_rows])
        o = eg * kqs[bb_rows:] + qk * v_new
        kv = _bdot_tn(k, own_rows(v_new))
        for i in range(bb_rows):
            sgdn_ref[i, hd] = sgdn0_ref[i, hd] * eg[i:i + 1, :] + kv[:, i * DK:(i + 1) * DK]
        on = o * lax.rsqrt(jnp.mean(o * o, axis=-1, keepdims=True) + EPS) * gnorm_ref[...]
        mix_ref[rows, 2 * W + hd * DK:2 * W + (hd + 1) * DK] = on * _silu(z)

    @pl.when(bb == n_bb - 1)
    def _finish():
        out = jnp.dot(mix_ref[...].astype(BF16), wout_ref[...], preferred_element_type=F32)
        r = ALPHA * xcur_ref[...] + out
        mu = jnp.mean(r, axis=-1, keepdims=True)
        d = r - mu
        var = jnp.mean(d * d, axis=-1, keepdims=True)
        y = d * lax.rsqrt(var + EPS) * lnw_ref[...] + lnb_ref[...]
        xcur_ref[...] = y
        y_ref[...] = y


def _sample_path(x, h0, rgb0, sret0, gb0, sgdn0, wts, *, bb_rows):
    nb = x.shape[0]
    n_bb = nb // bb_rows
    pos = jnp.arange(1, dtype=F32) + float(PAST_LEN)
    cos2, sin2 = _rope_tables(pos)
    lg = jnp.log1p(-jnp.exp2(-5.0 - jnp.arange(HEADS, dtype=F32)))
    gam = jnp.broadcast_to(jnp.exp(lg)[:, None, None], (HEADS, 1, DK))

    def const(a):
        nd = a.ndim
        return pl.BlockSpec(a.shape, lambda l, b, _n=nd: (0,) * _n)

    def per_layer(a, prefetch=False):
        nd = a.ndim - 1
        return pl.BlockSpec((None,) + a.shape[1:], lambda l, b, _n=nd: (l,) + (0,) * _n,
                            pipeline_mode=pl.Buffered(2 if prefetch else 1))

    def per_block(a):
        return pl.BlockSpec((None, bb_rows) + a.shape[2:], lambda l, b: (l, b, 0, 0, 0))

    ins = (x, cos2, sin2, gam, h0, rgb0, sret0, gb0, sgdn0) + tuple(wts)
    in_specs = ([const(x), const(cos2), const(sin2), const(gam),
                 per_layer(h0), per_layer(rgb0), per_block(sret0), per_layer(gb0), per_block(sgdn0)]
                + [per_layer(a, prefetch=(i == 0)) for i, a in enumerate(wts)])
    out_shape = (jax.ShapeDtypeStruct(x.shape, F32),
                 jax.ShapeDtypeStruct(h0.shape, F32), jax.ShapeDtypeStruct(rgb0.shape, F32),
                 jax.ShapeDtypeStruct(sret0.shape, F32), jax.ShapeDtypeStruct(gb0.shape, F32),
                 jax.ShapeDtypeStruct(sgdn0.shape, F32))
    out_specs = (const(x), per_layer(h0), per_layer(rgb0), per_block(sret0), per_layer(gb0), per_block(sgdn0))
    scratch = [pltpu.VMEM((nb, D_MODEL), F32),
               pltpu.VMEM((nb, D_MAIN), F32),
               pltpu.VMEM((nb, LANES), F32),
               pltpu.VMEM((nb, LANES), F32),
               pltpu.VMEM((nb, 3 * W), F32)]
    kern = functools.partial(_sample_kernel, bb_rows=bb_rows, n_bb=n_bb)
    return pl.pallas_call(
        kern, grid=(DEPTH, n_bb), in_specs=in_specs, out_specs=out_specs, out_shape=out_shape,
        scratch_shapes=scratch,
        compiler_params=pltpu.CompilerParams(dimension_semantics=("arbitrary", "arbitrary"),
                                             vmem_limit_bytes=VMEM_LIMIT_SAMPLE),
        name="sample_path",
    )(*ins)


def kernel(x_prompt, x_sample, state_rglru_h, state_rglru_conv, state_ret, state_gdn_conv, state_gdn,
           meta_tokens, w_in, rg_conv_w, rg_conv_b, rg_w_a, rg_b_a, rg_w_x, rg_b_x, rg_lambda,
           ret_gn_w, ret_gn_b, gdn_conv_w, gdn_a_log, gdn_dt_bias, gdn_norm_w, w_out, ln_w, ln_b):
    bp, seq, _ = x_prompt.shape
    nb = x_sample.shape[0]
    wts = _prep_weights(w_in, rg_conv_w, rg_conv_b, rg_w_a, rg_b_a, rg_w_x, rg_b_x, rg_lambda,
                        ret_gn_w, ret_gn_b, gdn_conv_w, gdn_a_log, gdn_dt_bias, gdn_norm_w, w_out, ln_w, ln_b)

    pos = jnp.arange(N_META + seq, dtype=F32)
    zeros = lambda *s: jnp.zeros(s, F32)
    init0 = (zeros(1, 1, W), zeros(1, CONV_W - 1, W), zeros(1, HEADS, DK, DK),
             zeros(1, CONV_W - 1, 3 * W), zeros(1, HEADS, DK, DK))
    st_meta = _meta_prefix(meta_tokens.astype(x_prompt.dtype)[None], pos[:N_META], init0, wts)
    xp = x_prompt
    new_p = [[] for _ in range(5)]
    for l in range(DEPTH):
        st_m = tuple(a[l:l + 1] for a in st_meta)
        xp, *st_p = _prompt_layer(l, xp, pos[N_META:], st_m, wts,
                                  tb=PROMPT_BLOCK, c_ret=RET_CHUNK, c_gdn=GDN_CHUNK)
        for j in range(5):
            new_p[j].append(st_p[j])
    sp = [jnp.stack(a) for a in new_p]
    sp[0] = sp[0].reshape(DEPTH, bp, W)

    tap_major = lambda a: jnp.transpose(a, (0, 2, 1, 3))
    ys, sh, srgb, sret, sgb, sgdn = _sample_path(
        x_sample.reshape(nb, D_MODEL), state_rglru_h, tap_major(state_rglru_conv), state_ret,
        tap_major(state_gdn_conv), state_gdn, wts, bb_rows=SAMPLE_BLOCK)
    return (xp, ys.reshape(x_sample.shape), sp[0], sp[1], sp[2], sp[3], sp[4],
            sh, tap_major(srgb), sret, tap_major(sgb), sgdn)
```

```python
import functools
import math

import jax
import jax.numpy as jnp
from jax import lax
from jax.experimental import pallas as pl
from jax.experimental.pallas import tpu as pltpu

F32 = jnp.float32
BF16 = jnp.bfloat16

D_MODEL = 1024
DEPTH = 4
N_META = 16
PAST_LEN = 16384
W = 512
CONV_W = 4
RG_BLOCKS = 8
RG_C = 8.0
HEADS = 4
DK = W // HEADS
ROPE_BASE = 10000.0
EPS = 1e-6
ALPHA = (2.0 * DEPTH) ** 0.25
D_MAIN = 10 * W
LANES = 128
SUBLANES = 8
N_CONV = 4 * W

PROMPT_BLOCK = 512
RET_CHUNK = 256
GDN_CHUNK = 64
SAMPLE_BLOCK = 2 * SUBLANES
MIB = 1024 * 1024
VMEM_LIMIT_PROMPT = 56 * MIB
VMEM_LIMIT_SAMPLE = 61 * MIB
VMEM_LIMIT_META = 48 * MIB
VMEM_LIMIT_CAST = 32 * MIB

C_RGX, C_RGZ, C_RQ, C_RK, C_RV, C_RZ, C_GQ, C_GK, C_GV, C_GZ = (i * W for i in range(10))


def _bdot(a, b):
    return jnp.dot(a.astype(BF16), b.astype(BF16), preferred_element_type=F32)


def _bdot_nt(a, b):
    return lax.dot_general(a.astype(BF16), b.astype(BF16), (((1,), (1,)), ((), ())), preferred_element_type=F32)


def _bdot_tn(a, b):
    return lax.dot_general(a.astype(BF16), b.astype(BF16), (((0,), (0,)), ((), ())), preferred_element_type=F32)


def _bmm(a, b):
    return jnp.einsum('nij,njk->nik', a.astype(BF16), b.astype(BF16), preferred_element_type=F32)


def _bmm_nt(a, b):
    return jnp.einsum('nik,njk->nij', a.astype(BF16), b.astype(BF16), preferred_element_type=F32)


def _silu(x):
    return x * jax.nn.sigmoid(x)


def _sqrt01(y):
    return y * lax.rsqrt(jnp.maximum(y, 1e-30))


def _scan_rows(a, b, h0):
    n, lanes = a.shape
    g = n // SUBLANES
    a3 = a.reshape(g, SUBLANES, lanes)
    b3 = b.reshape(g, SUBLANES, lanes)
    sub = lax.broadcasted_iota(jnp.int32, a3.shape, 1)
    s = 1
    while s < SUBLANES:
        m = sub >= s
        a_s = pltpu.roll(a3, s, axis=1)
        b_s = pltpu.roll(b3, s, axis=1)
        b3 = jnp.where(m, a3 * b_s + b3, b3)
        a3 = jnp.where(m, a3 * a_s, a3)
        s *= 2
    carry = h0
    hs = []
    for i in range(g):
        hi = a3[i] * carry + b3[i]
        carry = hi[SUBLANES - 1:SUBLANES, :]
        hs.append(hi)
    return jnp.concatenate(hs, axis=0), carry


def _causal_conv_strip(x, hist, taps):
    assert CONV_W == 4
    r8 = lax.broadcasted_iota(jnp.int32, hist.shape, 0)

    def shift(v, prev, d):
        vs = pltpu.roll(v, d, axis=0)
        top = jnp.where(r8 < d, pltpu.roll(prev, d, axis=0), vs[0:SUBLANES])
        return jnp.concatenate([top, vs[SUBLANES:]], axis=0)

    w0, w1, w2, w3 = taps
    x1 = shift(x, hist, 1)
    z = w1 * x + w0 * x1
    z_hist = w1 * hist + w0 * pltpu.roll(hist, 1, axis=0)
    return w3 * x + w2 * x1 + shift(z, z_hist, 2)


def _prompt_layer_kernel(
        x_ref, cos_ref, sin_ref,
        h0_ref, rgb0_ref, sret0_ref, gb0_ref, sgdn0_ref,
        win_ref, wab_ref, wg_ref, bg_ref, lam_ref, rcw_ref, rcb_ref, gnw_ref, gnb_ref,
        gcw_ref, alog_ref, dtb_ref, gnorm_ref, wout_ref, lnw_ref, lnb_ref,
        rdecay_ref, rqdec_ref, rkdec_ref, rsdec_ref,
        y_ref, h_ref, rgb_ref, sret_ref, gb_ref, sgdn_ref,
        u_ref, hist_ref, xc_ref, mix_ref,
        *, tb, c_ret, c_gdn):
    t = pl.program_id(1)

    @pl.when(t == 0)
    def _init():
        h_ref[...] = h0_ref[...]
        sret_ref[...] = sret0_ref[...]
        sgdn_ref[...] = sgdn0_ref[...]
        hist_ref[SUBLANES - 3:SUBLANES, 0:W] = rgb0_ref[...]
        hist_ref[SUBLANES - 3:SUBLANES, W:N_CONV] = gb0_ref[...]

    x = x_ref[...]
    xb = x.astype(BF16)

    def project(c0, c1):
        u_ref[:, c0:c1] = _bdot_nt(xb, win_ref[c0:c1, :])

    project(C_RGX, C_RQ)
    ab = _bdot_nt(xb, wab_ref[...])
    project(C_GQ, D_MAIN)

    for j in range(W // LANES):
        cs = slice(C_RGX + j * LANES, C_RGX + (j + 1) * LANES)
        ws = slice(j * LANES, (j + 1) * LANES)
        xj = u_ref[:, cs]
        taps = [rcw_ref[k:k + 1, ws] for k in range(CONV_W)]
        xc_ref[:, ws] = _causal_conv_strip(xj, hist_ref[:, ws], taps) + rcb_ref[:, ws]
        hist_ref[:, ws] = xj[tb - SUBLANES:tb, :]
        rgb_ref[:, ws] = xj[tb - 3:tb, :]
    gates = jnp.dot(xc_ref[...].astype(BF16), wg_ref[...], preferred_element_type=F32) + bg_ref[...]
    for j in range(W // LANES):
        zs = slice(C_RGZ + j * LANES, C_RGZ + (j + 1) * LANES)
        ws = slice(j * LANES, (j + 1) * LANES)
        r = jax.nn.sigmoid(gates[:, j * LANES:(j + 1) * LANES])
        i = jax.nn.sigmoid(gates[:, W + j * LANES:W + (j + 1) * LANES])
        log_a = (-RG_C) * r * jax.nn.softplus(-lam_ref[:, ws])
        a = jnp.exp(log_a)
        b = _sqrt01(1.0 - a * a) * (i * xc_ref[:, ws])
        h, h_last = _scan_rows(a, b, h_ref[:, ws])
        h_ref[:, ws] = h_last
        mix_ref[:, ws] = h * _silu(u_ref[:, zs])

    project(C_RQ, C_GQ)
    cosf = cos_ref[...]
    sinf = sin_ref[...]
    for hd in range(HEADS):
        q = u_ref[:, C_RQ + hd * DK:C_RQ + (hd + 1) * DK]
        k = u_ref[:, C_RK + hd * DK:C_RK + (hd + 1) * DK]
        v = u_ref[:, C_RV + hd * DK:C_RV + (hd + 1) * DK]
        z = u_ref[:, C_RZ + hd * DK:C_RZ + (hd + 1) * DK]
        q = q * cosf + pltpu.roll(q, DK // 2, axis=1) * sinf
        k = (k * cosf + pltpu.roll(k, DK // 2, axis=1) * sinf) * (DK ** -0.5)
        s = sret_ref[hd]
        outs = []
        for c in range(tb // c_ret):
            rs = slice(c * c_ret, (c + 1) * c_ret)
            qc, kc, vc = q[rs], k[rs], v[rs]
            sc = _bdot_nt(qc, kc) * rdecay_ref[hd]
            outs.append(_bdot(sc, vc) + _bdot(qc * rqdec_ref[hd], s))
            s = s * rsdec_ref[hd] + _bdot_tn(kc * rkdec_ref[hd], vc)
        sret_ref[hd] = s
        o = outs[0] if len(outs) == 1 else jnp.concatenate(outs, axis=0)
        mu = jnp.mean(o, axis=-1, keepdims=True)
        d = o - mu
        on = d * lax.rsqrt(jnp.mean(d * d, axis=-1, keepdims=True) + EPS)
        gs = slice(hd * DK, (hd + 1) * DK)
        mix_ref[:, W + hd * DK:W + (hd + 1) * DK] = (on * gnw_ref[:, gs] + gnb_ref[:, gs]) * _silu(z)

    for j in range(3 * W // LANES):
        cs = slice(C_GQ + j * LANES, C_GQ + (j + 1) * LANES)
        ws = slice(j * LANES, (j + 1) * LANES)
        hs = slice(W + j * LANES, W + (j + 1) * LANES)
        xj = u_ref[:, cs]
        taps = [gcw_ref[k:k + 1, ws] for k in range(CONV_W)]
        u_ref[:, cs] = _silu(_causal_conv_strip(xj, hist_ref[:, hs], taps))
        hist_ref[:, hs] = xj[tb - SUBLANES:tb, :]
        gb_ref[:, ws] = xj[tb - 3:tb, :]

    nc = tb // c_gdn
    glog = -jnp.exp(alog_ref[...]) * jax.nn.softplus(ab + dtb_ref[...])
    beta_all = jax.nn.sigmoid(ab)
    in_chunk = lax.broadcasted_iota(jnp.int32, (tb, LANES), 0) % c_gdn
    gcs = glog
    sh = 1
    while sh < c_gdn:
        gcs = gcs + jnp.where(in_chunk >= sh, pltpu.roll(gcs, sh, axis=0), 0.0)
        sh *= 2

    ri = lax.broadcasted_iota(jnp.int32, (c_gdn, c_gdn), 0)
    ci = lax.broadcasted_iota(jnp.int32, (c_gdn, c_gdn), 1)
    tril = ri >= ci
    strict = ri > ci
    eye_f = jnp.where(ri == ci, 1.0, 0.0).astype(F32)
    n_lv = int(math.ceil(math.log2(c_gdn)))

    names = ("q", "k", "kb", "vb", "kbe", "qe", "kd", "gl", "dec")
    parts = {n: [[None] * HEADS for _ in range(nc)] for n in names}
    for hd in range(HEADS):
        q = u_ref[:, C_GQ + hd * DK:C_GQ + (hd + 1) * DK]
        k = u_ref[:, C_GK + hd * DK:C_GK + (hd + 1) * DK]
        v = u_ref[:, C_GV + hd * DK:C_GV + (hd + 1) * DK]
        q = q * lax.rsqrt(jnp.sum(q * q, axis=-1, keepdims=True) + EPS) * (DK ** -0.5)
        k = k * lax.rsqrt(jnp.sum(k * k, axis=-1, keepdims=True) + EPS)
        gc = jnp.broadcast_to(gcs[:, hd:hd + 1], (tb, LANES))
        beta = jnp.broadcast_to(beta_all[:, HEADS + hd:HEADS + hd + 1], (tb, LANES))
        egc = jnp.exp(gc)
        kb = k * beta
        vb = v * beta
        kbe = kb * egc
        qe = q * egc
        for c in range(nc):
            rs = slice(c * c_gdn, (c + 1) * c_gdn)
            gcc = gc[rs]
            gl = gcc[c_gdn - 1:c_gdn, :]
            diff = gcc[:, :c_gdn] - gcc.T[:c_gdn, :]
            parts["dec"][c][hd] = jnp.where(tril, jnp.exp(jnp.where(tril, diff, 0.0)), 0.0)
            parts["gl"][c][hd] = gl
            parts["kd"][c][hd] = k[rs] * jnp.exp(gl - gcc)
            for n, val in (("q", q), ("k", k), ("kb", kb), ("vb", vb), ("kbe", kbe), ("qe", qe)):
                parts[n][c][hd] = val[rs]
    st = {n: jnp.stack([parts[n][c][hd] for c in range(nc) for hd in range(HEADS)]) for n in names}

    a_low = jnp.where(strict, _bmm_nt(st["kb"], st["k"]) * st["dec"], 0.0)
    att = _bmm_nt(st["q"], st["k"]) * st["dec"]
    pt = jnp.concatenate([-a_low, jnp.broadcast_to(eye_f, a_low.shape)], axis=-1)
    right = lax.broadcasted_iota(jnp.int32, (c_gdn, 2 * c_gdn), 1) >= c_gdn
    for _ in range(n_lv):
        ptb = pt.astype(BF16)
        r = jnp.einsum('nij,njk->nik', ptb[..., :c_gdn], ptb, preferred_element_type=F32)
        pt = r + jnp.where(right, pt, 0.0)
    tinv = pt[..., c_gdn:]
    sol = _bmm(tinv, jnp.concatenate([st["vb"], st["kbe"]], axis=-1))
    uu, ww = sol[..., :DK], sol[..., DK:]

    s = sgdn_ref[...]
    outs = []
    for c in range(nc):
        hs4 = slice(c * HEADS, (c + 1) * HEADS)
        ws_ = _bmm(jnp.concatenate([ww[hs4], st["qe"][hs4]], axis=1), s)
        v_new = uu[hs4] - ws_[:, :c_gdn]
        outs.append(ws_[:, c_gdn:] + _bmm(att[hs4], v_new))
        kv = jnp.einsum('hik,hiv->hkv', st["kd"][hs4].astype(BF16), v_new.astype(BF16),
                        preferred_element_type=F32)
        s = s * jnp.exp(st["gl"][hs4]) + kv
    sgdn_ref[...] = s
    for hd in range(HEADS):
        z = u_ref[:, C_GZ + hd * DK:C_GZ + (hd + 1) * DK]
        o = outs[0][hd] if nc == 1 else jnp.concatenate([outs[c][hd] for c in range(nc)], axis=0)
        on = o * lax.rsqrt(jnp.mean(o * o, axis=-1, keepdims=True) + EPS) * gnorm_ref[...]
        mix_ref[:, 2 * W + hd * DK:2 * W + (hd + 1) * DK] = on * _silu(z)

    out = jnp.dot(mix_ref[...].astype(BF16), wout_ref[...], preferred_element_type=F32)
    r = ALPHA * x + out
    mu = jnp.mean(r, axis=-1, keepdims=True)
    d = r - mu
    var = jnp.mean(d * d, axis=-1, keepdims=True)
    y_ref[...] = d * lax.rsqrt(var + EPS) * lnw_ref[...] + lnb_ref[...]


def _ret_tables(c):
    lg = jnp.log1p(-jnp.exp2(-5.0 - jnp.arange(HEADS, dtype=F32)))[:, None, None]
    idx = jnp.arange(c, dtype=F32)
    diff = idx[:, None] - idx[None, :]
    decay = jnp.where(diff >= 0, jnp.exp(lg * jnp.maximum(diff, 0.0)), 0.0)
    q_dec = jnp.broadcast_to(jnp.exp(lg[:, 0] * (idx + 1.0))[:, :, None], (HEADS, c, DK))
    k_dec = jnp.broadcast_to(jnp.exp(lg[:, 0] * (c - 1.0 - idx))[:, :, None], (HEADS, c, DK))
    s_dec = jnp.broadcast_to(jnp.exp(lg * c), (HEADS, 1, DK))
    return decay, q_dec, k_dec, s_dec


def _rope_tables(pos):
    half = DK // 2
    inv = ROPE_BASE ** (-jnp.arange(half, dtype=F32) / half)
    ang = pos[:, None] * inv[None, :]
    cos, sin = jnp.cos(ang), jnp.sin(ang)
    return jnp.concatenate([cos, cos], -1), jnp.concatenate([-sin, sin], -1)


def _prompt_layer(layer, x, pos, init, wts, *, tb, c_ret, c_gdn):
    bsz, tlen, _ = x.shape
    assert tlen % tb == 0 and tb % c_ret == 0 and tb % c_gdn == 0 and tb % SUBLANES == 0
    nt = tlen // tb
    cos2, sin2 = _rope_tables(pos)
    rtabs = _ret_tables(c_ret)

    def wspec(a):
        nd = a.ndim - 1
        return pl.BlockSpec((None,) + a.shape[1:], lambda b, t, _n=nd: (layer,) + (0,) * _n,
                            pipeline_mode=pl.Buffered(1))

    def cspec(a):
        nd = a.ndim
        return pl.BlockSpec(a.shape, lambda b, t, _n=nd: (0,) * _n, pipeline_mode=pl.Buffered(1))

    def ispec(a):
        nd = a.ndim - 1
        return pl.BlockSpec((None,) + a.shape[1:], lambda b, t, _n=nd: (0,) * (_n + 1),
                            pipeline_mode=pl.Buffered(1))

    def ospec(shape):
        nd = len(shape)
        return pl.BlockSpec((None,) + shape, lambda b, t, _n=nd: (b,) + (0,) * _n)

    in_specs = ([pl.BlockSpec((None, tb, D_MODEL), lambda b, t: (b, t, 0)),
                 pl.BlockSpec((tb, DK), lambda b, t: (t, 0)),
                 pl.BlockSpec((tb, DK), lambda b, t: (t, 0))]
                + [ispec(a) for a in init] + [wspec(a) for a in wts] + [cspec(a) for a in rtabs])
    out_shape = (jax.ShapeDtypeStruct((bsz, tlen, D_MODEL), F32),
                 jax.ShapeDtypeStruct((bsz, 1, W), F32),
                 jax.ShapeDtypeStruct((bsz, CONV_W - 1, W), F32),
                 jax.ShapeDtypeStruct((bsz, HEADS, DK, DK), F32),
                 jax.ShapeDtypeStruct((bsz, CONV_W - 1, 3 * W), F32),
                 jax.ShapeDtypeStruct((bsz, HEADS, DK, DK), F32))
    out_specs = (pl.BlockSpec((None, tb, D_MODEL), lambda b, t: (b, t, 0)),
                 ospec((1, W)), ospec((CONV_W - 1, W)), ospec((HEADS, DK, DK)),
                 ospec((CONV_W - 1, 3 * W)), ospec((HEADS, DK, DK)))
    scratch = [pltpu.VMEM((tb, D_MAIN), F32),
               pltpu.VMEM((SUBLANES, N_CONV), F32),
               pltpu.VMEM((tb, W), F32),
               pltpu.VMEM((tb, 3 * W), F32)]
    kern = functools.partial(_prompt_layer_kernel, tb=tb, c_ret=c_ret, c_gdn=c_gdn)
    return pl.pallas_call(
        kern, grid=(bsz, nt), in_specs=in_specs, out_specs=out_specs, out_shape=out_shape,
        scratch_shapes=scratch,
        compiler_params=pltpu.CompilerParams(dimension_semantics=("arbitrary", "arbitrary"),
                                             vmem_limit_bytes=VMEM_LIMIT_PROMPT),
        name=f"prompt_layer{layer}_t{tlen}",
    )(x, cos2, sin2, *init, *wts, *rtabs)


def _meta_kernel(*refs, n_tok):
    x_ref, body, y_ref, xcarry_ref = refs[0], refs[1:-1], refs[28], refs[-1]

    @pl.when(pl.program_id(0) == 0)
    def _first_layer():
        xcarry_ref[...] = x_ref[...]

    _prompt_layer_kernel(xcarry_ref, *body, tb=n_tok, c_ret=n_tok, c_gdn=n_tok)
    xcarry_ref[...] = y_ref[...]


def _meta_prefix(x, pos, init, wts):
    _, n_tok, _ = x.shape
    depth = wts[0].shape[0]
    cos2, sin2 = _rope_tables(pos)
    rtabs = _ret_tables(n_tok)
    const = lambda a: pl.BlockSpec(a.shape, lambda l, t, _n=a.ndim: (0,) * _n)
    squeeze0 = lambda a: pl.BlockSpec((None,) + a.shape[1:], lambda l, t, _n=a.ndim - 1: (0,) * (_n + 1))
    per_layer = lambda shape: pl.BlockSpec((None,) + shape, lambda l, t, _n=len(shape): (l,) + (0,) * _n)
    in_specs = ([squeeze0(x), const(cos2), const(sin2)] + [squeeze0(a) for a in init]
                + [per_layer(a.shape[1:]) for a in wts] + [const(a) for a in rtabs])
    st_shapes = ((1, W), (CONV_W - 1, W), (HEADS, DK, DK), (CONV_W - 1, 3 * W), (HEADS, DK, DK))
    out_shape = tuple(jax.ShapeDtypeStruct((depth,) + s, F32) for s in ((n_tok, D_MODEL),) + st_shapes)
    out_specs = tuple(per_layer(s) for s in ((n_tok, D_MODEL),) + st_shapes)
    scratch = [pltpu.VMEM((n_tok, D_MAIN), F32), pltpu.VMEM((SUBLANES, N_CONV), F32),
               pltpu.VMEM((n_tok, W), F32), pltpu.VMEM((n_tok, 3 * W), F32),
               pltpu.VMEM((n_tok, D_MODEL), F32)]
    outs = pl.pallas_call(
        functools.partial(_meta_kernel, n_tok=n_tok), grid=(depth, 1),
        in_specs=in_specs, out_specs=out_specs, out_shape=out_shape, scratch_shapes=scratch,
        compiler_params=pltpu.CompilerParams(dimension_semantics=("arbitrary", "arbitrary"),
                                             vmem_limit_bytes=VMEM_LIMIT_META),
        name="meta_prefix",
    )(x, cos2, sin2, *init, *wts, *rtabs)
    return outs[1:]


def _cast_kernel(x_ref, o_ref):
    o_ref[...] = x_ref[...].astype(o_ref.dtype)


def _cast_main_rows(w_t):
    depth, _, d_model = w_t.shape
    rows = D_MAIN // 4
    return pl.pallas_call(
        _cast_kernel, grid=(depth, D_MAIN // rows),
        in_specs=[pl.BlockSpec((None, rows, d_model), lambda l, j: (l, j, 0))],
        out_specs=pl.BlockSpec((None, rows, d_model), lambda l, j: (l, j, 0)),
        out_shape=jax.ShapeDtypeStruct((depth, D_MAIN, d_model), BF16),
        compiler_params=pltpu.CompilerParams(vmem_limit_bytes=VMEM_LIMIT_CAST),
        name="cast_w_in",
    )(w_t)


def _prep_weights(w_in, rg_conv_w, rg_conv_b, rg_w_a, rg_b_a, rg_w_x, rg_b_x, rg_lambda,
                  ret_gn_w, ret_gn_b, gdn_conv_w, gdn_a_log, gdn_dt_bias, gdn_norm_w, w_out, ln_w, ln_b):
    eye = jnp.eye(RG_BLOCKS, dtype=F32)

    def bdiag(w):
        l, n, c, d = w.shape
        return jnp.einsum('lncd,nm->lncmd', w.astype(F32), eye).reshape(l, n * c, n * d)

    pad = LANES - 2 * HEADS
    w_t = jnp.transpose(w_in, (0, 2, 1))
    w_main = _cast_main_rows(w_t)
    w_ab = jnp.pad(w_t[:, D_MAIN:, :], ((0, 0), (0, pad), (0, 0))).astype(BF16)
    wg = jnp.concatenate([bdiag(rg_w_a), bdiag(rg_w_x)], axis=-1).astype(BF16)
    bg = jnp.concatenate([rg_b_a, rg_b_x], axis=-1)[:, None, :].astype(F32)
    row = lambda a: a[:, None, :].astype(F32)
    padh = lambda a: jnp.pad(a.astype(F32), ((0, 0), (0, LANES - HEADS)))[:, None, :]
    return (w_main, w_ab, wg, bg, row(rg_lambda), rg_conv_w.astype(F32), row(rg_conv_b),
            row(ret_gn_w), row(ret_gn_b), gdn_conv_w.astype(F32), padh(gdn_a_log), padh(gdn_dt_bias),
            row(gdn_norm_w), w_out.astype(BF16), row(ln_w), row(ln_b))


def _sample_kernel(
        x_ref, cos_ref, sin_ref, gam_ref,
        h0_ref, rgb0_ref, sret0_ref, gb0_ref, sgdn0_ref,
        win_ref, wab_ref, wg_ref, bg_ref, lam_ref, rcw_ref, rcb_ref, gnw_ref, gnb_ref,
        gcw_ref, alog_ref, dtb_ref, gnorm_ref, wout_ref, lnw_ref, lnb_ref,
        y_ref, h_ref, rgb_ref, sret_ref, gb_ref, sgdn_ref,
        xcur_ref, u_ref, eg_ref, beta_ref, mix_ref,
        *, bb_rows, n_bb):
    layer = pl.program_id(0)
    bb = pl.program_id(1)

    @pl.when(jnp.logical_and(layer == 0, bb == 0))
    def _load_x():
        xcur_ref[...] = x_ref[...]

    @pl.when(bb == 0)
    def _project():
        xb = xcur_ref[...].astype(BF16)
        u_ref[...] = _bdot_nt(xb, win_ref[...])
        ab = _bdot_nt(xb, wab_ref[...])
        eg_ref[...] = jnp.exp(-jnp.exp(alog_ref[...]) * jax.nn.softplus(ab + dtb_ref[...]))
        beta_ref[...] = jax.nn.sigmoid(ab)

        cur = u_ref[:, C_RGX:C_RGX + W]
        xc = rcb_ref[...] + rcw_ref[CONV_W - 1:CONV_W, :] * cur
        for k in range(CONV_W - 1):
            xc = xc + rcw_ref[k:k + 1, :] * rgb0_ref[k]
        rgb_ref[0] = rgb0_ref[1]
        rgb_ref[1] = rgb0_ref[2]
        rgb_ref[2] = cur
        gates = jnp.dot(xc.astype(BF16), wg_ref[...], preferred_element_type=F32) + bg_ref[...]
        r = jax.nn.sigmoid(gates[:, :W])
        i = jax.nn.sigmoid(gates[:, W:])
        log_a = (-RG_C) * r * jax.nn.softplus(-lam_ref[...])
        a = jnp.exp(log_a)
        h = a * h0_ref[...] + _sqrt01(1.0 - a * a) * (i * xc)
        h_ref[...] = h
        mix_ref[:, 0:W] = h * _silu(u_ref[:, C_RGZ:C_RGZ + W])

        cosf = cos_ref[...]
        sinf = sin_ref[...]
        for hd in range(HEADS):
            qs = slice(C_RQ + hd * DK, C_RQ + (hd + 1) * DK)
            ks = slice(C_RK + hd * DK, C_RK + (hd + 1) * DK)
            q = u_ref[:, qs]
            k = u_ref[:, ks]
            u_ref[:, qs] = q * cosf + pltpu.roll(q, DK // 2, axis=1) * sinf
            u_ref[:, ks] = (k * cosf + pltpu.roll(k, DK // 2, axis=1) * sinf) * (DK ** -0.5)

        for j in range(3 * W // LANES):
            cs = slice(C_GQ + j * LANES, C_GQ + (j + 1) * LANES)
            ws = slice(j * LANES, (j + 1) * LANES)
            cur = u_ref[:, cs]
            acc = gcw_ref[CONV_W - 1:CONV_W, ws] * cur
            for k in range(CONV_W - 1):
                acc = acc + gcw_ref[k:k + 1, ws] * gb0_ref[k, :, ws]
            gb_ref[0, :, ws] = gb0_ref[1, :, ws]
            gb_ref[1, :, ws] = gb0_ref[2, :, ws]
            gb_ref[2, :, ws] = cur
            y = _silu(acc)
            if j < 2 * HEADS:
                y = y * lax.rsqrt(jnp.sum(y * y, axis=-1, keepdims=True) + EPS)
                if j < HEADS:
                    y = y * (DK ** -0.5)
            u_ref[:, cs] = y

    r0 = pl.multiple_of(bb * bb_rows, SUBLANES)
    rows = pl.ds(r0, bb_rows)
    egb = eg_ref[rows, :]
    btb = beta_ref[rows, :]
    rid = lax.broadcasted_iota(jnp.int32, (bb_rows, DK), 0)
    rid2 = lax.broadcasted_iota(jnp.int32, (2 * bb_rows, DK), 0)

    def own_rows(x):
        return jnp.concatenate([jnp.where(rid == i, x, 0.0) for i in range(bb_rows)], axis=1)

    for hd in range(HEADS):
        q = u_ref[rows, C_RQ + hd * DK:C_RQ + (hd + 1) * DK]
        k = u_ref[rows, C_RK + hd * DK:C_RK + (hd + 1) * DK]
        v = u_ref[rows, C_RV + hd * DK:C_RV + (hd + 1) * DK]
        z = u_ref[rows, C_RZ + hd * DK:C_RZ + (hd + 1) * DK]
        qk = jnp.sum(q * k, axis=-1, keepdims=True)
        gam = gam_ref[hd]
        qs_ = jnp.zeros((bb_rows, DK), F32)
        for i in range(bb_rows):
            qs_ = jnp.where(rid == i, _bdot(q, sret0_ref[i, hd]), qs_)
        o = qk * v + gam * qs_
        kv = _bdot_tn(k, own_rows(v))
        for i in range(bb_rows):
            sret_ref[i, hd] = gam * sret0_ref[i, hd] + kv[:, i * DK:(i + 1) * DK]
        mu = jnp.mean(o, axis=-1, keepdims=True)
        d = o - mu
        on = d * lax.rsqrt(jnp.mean(d * d, axis=-1, keepdims=True) + EPS)
        gs = slice(hd * DK, (hd + 1) * DK)
        mix_ref[rows, W + hd * DK:W + (hd + 1) * DK] = (on * gnw_ref[:, gs] + gnb_ref[:, gs]) * _silu(z)

        q = u_ref[rows, C_GQ + hd * DK:C_GQ + (hd + 1) * DK]
        k = u_ref[rows, C_GK + hd * DK:C_GK + (hd + 1) * DK]
        v = u_ref[rows, C_GV + hd * DK:C_GV + (hd + 1) * DK]
        z = u_ref[rows, C_GZ + hd * DK:C_GZ + (hd + 1) * DK]
        qk = jnp.sum(q * k, axis=-1, keepdims=True)
        eg = jnp.broadcast_to(egb[:, hd:hd + 1], (bb_rows, DK))
        beta = jnp.broadcast_to(btb[:, HEADS + hd:HEADS + hd + 1], (bb_rows, DK))
        kq = jnp.concatenate([k, q], axis=0)
        kqs = jnp.zeros((2 * bb_rows, DK), F32)
        for i in range(bb_rows):
            kqs = jnp.where(rid2 % bb_rows == i, _bdot(kq, sgdn0_ref[i, hd]), kqs)
        v_new = beta * (v - eg * kqs[:bb_rows])
        o = eg * kqs[bb_rows:] + qk * v_new
        kv = _bdot_tn(k, own_rows(v_new))
        for i in range(bb_rows):
            sgdn_ref[i, hd] = sgdn0_ref[i, hd] * eg[i:i + 1, :] + kv[:, i * DK:(i + 1) * DK]
        on = o * lax.rsqrt(jnp.mean(o * o, axis=-1, keepdims=True) + EPS) * gnorm_ref[...]
        mix_ref[rows, 2 * W + hd * DK:2 * W + (hd + 1) * DK] = on * _silu(z)

    @pl.when(bb == n_bb - 1)
    def _finish():
        out = jnp.dot(mix_ref[...].astype(BF16), wout_ref[...], preferred_element_type=F32)
        r = ALPHA * xcur_ref[...] + out
        mu = jnp.mean(r, axis=-1, keepdims=True)
        d = r - mu
        var = jnp.mean(d * d, axis=-1, keepdims=True)
        y = d * lax.rsqrt(var + EPS) * lnw_ref[...] + lnb_ref[...]
        xcur_ref[...] = y
        y_ref[...] = y


def _sample_path(x, h0, rgb0, sret0, gb0, sgdn0, wts, *, bb_rows):
    nb = x.shape[0]
    n_bb = nb // bb_rows
    pos = jnp.arange(1, dtype=F32) + float(PAST_LEN)
    cos2, sin2 = _rope_tables(pos)
    lg = jnp.log1p(-jnp.exp2(-5.0 - jnp.arange(HEADS, dtype=F32)))
    gam = jnp.broadcast_to(jnp.exp(lg)[:, None, None], (HEADS, 1, DK))

    def const(a):
        nd = a.ndim
        return pl.BlockSpec(a.shape, lambda l, b, _n=nd: (0,) * _n)

    def per_layer(a, prefetch=False):
        nd = a.ndim - 1
        return pl.BlockSpec((None,) + a.shape[1:], lambda l, b, _n=nd: (l,) + (0,) * _n,
                            pipeline_mode=pl.Buffered(2 if prefetch else 1))

    def per_block(a):
        return pl.BlockSpec((None, bb_rows) + a.shape[2:], lambda l, b: (l, b, 0, 0, 0))

    ins = (x, cos2, sin2, gam, h0, rgb0, sret0, gb0, sgdn0) + tuple(wts)
    in_specs = ([const(x), const(cos2), const(sin2), const(gam),
                 per_layer(h0), per_layer(rgb0), per_block(sret0), per_layer(gb0), per_block(sgdn0)]
                + [per_layer(a) for a in wts])
    out_shape = (jax.ShapeDtypeStruct(x.shape, F32),
                 jax.ShapeDtypeStruct(h0.shape, F32), jax.ShapeDtypeStruct(rgb0.shape, F32),
                 jax.ShapeDtypeStruct(sret0.shape, F32), jax.ShapeDtypeStruct(gb0.shape, F32),
                 jax.ShapeDtypeStruct(sgdn0.shape, F32))
    out_specs = (const(x), per_layer(h0), per_layer(rgb0), per_block(sret0), per_layer(gb0), per_block(sgdn0))
    scratch = [pltpu.VMEM((nb, D_MODEL), F32),
               pltpu.VMEM((nb, D_MAIN), F32),
               pltpu.VMEM((nb, LANES), F32),
               pltpu.VMEM((nb, LANES), F32),
               pltpu.VMEM((nb, 3 * W), F32)]
    kern = functools.partial(_sample_kernel, bb_rows=bb_rows, n_bb=n_bb)
    return pl.pallas_call(
        kern, grid=(DEPTH, n_bb), in_specs=in_specs, out_specs=out_specs, out_shape=out_shape,
        scratch_shapes=scratch,
        compiler_params=pltpu.CompilerParams(dimension_semantics=("arbitrary", "arbitrary"),
                                             vmem_limit_bytes=VMEM_LIMIT_SAMPLE),
        name="sample_path",
    )(*ins)


def kernel(x_prompt, x_sample, state_rglru_h, state_rglru_conv, state_ret, state_gdn_conv, state_gdn,
           meta_tokens, w_in, rg_conv_w, rg_conv_b, rg_w_a, rg_b_a, rg_w_x, rg_b_x, rg_lambda,
           ret_gn_w, ret_gn_b, gdn_conv_w, gdn_a_log, gdn_dt_bias, gdn_norm_w, w_out, ln_w, ln_b):
    bp, seq, _ = x_prompt.shape
    nb = x_sample.shape[0]
    wts = _prep_weights(w_in, rg_conv_w, rg_conv_b, rg_w_a, rg_b_a, rg_w_x, rg_b_x, rg_lambda,
                        ret_gn_w, ret_gn_b, gdn_conv_w, gdn_a_log, gdn_dt_bias, gdn_norm_w, w_out, ln_w, ln_b)

    pos = jnp.arange(N_META + seq, dtype=F32)
    zeros = lambda *s: jnp.zeros(s, F32)
    init0 = (zeros(1, 1, W), zeros(1, CONV_W - 1, W), zeros(1, HEADS, DK, DK),
             zeros(1, CONV_W - 1, 3 * W), zeros(1, HEADS, DK, DK))
    st_meta = _meta_prefix(meta_tokens.astype(x_prompt.dtype)[None], pos[:N_META], init0, wts)
    xp = x_prompt
    new_p = [[] for _ in range(5)]
    for l in range(DEPTH):
        st_m = tuple(a[l:l + 1] for a in st_meta)
        xp, *st_p = _prompt_layer(l, xp, pos[N_META:], st_m, wts,
                                  tb=PROMPT_BLOCK, c_ret=RET_CHUNK, c_gdn=GDN_CHUNK)
        for j in range(5):
            new_p[j].append(st_p[j])
    sp = [jnp.stack(a) for a in new_p]
    sp[0] = sp[0].reshape(DEPTH, bp, W)

    tap_major = lambda a: jnp.transpose(a, (0, 2, 1, 3))
    ys, sh, srgb, sret, sgb, sgdn = _sample_path(
        x_sample.reshape(nb, D_MODEL), state_rglru_h, tap_major(state_rglru_conv), state_ret,
        tap_major(state_gdn_conv), state_gdn, wts, bb_rows=SAMPLE_BLOCK)
    return (xp, ys.reshape(x_sample.shape), sp[0], sp[1], sp[2], sp[3], sp[4],
            sh, tap_major(srgb), sret, tap_major(sgb), sgdn)
```

```python
import functools
import math

import jax
import jax.numpy as jnp
from jax import lax
from jax.experimental import pallas as pl
from jax.experimental.pallas import tpu as pltpu

F32 = jnp.float32
BF16 = jnp.bfloat16

D_MODEL = 1024
DEPTH = 4
N_META = 16
PAST_LEN = 16384
W = 512
CONV_W = 4
RG_BLOCKS = 8
RG_C = 8.0
HEADS = 4
DK = W // HEADS
ROPE_BASE = 10000.0
EPS = 1e-6
ALPHA = (2.0 * DEPTH) ** 0.25
D_MAIN = 10 * W
LANES = 128
SUBLANES = 8
N_CONV = 4 * W

PROMPT_BLOCK = 512
RET_CHUNK = 256
GDN_CHUNK = 64
SAMPLE_BLOCK = 2 * SUBLANES
OUT_ROWS = 1024
MIB = 1024 * 1024
VMEM_LIMIT_PROMPT = 56 * MIB
VMEM_LIMIT_SAMPLE = 61 * MIB
VMEM_LIMIT_META = 48 * MIB
VMEM_LIMIT_CAST = 32 * MIB

C_RGX, C_RGZ, C_RQ, C_RK, C_RV, C_RZ, C_GQ, C_GK, C_GV, C_GZ = (i * W for i in range(10))


def _bdot(a, b):
    return jnp.dot(a.astype(BF16), b.astype(BF16), preferred_element_type=F32)


def _bdot_nt(a, b):
    return lax.dot_general(a.astype(BF16), b.astype(BF16), (((1,), (1,)), ((), ())), preferred_element_type=F32)


def _bdot_tn(a, b):
    return lax.dot_general(a.astype(BF16), b.astype(BF16), (((0,), (0,)), ((), ())), preferred_element_type=F32)


def _bmm(a, b):
    return jnp.einsum('nij,njk->nik', a.astype(BF16), b.astype(BF16), preferred_element_type=F32)


def _bmm_nt(a, b):
    return jnp.einsum('nik,njk->nij', a.astype(BF16), b.astype(BF16), preferred_element_type=F32)


def _silu(x):
    return x * jax.nn.sigmoid(x)


def _sqrt01(y):
    return y * lax.rsqrt(jnp.maximum(y, 1e-30))


def _scan_rows(a, b, h0):
    n, lanes = a.shape
    g = n // SUBLANES
    a3 = a.reshape(g, SUBLANES, lanes)
    b3 = b.reshape(g, SUBLANES, lanes)
    sub = lax.broadcasted_iota(jnp.int32, a3.shape, 1)
    s = 1
    while s < SUBLANES:
        m = sub >= s
        a_s = pltpu.roll(a3, s, axis=1)
        b_s = pltpu.roll(b3, s, axis=1)
        b3 = jnp.where(m, a3 * b_s + b3, b3)
        a3 = jnp.where(m, a3 * a_s, a3)
        s *= 2
    carry = h0
    hs = []
    for i in range(g):
        hi = a3[i] * carry + b3[i]
        carry = hi[SUBLANES - 1:SUBLANES, :]
        hs.append(hi)
    return jnp.concatenate(hs, axis=0), carry


def _causal_conv_strip(x, hist, taps):
    assert CONV_W == 4
    r8 = lax.broadcasted_iota(jnp.int32, hist.shape, 0)

    def shift(v, prev, d):
        vs = pltpu.roll(v, d, axis=0)
        top = jnp.where(r8 < d, pltpu.roll(prev, d, axis=0), vs[0:SUBLANES])
        return jnp.concatenate([top, vs[SUBLANES:]], axis=0)

    w0, w1, w2, w3 = taps
    x1 = shift(x, hist, 1)
    z = w1 * x + w0 * x1
    z_hist = w1 * hist + w0 * pltpu.roll(hist, 1, axis=0)
    return w3 * x + w2 * x1 + shift(z, z_hist, 2)


def _prompt_layer_kernel(
        x_ref, cos_ref, sin_ref,
        h0_ref, rgb0_ref, sret0_ref, gb0_ref, sgdn0_ref,
        win_ref, wab_ref, wg_ref, bg_ref, lam_ref, rcw_ref, rcb_ref, gnw_ref, gnb_ref,
        gcw_ref, alog_ref, dtb_ref, gnorm_ref, wout_ref, lnw_ref, lnb_ref,
        rdecay_ref, rqdec_ref, rkdec_ref, rsdec_ref,
        y_ref, h_ref, rgb_ref, sret_ref, gb_ref, sgdn_ref,
        u_ref, hist_ref, xc_ref, mix_ref,
        *, tb, c_ret, c_gdn, fuse_out=True):
    t = pl.program_id(1)

    @pl.when(t == 0)
    def _init():
        h_ref[...] = h0_ref[...]
        sret_ref[...] = sret0_ref[...]
        sgdn_ref[...] = sgdn0_ref[...]
        hist_ref[SUBLANES - 3:SUBLANES, 0:W] = rgb0_ref[...]
        hist_ref[SUBLANES - 3:SUBLANES, W:N_CONV] = gb0_ref[...]

    x = x_ref[...]
    xb = x.astype(BF16)

    def project(c0, c1):
        u_ref[:, c0:c1] = _bdot_nt(xb, win_ref[c0:c1, :])

    project(C_RGX, C_RQ)
    ab = _bdot_nt(xb, wab_ref[...])
    project(C_GQ, D_MAIN)

    for j in range(W // LANES):
        cs = slice(C_RGX + j * LANES, C_RGX + (j + 1) * LANES)
        ws = slice(j * LANES, (j + 1) * LANES)
        xj = u_ref[:, cs]
        taps = [rcw_ref[k:k + 1, ws] for k in range(CONV_W)]
        xc_ref[:, ws] = _causal_conv_strip(xj, hist_ref[:, ws], taps) + rcb_ref[:, ws]
        hist_ref[:, ws] = xj[tb - SUBLANES:tb, :]
        rgb_ref[:, ws] = xj[tb - 3:tb, :]
    gates = jnp.dot(xc_ref[...].astype(BF16), wg_ref[...], preferred_element_type=F32) + bg_ref[...]
    for j in range(W // LANES):
        zs = slice(C_RGZ + j * LANES, C_RGZ + (j + 1) * LANES)
        ws = slice(j * LANES, (j + 1) * LANES)
        r = jax.nn.sigmoid(gates[:, j * LANES:(j + 1) * LANES])
        i = jax.nn.sigmoid(gates[:, W + j * LANES:W + (j + 1) * LANES])
        log_a = (-RG_C) * r * jax.nn.softplus(-lam_ref[:, ws])
        a = jnp.exp(log_a)
        b = _sqrt01(1.0 - a * a) * (i * xc_ref[:, ws])
        h, h_last = _scan_rows(a, b, h_ref[:, ws])
        h_ref[:, ws] = h_last
        mix_ref[:, ws] = h * _silu(u_ref[:, zs])

    project(C_RQ, C_GQ)
    cosf = cos_ref[...]
    sinf = sin_ref[...]
    for hd in range(HEADS):
        q = u_ref[:, C_RQ + hd * DK:C_RQ + (hd + 1) * DK]
        k = u_ref[:, C_RK + hd * DK:C_RK + (hd + 1) * DK]
        v = u_ref[:, C_RV + hd * DK:C_RV + (hd + 1) * DK]
        z = u_ref[:, C_RZ + hd * DK:C_RZ + (hd + 1) * DK]
        q = q * cosf + pltpu.roll(q, DK // 2, axis=1) * sinf
        k = (k * cosf + pltpu.roll(k, DK // 2, axis=1) * sinf) * (DK ** -0.5)
        s = sret_ref[hd]
        outs = []
        for c in range(tb // c_ret):
            rs = slice(c * c_ret, (c + 1) * c_ret)
            qc, kc, vc = q[rs], k[rs], v[rs]
            sc = _bdot_nt(qc, kc) * rdecay_ref[hd]
            outs.append(_bdot(sc, vc) + _bdot(qc * rqdec_ref[hd], s))
            s = s * rsdec_ref[hd] + _bdot_tn(kc * rkdec_ref[hd], vc)
        sret_ref[hd] = s
        o = outs[0] if len(outs) == 1 else jnp.concatenate(outs, axis=0)
        mu = jnp.mean(o, axis=-1, keepdims=True)
        d = o - mu
        on = d * lax.rsqrt(jnp.mean(d * d, axis=-1, keepdims=True) + EPS)
        gs = slice(hd * DK, (hd + 1) * DK)
        mix_ref[:, W + hd * DK:W + (hd + 1) * DK] = (on * gnw_ref[:, gs] + gnb_ref[:, gs]) * _silu(z)

    for j in range(3 * W // LANES):
        cs = slice(C_GQ + j * LANES, C_GQ + (j + 1) * LANES)
        ws = slice(j * LANES, (j + 1) * LANES)
        hs = slice(W + j * LANES, W + (j + 1) * LANES)
        xj = u_ref[:, cs]
        taps = [gcw_ref[k:k + 1, ws] for k in range(CONV_W)]
        u_ref[:, cs] = _silu(_causal_conv_strip(xj, hist_ref[:, hs], taps))
        hist_ref[:, hs] = xj[tb - SUBLANES:tb, :]
        gb_ref[:, ws] = xj[tb - 3:tb, :]

    nc = tb // c_gdn
    glog = -jnp.exp(alog_ref[...]) * jax.nn.softplus(ab + dtb_ref[...])
    beta_all = jax.nn.sigmoid(ab)
    in_chunk = lax.broadcasted_iota(jnp.int32, (tb, LANES), 0) % c_gdn
    gcs = glog
    sh = 1
    while sh < c_gdn:
        gcs = gcs + jnp.where(in_chunk >= sh, pltpu.roll(gcs, sh, axis=0), 0.0)
        sh *= 2

    ri = lax.broadcasted_iota(jnp.int32, (c_gdn, c_gdn), 0)
    ci = lax.broadcasted_iota(jnp.int32, (c_gdn, c_gdn), 1)
    tril = ri >= ci
    strict = ri > ci
    eye_f = jnp.where(ri == ci, 1.0, 0.0).astype(F32)
    n_lv = int(math.ceil(math.log2(c_gdn)))

    names = ("q", "k", "kb", "vb", "kbe", "qe", "kd", "gl", "dec")
    parts = {n: [[None] * HEADS for _ in range(nc)] for n in names}
    for hd in range(HEADS):
        q = u_ref[:, C_GQ + hd * DK:C_GQ + (hd + 1) * DK]
        k = u_ref[:, C_GK + hd * DK:C_GK + (hd + 1) * DK]
        v = u_ref[:, C_GV + hd * DK:C_GV + (hd + 1) * DK]
        q = q * lax.rsqrt(jnp.sum(q * q, axis=-1, keepdims=True) + EPS) * (DK ** -0.5)
        k = k * lax.rsqrt(jnp.sum(k * k, axis=-1, keepdims=True) + EPS)
        gc = jnp.broadcast_to(gcs[:, hd:hd + 1], (tb, LANES))
        beta = jnp.broadcast_to(beta_all[:, HEADS + hd:HEADS + hd + 1], (tb, LANES))
        egc = jnp.exp(gc)
        kb = k * beta
        vb = v * beta
        kbe = kb * egc
        qe = q * egc
        for c in range(nc):
            rs = slice(c * c_gdn, (c + 1) * c_gdn)
            gcc = gc[rs]
            gl = gcc[c_gdn - 1:c_gdn, :]
            diff = gcc[:, :c_gdn] - gcc.T[:c_gdn, :]
            parts["dec"][c][hd] = jnp.where(tril, jnp.exp(jnp.where(tril, diff, 0.0)), 0.0)
            parts["gl"][c][hd] = gl
            parts["kd"][c][hd] = k[rs] * jnp.exp(gl - gcc)
            for n, val in (("q", q), ("k", k), ("kb", kb), ("vb", vb), ("kbe", kbe), ("qe", qe)):
                parts[n][c][hd] = val[rs]
    st = {n: jnp.stack([parts[n][c][hd] for c in range(nc) for hd in range(HEADS)]) for n in names}

    a_low = jnp.where(strict, _bmm_nt(st["kb"], st["k"]) * st["dec"], 0.0)
    att = _bmm_nt(st["q"], st["k"]) * st["dec"]
    pt = jnp.concatenate([-a_low, jnp.broadcast_to(eye_f, a_low.shape)], axis=-1)
    right = lax.broadcasted_iota(jnp.int32, (c_gdn, 2 * c_gdn), 1) >= c_gdn
    for _ in range(n_lv):
        ptb = pt.astype(BF16)
        r = jnp.einsum('nij,njk->nik', ptb[..., :c_gdn], ptb, preferred_element_type=F32)
        pt = r + jnp.where(right, pt, 0.0)
    tinv = pt[..., c_gdn:]
    sol = _bmm(tinv, jnp.concatenate([st["vb"], st["kbe"]], axis=-1))
    uu, ww = sol[..., :DK], sol[..., DK:]

    s = sgdn_ref[...]
    outs = []
    for c in range(nc):
        hs4 = slice(c * HEADS, (c + 1) * HEADS)
        ws_ = _bmm(jnp.concatenate([ww[hs4], st["qe"][hs4]], axis=1), s)
        v_new = uu[hs4] - ws_[:, :c_gdn]
        outs.append(ws_[:, c_gdn:] + _bmm(att[hs4], v_new))
        kv = jnp.einsum('hik,hiv->hkv', st["kd"][hs4].astype(BF16), v_new.astype(BF16),
                        preferred_element_type=F32)
        s = s * jnp.exp(st["gl"][hs4]) + kv
    sgdn_ref[...] = s
    for hd in range(HEADS):
        z = u_ref[:, C_GZ + hd * DK:C_GZ + (hd + 1) * DK]
        o = outs[0][hd] if nc == 1 else jnp.concatenate([outs[c][hd] for c in range(nc)], axis=0)
        on = o * lax.rsqrt(jnp.mean(o * o, axis=-1, keepdims=True) + EPS) * gnorm_ref[...]
        mix_ref[:, 2 * W + hd * DK:2 * W + (hd + 1) * DK] = on * _silu(z)

    if fuse_out:
        y_ref[...] = _outproj_ln(x, mix_ref[...].astype(BF16), wout_ref, lnw_ref, lnb_ref)
    else:
        y_ref[...] = mix_ref[...].astype(y_ref.dtype)


def _outproj_ln(x, mix_b, wout_ref, lnw_ref, lnb_ref):
    out = jnp.dot(mix_b, wout_ref[...], preferred_element_type=F32)
    r = ALPHA * x + out
    mu = jnp.mean(r, axis=-1, keepdims=True)
    d = r - mu
    var = jnp.mean(d * d, axis=-1, keepdims=True)
    return d * lax.rsqrt(var + EPS) * lnw_ref[...] + lnb_ref[...]


def _outproj_ln_kernel(x_ref, m_ref, wout_ref, lnw_ref, lnb_ref, y_ref):
    y_ref[...] = _outproj_ln(x_ref[...], m_ref[...], wout_ref, lnw_ref, lnb_ref)


def _outproj_ln_call(layer, x, mix, w_out, ln_w, ln_b):
    shape = x.shape
    x2 = x.reshape(-1, D_MODEL)
    m2 = mix.reshape(-1, 3 * W)
    of_layer = lambda a: pl.BlockSpec((None,) + a.shape[1:], lambda i, _n=a.ndim - 1: (layer,) + (0,) * _n,
                                      pipeline_mode=pl.Buffered(1))
    y = pl.pallas_call(
        _outproj_ln_kernel, grid=(x2.shape[0] // OUT_ROWS,),
        in_specs=[pl.BlockSpec((OUT_ROWS, D_MODEL), lambda i: (i, 0)),
                  pl.BlockSpec((OUT_ROWS, 3 * W), lambda i: (i, 0)),
                  of_layer(w_out), of_layer(ln_w), of_layer(ln_b)],
        out_specs=pl.BlockSpec((OUT_ROWS, D_MODEL), lambda i: (i, 0)),
        out_shape=jax.ShapeDtypeStruct(x2.shape, F32),
        compiler_params=pltpu.CompilerParams(dimension_semantics=("arbitrary",), vmem_limit_bytes=VMEM_LIMIT_META),
        name=f"outproj_ln{layer}",
    )(x2, m2, w_out, ln_w, ln_b)
    return y.reshape(shape)


def _ret_tables(c):
    lg = jnp.log1p(-jnp.exp2(-5.0 - jnp.arange(HEADS, dtype=F32)))[:, None, None]
    idx = jnp.arange(c, dtype=F32)
    diff = idx[:, None] - idx[None, :]
    decay = jnp.where(diff >= 0, jnp.exp(lg * jnp.maximum(diff, 0.0)), 0.0)
    q_dec = jnp.broadcast_to(jnp.exp(lg[:, 0] * (idx + 1.0))[:, :, None], (HEADS, c, DK))
    k_dec = jnp.broadcast_to(jnp.exp(lg[:, 0] * (c - 1.0 - idx))[:, :, None], (HEADS, c, DK))
    s_dec = jnp.broadcast_to(jnp.exp(lg * c), (HEADS, 1, DK))
    return decay, q_dec, k_dec, s_dec


def _rope_tables(pos):
    half = DK // 2
    inv = ROPE_BASE ** (-jnp.arange(half, dtype=F32) / half)
    ang = pos[:, None] * inv[None, :]
    cos, sin = jnp.cos(ang), jnp.sin(ang)
    return jnp.concatenate([cos, cos], -1), jnp.concatenate([-sin, sin], -1)


def _prompt_layer(layer, x, pos, init, wts, *, tb, c_ret, c_gdn):
    bsz, tlen, _ = x.shape
    assert tlen % tb == 0 and tb % c_ret == 0 and tb % c_gdn == 0 and tb % SUBLANES == 0
    nt = tlen // tb
    cos2, sin2 = _rope_tables(pos)
    rtabs = _ret_tables(c_ret)

    def wspec(a):
        nd = a.ndim - 1
        return pl.BlockSpec((None,) + a.shape[1:], lambda b, t, _n=nd: (layer,) + (0,) * _n,
                            pipeline_mode=pl.Buffered(1))

    def cspec(a):
        nd = a.ndim
        return pl.BlockSpec(a.shape, lambda b, t, _n=nd: (0,) * _n, pipeline_mode=pl.Buffered(1))

    def ispec(a):
        nd = a.ndim - 1
        return pl.BlockSpec((None,) + a.shape[1:], lambda b, t, _n=nd: (0,) * (_n + 1),
                            pipeline_mode=pl.Buffered(1))

    def ospec(shape):
        nd = len(shape)
        return pl.BlockSpec((None,) + shape, lambda b, t, _n=nd: (b,) + (0,) * _n)

    in_specs = ([pl.BlockSpec((None, tb, D_MODEL), lambda b, t: (b, t, 0)),
                 pl.BlockSpec((tb, DK), lambda b, t: (t, 0)),
                 pl.BlockSpec((tb, DK), lambda b, t: (t, 0))]
                + [ispec(a) for a in init] + [wspec(a) for a in wts] + [cspec(a) for a in rtabs])
    out_shape = (jax.ShapeDtypeStruct((bsz, tlen, 3 * W), BF16),
                 jax.ShapeDtypeStruct((bsz, 1, W), F32),
                 jax.ShapeDtypeStruct((bsz, CONV_W - 1, W), F32),
                 jax.ShapeDtypeStruct((bsz, HEADS, DK, DK), F32),
                 jax.ShapeDtypeStruct((bsz, CONV_W - 1, 3 * W), F32),
                 jax.ShapeDtypeStruct((bsz, HEADS, DK, DK), F32))
    out_specs = (pl.BlockSpec((None, tb, 3 * W), lambda b, t: (b, t, 0)),
                 ospec((1, W)), ospec((CONV_W - 1, W)), ospec((HEADS, DK, DK)),
                 ospec((CONV_W - 1, 3 * W)), ospec((HEADS, DK, DK)))
    scratch = [pltpu.VMEM((tb, D_MAIN), F32),
               pltpu.VMEM((SUBLANES, N_CONV), F32),
               pltpu.VMEM((tb, W), F32),
               pltpu.VMEM((tb, 3 * W), F32)]
    kern = functools.partial(_prompt_layer_kernel, tb=tb, c_ret=c_ret, c_gdn=c_gdn, fuse_out=False)
    mix, *states = pl.pallas_call(
        kern, grid=(bsz, nt), in_specs=in_specs, out_specs=out_specs, out_shape=out_shape,
        scratch_shapes=scratch,
        compiler_params=pltpu.CompilerParams(dimension_semantics=("arbitrary", "arbitrary"),
                                             vmem_limit_bytes=VMEM_LIMIT_PROMPT),
        name=f"prompt_layer{layer}_t{tlen}",
    )(x, cos2, sin2, *init, *wts, *rtabs)
    return (_outproj_ln_call(layer, x, mix, wts[13], wts[14], wts[15]), *states)


def _meta_kernel(*refs, n_tok):
    x_ref, body, y_ref, xcarry_ref = refs[0], refs[1:-1], refs[28], refs[-1]

    @pl.when(pl.program_id(0) == 0)
    def _first_layer():
        xcarry_ref[...] = x_ref[...]

    _prompt_layer_kernel(xcarry_ref, *body, tb=n_tok, c_ret=n_tok, c_gdn=n_tok)
    xcarry_ref[...] = y_ref[...]


def _meta_prefix(x, pos, init, wts):
    _, n_tok, _ = x.shape
    depth = wts[0].shape[0]
    cos2, sin2 = _rope_tables(pos)
    rtabs = _ret_tables(n_tok)
    const = lambda a: pl.BlockSpec(a.shape, lambda l, t, _n=a.ndim: (0,) * _n)
    squeeze0 = lambda a: pl.BlockSpec((None,) + a.shape[1:], lambda l, t, _n=a.ndim - 1: (0,) * (_n + 1))
    per_layer = lambda shape: pl.BlockSpec((None,) + shape, lambda l, t, _n=len(shape): (l,) + (0,) * _n)
    in_specs = ([squeeze0(x), const(cos2), const(sin2)] + [squeeze0(a) for a in init]
                + [per_layer(a.shape[1:]) for a in wts] + [const(a) for a in rtabs])
    st_shapes = ((1, W), (CONV_W - 1, W), (HEADS, DK, DK), (CONV_W - 1, 3 * W), (HEADS, DK, DK))
    out_shape = tuple(jax.ShapeDtypeStruct((depth,) + s, F32) for s in ((n_tok, D_MODEL),) + st_shapes)
    out_specs = tuple(per_layer(s) for s in ((n_tok, D_MODEL),) + st_shapes)
    scratch = [pltpu.VMEM((n_tok, D_MAIN), F32), pltpu.VMEM((SUBLANES, N_CONV), F32),
               pltpu.VMEM((n_tok, W), F32), pltpu.VMEM((n_tok, 3 * W), F32),
               pltpu.VMEM((n_tok, D_MODEL), F32)]
    outs = pl.pallas_call(
        functools.partial(_meta_kernel, n_tok=n_tok), grid=(depth, 1),
        in_specs=in_specs, out_specs=out_specs, out_shape=out_shape, scratch_shapes=scratch,
        compiler_params=pltpu.CompilerParams(dimension_semantics=("arbitrary", "arbitrary"),
                                             vmem_limit_bytes=VMEM_LIMIT_META),
        name="meta_prefix",
    )(x, cos2, sin2, *init, *wts, *rtabs)
    return outs[1:]


def _cast_kernel(x_ref, o_ref):
    o_ref[...] = x_ref[...].astype(o_ref.dtype)


def _cast_main_rows(w_t):
    depth, _, d_model = w_t.shape
    rows = D_MAIN // 4
    return pl.pallas_call(
        _cast_kernel, grid=(depth, D_MAIN // rows),
        in_specs=[pl.BlockSpec((None, rows, d_model), lambda l, j: (l, j, 0))],
        out_specs=pl.BlockSpec((None, rows, d_model), lambda l, j: (l, j, 0)),
        out_shape=jax.ShapeDtypeStruct((depth, D_MAIN, d_model), BF16),
        compiler_params=pltpu.CompilerParams(vmem_limit_bytes=VMEM_LIMIT_CAST),
        name="cast_w_in",
    )(w_t)


def _prep_weights(w_in, rg_conv_w, rg_conv_b, rg_w_a, rg_b_a, rg_w_x, rg_b_x, rg_lambda,
                  ret_gn_w, ret_gn_b, gdn_conv_w, gdn_a_log, gdn_dt_bias, gdn_norm_w, w_out, ln_w, ln_b):
    eye = jnp.eye(RG_BLOCKS, dtype=F32)

    def bdiag(w):
        l, n, c, d = w.shape
        return jnp.einsum('lncd,nm->lncmd', w.astype(F32), eye).reshape(l, n * c, n * d)

    pad = LANES - 2 * HEADS
    w_t = jnp.transpose(w_in, (0, 2, 1))
    w_main = _cast_main_rows(w_t)
    w_ab = jnp.pad(w_t[:, D_MAIN:, :], ((0, 0), (0, pad), (0, 0))).astype(BF16)
    wg = jnp.concatenate([bdiag(rg_w_a), bdiag(rg_w_x)], axis=-1).astype(BF16)
    bg = jnp.concatenate([rg_b_a, rg_b_x], axis=-1)[:, None, :].astype(F32)
    row = lambda a: a[:, None, :].astype(F32)
    padh = lambda a: jnp.pad(a.astype(F32), ((0, 0), (0, LANES - HEADS)))[:, None, :]
    return (w_main, w_ab, wg, bg, row(rg_lambda), rg_conv_w.astype(F32), row(rg_conv_b),
            row(ret_gn_w), row(ret_gn_b), gdn_conv_w.astype(F32), padh(gdn_a_log), padh(gdn_dt_bias),
            row(gdn_norm_w), w_out.astype(BF16), row(ln_w), row(ln_b))


def _sample_kernel(
        x_ref, cos_ref, sin_ref, gam_ref,
        h0_ref, rgb0_ref, sret0_ref, gb0_ref, sgdn0_ref,
        win_ref, wab_ref, wg_ref, bg_ref, lam_ref, rcw_ref, rcb_ref, gnw_ref, gnb_ref,
        gcw_ref, alog_ref, dtb_ref, gnorm_ref, wout_ref, lnw_ref, lnb_ref,
        y_ref, h_ref, rgb_ref, sret_ref, gb_ref, sgdn_ref,
        xcur_ref, u_ref, eg_ref, beta_ref, mix_ref,
        *, bb_rows, n_bb):
    layer = pl.program_id(0)
    bb = pl.program_id(1)

    @pl.when(jnp.logical_and(layer == 0, bb == 0))
    def _load_x():
        xcur_ref[...] = x_ref[...]

    @pl.when(bb == 0)
    def _project():
        xb = xcur_ref[...].astype(BF16)
        u_ref[...] = _bdot_nt(xb, win_ref[...])
        ab = _bdot_nt(xb, wab_ref[...])
        eg_ref[...] = jnp.exp(-jnp.exp(alog_ref[...]) * jax.nn.softplus(ab + dtb_ref[...]))
        beta_ref[...] = jax.nn.sigmoid(ab)

        cur = u_ref[:, C_RGX:C_RGX + W]
        xc = rcb_ref[...] + rcw_ref[CONV_W - 1:CONV_W, :] * cur
        for k in range(CONV_W - 1):
            xc = xc + rcw_ref[k:k + 1, :] * rgb0_ref[k]
        rgb_ref[0] = rgb0_ref[1]
        rgb_ref[1] = rgb0_ref[2]
        rgb_ref[2] = cur
        gates = jnp.dot(xc.astype(BF16), wg_ref[...], preferred_element_type=F32) + bg_ref[...]
        r = jax.nn.sigmoid(gates[:, :W])
        i = jax.nn.sigmoid(gates[:, W:])
        log_a = (-RG_C) * r * jax.nn.softplus(-lam_ref[...])
        a = jnp.exp(log_a)
        h = a * h0_ref[...] + _sqrt01(1.0 - a * a) * (i * xc)
        h_ref[...] = h
        mix_ref[:, 0:W] = h * _silu(u_ref[:, C_RGZ:C_RGZ + W])

        cosf = cos_ref[...]
        sinf = sin_ref[...]
        for hd in range(HEADS):
            qs = slice(C_RQ + hd * DK, C_RQ + (hd + 1) * DK)
            ks = slice(C_RK + hd * DK, C_RK + (hd + 1) * DK)
            q = u_ref[:, qs]
            k = u_ref[:, ks]
            u_ref[:, qs] = q * cosf + pltpu.roll(q, DK // 2, axis=1) * sinf
            u_ref[:, ks] = (k * cosf + pltpu.roll(k, DK // 2, axis=1) * sinf) * (DK ** -0.5)

        for j in range(3 * W // LANES):
            cs = slice(C_GQ + j * LANES, C_GQ + (j + 1) * LANES)
            ws = slice(j * LANES, (j + 1) * LANES)
            cur = u_ref[:, cs]
            acc = gcw_ref[CONV_W - 1:CONV_W, ws] * cur
            for k in range(CONV_W - 1):
                acc = acc + gcw_ref[k:k + 1, ws] * gb0_ref[k, :, ws]
            gb_ref[0, :, ws] = gb0_ref[1, :, ws]
            gb_ref[1, :, ws] = gb0_ref[2, :, ws]
            gb_ref[2, :, ws] = cur
            y = _silu(acc)
            if j < 2 * HEADS:
                y = y * lax.rsqrt(jnp.sum(y * y, axis=-1, keepdims=True) + EPS)
                if j < HEADS:
                    y = y * (DK ** -0.5)
            u_ref[:, cs] = y

    r0 = pl.multiple_of(bb * bb_rows, SUBLANES)
    rows = pl.ds(r0, bb_rows)
    egb = eg_ref[rows, :]
    btb = beta_ref[rows, :]
    rid = lax.broadcasted_iota(jnp.int32, (bb_rows, DK), 0)
    rid2 = lax.broadcasted_iota(jnp.int32, (2 * bb_rows, DK), 0)

    def own_rows(x):
        return jnp.concatenate([jnp.where(rid == i, x, 0.0) for i in range(bb_rows)], axis=1)

    for hd in range(HEADS):
        q = u_ref[rows, C_RQ + hd * DK:C_RQ + (hd + 1) * DK]
        k = u_ref[rows, C_RK + hd * DK:C_RK + (hd + 1) * DK]
        v = u_ref[rows, C_RV + hd * DK:C_RV + (hd + 1) * DK]
        z = u_ref[rows, C_RZ + hd * DK:C_RZ + (hd + 1) * DK]
        qk = jnp.sum(q * k, axis=-1, keepdims=True)
        gam = gam_ref[hd]
        qs_ = jnp.zeros((bb_rows, DK), F32)
        for i in range(bb_rows):
            qs_ = jnp.where(rid == i, _bdot(q, sret0_ref[i, hd]), qs_)
        o = qk * v + gam * qs_
        kv = _bdot_tn(k, own_rows(v))
        for i in range(bb_rows):
            sret_ref[i, hd] = gam * sret0_ref[i, hd] + kv[:, i * DK:(i + 1) * DK]
        mu = jnp.mean(o, axis=-1, keepdims=True)
        d = o - mu
        on = d * lax.rsqrt(jnp.mean(d * d, axis=-1, keepdims=True) + EPS)
        gs = slice(hd * DK, (hd + 1) * DK)
        mix_ref[rows, W + hd * DK:W + (hd + 1) * DK] = (on * gnw_ref[:, gs] + gnb_ref[:, gs]) * _silu(z)

        q = u_ref[rows, C_GQ + hd * DK:C_GQ + (hd + 1) * DK]
        k = u_ref[rows, C_GK + hd * DK:C_GK + (hd + 1) * DK]
        v = u_ref[rows, C_GV + hd * DK:C_GV + (hd + 1) * DK]
        z = u_ref[rows, C_GZ + hd * DK:C_GZ + (hd + 1) * DK]
        qk = jnp.sum(q * k, axis=-1, keepdims=True)
        eg = jnp.broadcast_to(egb[:, hd:hd + 1], (bb_rows, DK))
        beta = jnp.broadcast_to(btb[:, HEADS + hd:HEADS + hd + 1], (bb_rows, DK))
        kq = jnp.concatenate([k, q], axis=0)
        kqs = jnp.zeros((2 * bb_rows, DK), F32)
        for i in range(bb_rows):
            kqs = jnp.where(rid2 % bb_rows == i, _bdot(kq, sgdn0_ref[i, hd]), kqs)
        v_new = beta * (v - eg * kqs[:bb_rows])
        o = eg * kqs[bb_rows:] + qk * v_new
        kv = _bdot_tn(k, own_rows(v_new))
        for i in range(bb_rows):
            sgdn_ref[i, hd] = sgdn0_ref[i, hd] * eg[i:i + 1, :] + kv[:, i * DK:(i + 1) * DK]
        on = o * lax.rsqrt(jnp.mean(o * o, axis=-1, keepdims=True) + EPS) * gnorm_ref[...]
        mix_ref[rows, 2 * W + hd * DK:2 * W + (hd + 1) * DK] = on * _silu(z)

    @pl.when(bb == n_bb - 1)
    def _finish():
        out = jnp.dot(mix_ref[...].astype(BF16), wout_ref[...], preferred_element_type=F32)
        r = ALPHA * xcur_ref[...] + out
        mu = jnp.mean(r, axis=-1, keepdims=True)
        d = r - mu
        var = jnp.mean(d * d, axis=-1, keepdims=True)
        y = d * lax.rsqrt(var + EPS) * lnw_ref[...] + lnb_ref[...]
        xcur_ref[...] = y
        y_ref[...] = y


def _sample_path(x, h0, rgb0, sret0, gb0, sgdn0, wts, *, bb_rows):
    nb = x.shape[0]
    n_bb = nb // bb_rows
    pos = jnp.arange(1, dtype=F32) + float(PAST_LEN)
    cos2, sin2 = _rope_tables(pos)
    lg = jnp.log1p(-jnp.exp2(-5.0 - jnp.arange(HEADS, dtype=F32)))
    gam = jnp.broadcast_to(jnp.exp(lg)[:, None, None], (HEADS, 1, DK))

    def const(a):
        nd = a.ndim
        return pl.BlockSpec(a.shape, lambda l, b, _n=nd: (0,) * _n)

    def per_layer(a, prefetch=False):
        nd = a.ndim - 1
        return pl.BlockSpec((None,) + a.shape[1:], lambda l, b, _n=nd: (l,) + (0,) * _n,
                            pipeline_mode=pl.Buffered(2 if prefetch else 1))

    def per_block(a):
        return pl.BlockSpec((None, bb_rows) + a.shape[2:], lambda l, b: (l, b, 0, 0, 0))

    ins = (x, cos2, sin2, gam, h0, rgb0, sret0, gb0, sgdn0) + tuple(wts)
    in_specs = ([const(x), const(cos2), const(sin2), const(gam),
                 per_layer(h0), per_layer(rgb0), per_block(sret0), per_layer(gb0), per_block(sgdn0)]
                + [per_layer(a) for a in wts])
    out_shape = (jax.ShapeDtypeStruct(x.shape, F32),
                 jax.ShapeDtypeStruct(h0.shape, F32), jax.ShapeDtypeStruct(rgb0.shape, F32),
                 jax.ShapeDtypeStruct(sret0.shape, F32), jax.ShapeDtypeStruct(gb0.shape, F32),
                 jax.ShapeDtypeStruct(sgdn0.shape, F32))
    out_specs = (const(x), per_layer(h0), per_layer(rgb0), per_block(sret0), per_layer(gb0), per_block(sgdn0))
    scratch = [pltpu.VMEM((nb, D_MODEL), F32),
               pltpu.VMEM((nb, D_MAIN), F32),
               pltpu.VMEM((nb, LANES), F32),
               pltpu.VMEM((nb, LANES), F32),
               pltpu.VMEM((nb, 3 * W), F32)]
    kern = functools.partial(_sample_kernel, bb_rows=bb_rows, n_bb=n_bb)
    return pl.pallas_call(
        kern, grid=(DEPTH, n_bb), in_specs=in_specs, out_specs=out_specs, out_shape=out_shape,
        scratch_shapes=scratch,
        compiler_params=pltpu.CompilerParams(dimension_semantics=("arbitrary", "arbitrary"),
                                             vmem_limit_bytes=VMEM_LIMIT_SAMPLE),
        name="sample_path",
    )(*ins)


def kernel(x_prompt, x_sample, state_rglru_h, state_rglru_conv, state_ret, state_gdn_conv, state_gdn,
           meta_tokens, w_in, rg_conv_w, rg_conv_b, rg_w_a, rg_b_a, rg_w_x, rg_b_x, rg_lambda,
           ret_gn_w, ret_gn_b, gdn_conv_w, gdn_a_log, gdn_dt_bias, gdn_norm_w, w_out, ln_w, ln_b):
    bp, seq, _ = x_prompt.shape
    nb = x_sample.shape[0]
    wts = _prep_weights(w_in, rg_conv_w, rg_conv_b, rg_w_a, rg_b_a, rg_w_x, rg_b_x, rg_lambda,
                        ret_gn_w, ret_gn_b, gdn_conv_w, gdn_a_log, gdn_dt_bias, gdn_norm_w, w_out, ln_w, ln_b)

    pos = jnp.arange(N_META + seq, dtype=F32)
    zeros = lambda *s: jnp.zeros(s, F32)
    init0 = (zeros(1, 1, W), zeros(1, CONV_W - 1, W), zeros(1, HEADS, DK, DK),
             zeros(1, CONV_W - 1, 3 * W), zeros(1, HEADS, DK, DK))
    st_meta = _meta_prefix(meta_tokens.astype(x_prompt.dtype)[None], pos[:N_META], init0, wts)
    xp = x_prompt
    new_p = [[] for _ in range(5)]
    for l in range(DEPTH):
        st_m = tuple(a[l:l + 1] for a in st_meta)
        xp, *st_p = _prompt_layer(l, xp, pos[N_META:], st_m, wts,
                                  tb=PROMPT_BLOCK, c_ret=RET_CHUNK, c_gdn=GDN_CHUNK)
        for j in range(5):
            new_p[j].append(st_p[j])
    sp = [jnp.stack(a) for a in new_p]
    sp[0] = sp[0].reshape(DEPTH, bp, W)

    tap_major = lambda a: jnp.transpose(a, (0, 2, 1, 3))
    ys, sh, srgb, sret, sgb, sgdn = _sample_path(
        x_sample.reshape(nb, D_MODEL), state_rglru_h, tap_major(state_rglru_conv), state_ret,
        tap_major(state_gdn_conv), state_gdn, wts, bb_rows=SAMPLE_BLOCK)
    return (xp, ys.reshape(x_sample.shape), sp[0], sp[1], sp[2], sp[3], sp[4],
            sh, tap_major(srgb), sret, tap_major(sgb), sgdn)
```
